```python
import math
import jax, jax.numpy as jnp
from jax import lax
import numpy as np

D_MODEL = 1024
BATCH = 32
SEQ = 2048
DEPTH = 1
DEC_BATCH = 8
DEC_SEQ = 4096
PAST_LEN = 128

GRID_W = 64
ROPE_THETA = 10000.0
QBLOCK = 128
EPS = 1e-6
MLA_HEADS = 8
Q_LORA = 384
KV_LORA = 256
MLA_NOPE = 64
MLA_ROPE = 32
MLA_V = 64
GQA_HEADS = 8
GQA_KV_HEADS = 2
GQA_HD = 64
N_MEM = 256
XA_HEADS = 4
XA_HD = 128
N_EXPERTS = 16
EC_FACTOR = 2
MOE_FF = 1024
IN_SPLITS = (Q_LORA, KV_LORA, MLA_ROPE, GQA_HEADS * GQA_HD, GQA_KV_HEADS * GQA_HD, GQA_KV_HEADS * GQA_HD, D_MODEL, D_MODEL)
IN_COLS = Q_LORA + KV_LORA + MLA_ROPE + GQA_HEADS * GQA_HD + 2 * GQA_KV_HEADS * GQA_HD + 2 * D_MODEL

kernel_name = "hybrid_mla_gqa_axial_ec_moe_encoder"


def rms_norm(x, g):
    xf = x.astype(jnp.float32)
    y = xf * lax.rsqrt(jnp.mean(xf * xf, axis=-1, keepdims=True) + EPS)
    return (y * g.astype(jnp.float32)).astype(x.dtype)


def split_cols(z, sizes):
    outs = []
    off = 0
    for s in sizes:
        outs.append(z[..., off:off + s])
        off += s
    return outs


def axial_rope_tables(S, rot_dim):
    rows = S // GRID_W
    n_ax = rot_dim // 4
    freqs = 1.0 / (ROPE_THETA ** (jnp.arange(n_ax, dtype=jnp.float32) / n_ax))
    row = jnp.repeat(jnp.arange(rows, dtype=jnp.float32), GRID_W)
    col = jnp.tile(jnp.arange(GRID_W, dtype=jnp.float32), rows)
    ang = jnp.concatenate([row[:, None] * freqs, col[:, None] * freqs], axis=-1)
    return jnp.cos(ang), jnp.sin(ang)


def apply_rope(x, cos, sin):
    xf = x.astype(jnp.float32)
    half = x.shape[-1] // 2
    x1, x2 = xf[..., :half], xf[..., half:]
    c = cos[None, :, None, :]
    s = sin[None, :, None, :]
    return jnp.concatenate([x1 * c - x2 * s, x1 * s + x2 * c], axis=-1).astype(x.dtype)


def block_attention(q, k, v, scale):
    B, S, G, R, Dk = q.shape
    Dv = v.shape[-1]
    nb = S // QBLOCK
    qb = q.reshape(B, nb, QBLOCK, G, R, Dk).transpose(1, 0, 2, 3, 4, 5)

    def one_block(qi):
        s = jnp.einsum('bqgrd,bkgd->bgrqk', qi, k).astype(jnp.float32) * scale
        p = jax.nn.softmax(s, axis=-1).astype(v.dtype)
        return jnp.einsum('bgrqk,bkgd->bqgrd', p, v)

    o = lax.map(one_block, qb)
    return o.transpose(1, 0, 2, 3, 4, 5).reshape(B, S, G * R * Dv)


def encoder_layer(x, mem, g_mix, w_in, mla_gq, mla_w_uq, mla_gkv, mla_w_ukv, mla_gqn, mla_gkn, mla_w_br,
                  gqa_gqn, gqa_gkn, gqa_w_br, w_mix_out, g_cross, g_mem, xa_w_q, xa_w_kv, xa_gqn, xa_gkn,
                  xa_w_o, g_moe, w_router, moe_w_gate, moe_w_up, moe_w_down):
    B, S, D = x.shape
    h = rms_norm(x, g_mix)
    z = h @ w_in
    c_q, c_kv, k_rope, q_b, k_b, v_b, gate_a, gate_b = split_cols(z, IN_SPLITS)
    cos_a, sin_a = axial_rope_tables(S, MLA_ROPE)
    cos_b, sin_b = axial_rope_tables(S, GQA_HD)

    q_a = (rms_norm(c_q, mla_gq) @ mla_w_uq).reshape(B, S, MLA_HEADS, MLA_NOPE + MLA_ROPE)
    kv_a = (rms_norm(c_kv, mla_gkv) @ mla_w_ukv).reshape(B, S, MLA_HEADS, MLA_NOPE + MLA_V)
    q_nope = rms_norm(q_a[..., :MLA_NOPE], mla_gqn[:MLA_NOPE])
    q_rot = apply_rope(rms_norm(q_a[..., MLA_NOPE:], mla_gqn[MLA_NOPE:]), cos_a, sin_a)
    k_nope = rms_norm(kv_a[..., :MLA_NOPE], mla_gkn[:MLA_NOPE])
    v_a = kv_a[..., MLA_NOPE:]
    k_rot = apply_rope(rms_norm(k_rope, mla_gkn[MLA_NOPE:])[:, :, None, :], cos_a, sin_a)
    q_full = jnp.concatenate([q_nope, q_rot], axis=-1)[:, :, :, None, :]
    k_full = jnp.concatenate([k_nope, jnp.broadcast_to(k_rot, (B, S, MLA_HEADS, MLA_ROPE))], axis=-1)
    o_a = block_attention(q_full, k_full, v_a, (MLA_NOPE + MLA_ROPE) ** -0.5)

    q_g = apply_rope(rms_norm(q_b.reshape(B, S, GQA_HEADS, GQA_HD), gqa_gqn), cos_b, sin_b)
    k_g = apply_rope(rms_norm(k_b.reshape(B, S, GQA_KV_HEADS, GQA_HD), gqa_gkn), cos_b, sin_b)
    v_g = v_b.reshape(B, S, GQA_KV_HEADS, GQA_HD)
    q_g = q_g.reshape(B, S, GQA_KV_HEADS, GQA_HEADS // GQA_KV_HEADS, GQA_HD)
    o_b = block_attention(q_g, k_g, v_g, GQA_HD ** -0.5)

    mixed = jax.nn.sigmoid(gate_a) * (o_a @ mla_w_br) + jax.nn.sigmoid(gate_b) * (o_b @ gqa_w_br)
    x = x + mixed @ w_mix_out

    hc = rms_norm(x, g_cross)
    m = rms_norm(mem, g_mem)
    M = m.shape[1]
    qx = rms_norm((hc @ xa_w_q).reshape(B, S, XA_HEADS, XA_HD), xa_gqn)
    kvx = m @ xa_w_kv
    kx = rms_norm(kvx[..., :XA_HEADS * XA_HD].reshape(B, M, XA_HEADS, XA_HD), xa_gkn)
    vx = kvx[..., XA_HEADS * XA_HD:].reshape(B, M, XA_HEADS, XA_HD)
    sx = jnp.einsum('bqhd,bkhd->bhqk', qx, kx).astype(jnp.float32) * (XA_HD ** -0.5)
    px = jax.nn.softmax(sx, axis=-1).astype(vx.dtype)
    ox = jnp.einsum('bhqk,bkhd->bqhd', px, vx).reshape(B, S, XA_HEADS * XA_HD)
    x = x + ox @ xa_w_o

    N = B * S
    cap = EC_FACTOR * N // N_EXPERTS
    hm = rms_norm(x, g_moe).reshape(N, D)
    affinity = jax.nn.softmax((hm @ w_router).astype(jnp.float32), axis=-1)
    vals, idx = lax.top_k(affinity.T, cap)
    xe = hm[idx]
    hid = jax.nn.silu(jnp.einsum('ecd,edf->ecf', xe, moe_w_gate)) * jnp.einsum('ecd,edf->ecf', xe, moe_w_up)
    out = jnp.einsum('ecf,efd->ecd', hid, moe_w_down) * vals[..., None].astype(hid.dtype)
    y = jnp.zeros((N, D), dtype=x.dtype).at[idx.reshape(-1)].add(out.reshape(-1, D).astype(x.dtype))
    x = x + y.reshape(B, S, D)
    return x


def setup_inputs(seed: int = 0) -> dict:
    key = jax.random.key(seed)
    ks = jax.random.split(key, 32)
    f32 = jnp.float32

    def nrm(k, shape, fan_in):
        return jax.random.normal(k, shape, f32) * (fan_in ** -0.5)

    def gain(k, n):
        return 1.0 + 0.02 * jax.random.normal(k, (DEPTH, n), f32)

    L = DEPTH
    return {
        "x_prompt": jax.random.normal(ks[0], (BATCH, SEQ, D_MODEL), f32),
        "x_sample": jax.random.normal(ks[1], (DEC_BATCH, DEC_SEQ, D_MODEL), f32),
        "mem_prompt": jax.random.normal(ks[2], (BATCH, N_MEM, D_MODEL), f32),
        "mem_sample": jax.random.normal(ks[3], (DEC_BATCH, N_MEM, D_MODEL), f32),
        "g_mix": gain(ks[4], D_MODEL),
        "w_in": nrm(ks[5], (L, D_MODEL, IN_COLS), D_MODEL),
        "mla_gq": gain(ks[6], Q_LORA),
        "mla_w_uq": nrm(ks[7], (L, Q_LORA, MLA_HEADS * (MLA_NOPE + MLA_ROPE)), Q_LORA),
        "mla_gkv": gain(ks[8], KV_LORA),
        "mla_w_ukv": nrm(ks[9], (L, KV_LORA, MLA_HEADS * (MLA_NOPE + MLA_V)), KV_LORA),
        "mla_gqn": gain(ks[10], MLA_NOPE + MLA_ROPE),
        "mla_gkn": gain(ks[11], MLA_NOPE + MLA_ROPE),
        "mla_w_br": nrm(ks[12], (L, MLA_HEADS * MLA_V, D_MODEL), MLA_HEADS * MLA_V),
        "gqa_gqn": gain(ks[13], GQA_HD),
        "gqa_gkn": gain(ks[14], GQA_HD),
        "gqa_w_br": nrm(ks[15], (L, GQA_HEADS * GQA_HD, D_MODEL), GQA_HEADS * GQA_HD),
        "w_mix_out": nrm(ks[16], (L, D_MODEL, D_MODEL), D_MODEL),
        "g_cross": gain(ks[17], D_MODEL),
        "g_mem": gain(ks[18], D_MODEL),
        "xa_w_q": nrm(ks[19], (L, D_MODEL, XA_HEADS * XA_HD), D_MODEL),
        "xa_w_kv": nrm(ks[20], (L, D_MODEL, 2 * XA_HEADS * XA_HD), D_MODEL),
        "xa_gqn": gain(ks[21], XA_HD),
        "xa_gkn": gain(ks[22], XA_HD),
        "xa_w_o": nrm(ks[23], (L, XA_HEADS * XA_HD, D_MODEL), XA_HEADS * XA_HD),
        "g_moe": gain(ks[24], D_MODEL),
        "w_router": nrm(ks[25], (L, D_MODEL, N_EXPERTS), D_MODEL),
        "moe_w_gate": nrm(ks[26], (L, N_EXPERTS, D_MODEL, MOE_FF), D_MODEL),
        "moe_w_up": nrm(ks[27], (L, N_EXPERTS, D_MODEL, MOE_FF), D_MODEL),
        "moe_w_down": nrm(ks[28], (L, N_EXPERTS, MOE_FF, D_MODEL), MOE_FF),
    }


def reference(x_prompt, x_sample, mem_prompt, mem_sample, g_mix, w_in, mla_gq, mla_w_uq, mla_gkv, mla_w_ukv,
              mla_gqn, mla_gkn, mla_w_br, gqa_gqn, gqa_gkn, gqa_w_br, w_mix_out, g_cross, g_mem, xa_w_q,
              xa_w_kv, xa_gqn, xa_gkn, xa_w_o, g_moe, w_router, moe_w_gate, moe_w_up, moe_w_down):
    layer_params = (g_mix, w_in, mla_gq, mla_w_uq, mla_gkv, mla_w_ukv, mla_gqn, mla_gkn, mla_w_br,
                    gqa_gqn, gqa_gkn, gqa_w_br, w_mix_out, g_cross, g_mem, xa_w_q, xa_w_kv, xa_gqn, xa_gkn,
                    xa_w_o, g_moe, w_router, moe_w_gate, moe_w_up, moe_w_down)
    y_prompt = x_prompt
    y_sample = x_sample
    for l in range(DEPTH):
        lp = [p[l] for p in layer_params]
        y_prompt = encoder_layer(y_prompt, mem_prompt, *lp)
        y_sample = encoder_layer(y_sample, mem_sample, *lp)
    return (y_prompt, y_sample)
```

```python
import functools
import math

import jax
import jax.numpy as jnp
from jax import lax
from jax.experimental import pallas as pl
from jax.experimental.pallas import tpu as pltpu

EPS = 1e-6
GRID_W = 64
ROPE_THETA = 10000.0
MLA_HEADS = 8
Q_LORA = 384
KV_LORA = 256
MLA_NOPE = 64
MLA_ROPE = 32
MLA_V = 64
GQA_HEADS = 8
GQA_KV_HEADS = 2
GQA_HD = 64
XA_HEADS = 4
XA_HD = 128
N_EXPERTS = 16
EC_FACTOR = 2

LANE = 128
MXU = 256
VMEM_LIMIT = 56 * 1024 * 1024

F32 = jnp.float32
BF16 = jnp.bfloat16


def _pick(n, pref):
    t = min(n, pref)
    while n % t:
        t -= LANE
    return t


def _rms_rows(x, g):
    ms = jnp.mean(x * x, axis=-1, keepdims=True)
    return x * lax.rsqrt(ms + EPS) * g


def _group_mean_sq(x, bd):
    x2 = x * x
    hi = x2.astype(BF16)
    lo = (x2 - hi.astype(F32)).astype(BF16)
    w = x.shape[1]
    outs = []
    for c in range(0, w, MXU):
        cw = min(MXU, w - c)
        b = bd[:cw, :cw]
        outs.append(jnp.dot(hi[:, c:c + cw], b, preferred_element_type=F32)
                    + jnp.dot(lo[:, c:c + cw], b, preferred_element_type=F32))
    return outs[0] if len(outs) == 1 else jnp.concatenate(outs, axis=1)


def _group_rms(x, g, bd):
    return x * lax.rsqrt(_group_mean_sq(x, bd) + EPS) * g


def _rope_slabs(x, tab, shift):
    c, s1, s2 = tab[0], tab[1], tab[2]
    outs = []
    for a in range(0, x.shape[1], LANE):
        xs = x[:, a:a + LANE]
        outs.append(xs * c + pltpu.roll(xs, LANE - shift, 1) * s1 + pltpu.roll(xs, shift, 1) * s2)
    return outs[0] if len(outs) == 1 else jnp.concatenate(outs, axis=1)


def _sigmoid(x):
    return 1.0 / (1.0 + jnp.exp(-x))


_SEG_CQ = (0, Q_LORA)
_SEG_CKV = (_SEG_CQ[1], _SEG_CQ[1] + KV_LORA)
_SEG_KR = (_SEG_CKV[1], _SEG_CKV[1] + LANE)
_SEG_QB = (_SEG_KR[1], _SEG_KR[1] + GQA_HEADS * GQA_HD)
_SEG_KB = (_SEG_QB[1], _SEG_QB[1] + GQA_KV_HEADS * LANE)
_SEG_VB = (_SEG_KB[1], _SEG_KB[1] + GQA_KV_HEADS * LANE)
_W_SLAB = MLA_HEADS * LANE


def _mixer_in_kernel(x_ref, gmix_ref, win_ref, gq_ref, wuq_ref, gkv_ref, wukv_ref,
                     gqa_ref, gka_ref, gkr_ref, gqg_ref, gkg_ref, bda_ref, bdb_ref, taba_ref, tabb_ref,
                     qa_ref, ka_ref, va_ref, qg_ref, kg_ref, vg_ref, ga_ref, gb_ref, *, d_model):
    h = _rms_rows(x_ref[...], gmix_ref[...]).astype(BF16)

    def proj(seg):
        return jnp.dot(h, win_ref[:, seg[0]:seg[1]], preferred_element_type=F32)

    bda = bda_ref[...]
    bdb = bdb_ref[...]
    taba = taba_ref[...]
    tabb = tabb_ref[...]

    cq = _rms_rows(proj(_SEG_CQ), gq_ref[...]).astype(BF16)
    qa = jnp.dot(cq, wuq_ref[...], preferred_element_type=F32)
    qa = _rope_slabs(_group_rms(qa, gqa_ref[...], bda), taba, MLA_ROPE // 2)
    qa_ref[...] = (qa * ((MLA_NOPE + MLA_ROPE) ** -0.5)).astype(BF16)

    ckv = _rms_rows(proj(_SEG_CKV), gkv_ref[...]).astype(BF16)
    kva = jnp.dot(ckv, wukv_ref[...], preferred_element_type=F32)
    kn = _group_rms(kva[:, :_W_SLAB], gka_ref[...], bdb)
    kr = _rope_slabs(_group_rms(proj(_SEG_KR), gkr_ref[...], bda), taba, MLA_ROPE // 2)
    ka_ref[...] = (kn + jnp.concatenate([kr] * MLA_HEADS, axis=1)).astype(BF16)
    va_ref[...] = kva[:, _W_SLAB:].astype(BF16)

    qg = _rope_slabs(_group_rms(proj(_SEG_QB), gqg_ref[...], bdb), tabb, GQA_HD // 2) * (GQA_HD ** -0.5)
    lane = lax.broadcasted_iota(jnp.int32, (1, LANE), 1)
    low = lane < GQA_HD
    parts = []
    for p in range(GQA_HEADS // 2):
        s = qg[:, p * LANE:(p + 1) * LANE]
        parts.append(jnp.where(low, s, 0.0))
        parts.append(jnp.where(low, 0.0, s))
    qg_ref[...] = jnp.concatenate(parts, axis=1).astype(BF16)
    kg = _rope_slabs(_group_rms(proj(_SEG_KB), gkg_ref[...], bdb), tabb, GQA_HD // 2)
    kg_ref[...] = kg.astype(BF16)
    vg_ref[...] = proj(_SEG_VB).astype(BF16)

    g0 = _SEG_VB[1]
    ga_ref[...] = _sigmoid(proj((g0, g0 + d_model))).astype(BF16)
    gb_ref[...] = _sigmoid(proj((g0 + d_model, g0 + 2 * d_model))).astype(BF16)


def _rope_tables(S):
    rows = S // GRID_W
    row = jnp.repeat(jnp.arange(rows, dtype=F32), GRID_W)
    col = jnp.tile(jnp.arange(GRID_W, dtype=F32), rows)

    def cs(rot_dim):
        n_ax = rot_dim // 4
        freqs = 1.0 / (ROPE_THETA ** (jnp.arange(n_ax, dtype=F32) / n_ax))
        ang = jnp.concatenate([row[:, None] * freqs, col[:, None] * freqs], axis=-1)
        return jnp.cos(ang), jnp.sin(ang)

    z = lambda w: jnp.zeros((S, w), F32)
    o = lambda w: jnp.ones((S, w), F32)
    ca, sa = cs(MLA_ROPE)
    pad = LANE - MLA_NOPE - MLA_ROPE
    taba = jnp.stack([
        jnp.concatenate([o(MLA_NOPE), ca, ca, z(pad)], axis=1),
        jnp.concatenate([z(MLA_NOPE), -sa, z(MLA_ROPE // 2), z(pad)], axis=1),
        jnp.concatenate([z(MLA_NOPE), z(MLA_ROPE // 2), sa, z(pad)], axis=1)])
    cb, sb = cs(GQA_HD)
    hz = z(GQA_HD // 2)
    tabb = jnp.stack([
        jnp.concatenate([cb, cb, cb, cb], axis=1),
        jnp.concatenate([-sb, hz, -sb, hz], axis=1),
        jnp.concatenate([hz, sb, hz, sb], axis=1)])
    return taba, tabb


def _block_diag(groups):
    idx = jnp.arange(MXU)
    m = jnp.zeros((MXU, MXU), F32)
    for base in range(0, MXU, LANE):
        for start, size in groups:
            inside = (idx >= base + start) & (idx < base + start + size)
            m = m + jnp.where(inside[:, None] & inside[None, :], 1.0 / size, 0.0)
    return m.astype(BF16)


def _prep_mixer_weights(p):
    d = p["w_in"].shape[0]
    w = p["w_in"]
    o = 0
    cq = w[:, o:o + Q_LORA]; o += Q_LORA
    ckv = w[:, o:o + KV_LORA]; o += KV_LORA
    kr = w[:, o:o + MLA_ROPE]; o += MLA_ROPE
    qb = w[:, o:o + GQA_HEADS * GQA_HD]; o += GQA_HEADS * GQA_HD
    kb = w[:, o:o + GQA_KV_HEADS * GQA_HD]; o += GQA_KV_HEADS * GQA_HD
    vb = w[:, o:o + GQA_KV_HEADS * GQA_HD]; o += GQA_KV_HEADS * GQA_HD
    ga = w[:, o:o + d]; o += d
    gb = w[:, o:o + d]
    zc = lambda n: jnp.zeros((d, n), w.dtype)
    dup = lambda m: jnp.concatenate(
        [m[:, g * GQA_HD:(g + 1) * GQA_HD] for g in range(GQA_KV_HEADS) for _ in range(2)], axis=1)
    win = jnp.concatenate(
        [cq, ckv, zc(MLA_NOPE), kr, zc(LANE - MLA_NOPE - MLA_ROPE), qb, dup(kb), dup(vb), ga, gb], axis=1)

    dq = MLA_NOPE + MLA_ROPE
    wuq = p["mla_w_uq"].reshape(Q_LORA, MLA_HEADS, dq)
    wuq = jnp.pad(wuq, ((0, 0), (0, 0), (0, LANE - dq))).reshape(Q_LORA, _W_SLAB)
    wukv = p["mla_w_ukv"].reshape(KV_LORA, MLA_HEADS, MLA_NOPE + MLA_V)
    wk = jnp.pad(wukv[:, :, :MLA_NOPE], ((0, 0), (0, 0), (0, LANE - MLA_NOPE))).reshape(KV_LORA, _W_SLAB)
    wv = wukv[:, :, MLA_NOPE:].reshape(KV_LORA, MLA_HEADS * MLA_V)
    wukv = jnp.concatenate([wk, wv], axis=1)

    row = lambda v: v.reshape(1, -1).astype(F32)
    zl = lambda n: jnp.zeros((n,), F32)
    gqn, gkn = p["mla_gqn"], p["mla_gkn"]
    pad = LANE - dq
    gqa = jnp.tile(jnp.concatenate([gqn, zl(pad)]), MLA_HEADS)
    gka = jnp.tile(jnp.concatenate([gkn[:MLA_NOPE], zl(LANE - MLA_NOPE)]), MLA_HEADS)
    gkr = jnp.concatenate([zl(MLA_NOPE), gkn[MLA_NOPE:], zl(pad)])
    gqg = jnp.tile(p["gqa_gqn"], GQA_HEADS)
    gkg = jnp.tile(p["gqa_gkn"], 2 * GQA_KV_HEADS)
    return dict(
        gmix=row(p["g_mix"]), win=win.astype(BF16), gq=row(p["mla_gq"]), wuq=wuq.astype(BF16),
        gkv=row(p["mla_gkv"]), wukv=wukv.astype(BF16), gqa=row(gqa), gka=row(gka), gkr=row(gkr),
        gqg=row(gqg), gkg=row(gkg),
        bda=_block_diag([(0, MLA_NOPE), (MLA_NOPE, MLA_ROPE)]),
        bdb=_block_diag([(0, GQA_HD), (GQA_HD, GQA_HD)]))


def _mixer_in(x2d, w, S, tm):
    n, d = x2d.shape
    tm = _pick(S, tm)
    nt_s = S // tm
    taba, tabb = _rope_tables(S)
    full = lambda a: pl.BlockSpec(a.shape, lambda i: (0,) * a.ndim)
    rows = lambda wd: pl.BlockSpec((tm, wd), lambda i: (i, 0))
    tab = pl.BlockSpec((3, tm, LANE), lambda i: (0, i % nt_s, 0))
    consts = [w[k] for k in ("gmix", "win", "gq", "wuq", "gkv", "wukv", "gqa", "gka", "gkr", "gqg", "gkg",
                             "bda", "bdb")]
    widths = (_W_SLAB, _W_SLAB, MLA_HEADS * MLA_V, _W_SLAB, GQA_KV_HEADS * LANE, GQA_KV_HEADS * LANE, d, d)
    return pl.pallas_call(
        functools.partial(_mixer_in_kernel, d_model=d),
        grid=(n // tm,),
        in_specs=[rows(d)] + [full(c) for c in consts] + [tab, tab],
        out_specs=[rows(wd) for wd in widths],
        out_shape=[jax.ShapeDtypeStruct((n, wd), BF16) for wd in widths],
        compiler_params=pltpu.CompilerParams(dimension_semantics=("parallel",), vmem_limit_bytes=VMEM_LIMIT),
        name="mixer_in",
    )(x2d, *consts, taba, tabb)


def _attn_kernel(q_ref, k_ref, v_ref, o_ref, m_scr, l_scr, acc_scr, *, k_slab, v_slab, half):
    j = pl.program_id(2)
    nheads = len(k_slab)

    @pl.when(j == 0)
    def _():
        m_scr[...] = jnp.full(m_scr.shape, -jnp.inf, F32)
        l_scr[...] = jnp.zeros(l_scr.shape, F32)
        acc_scr[...] = jnp.zeros(acc_scr.shape, F32)

    for h in range(nheads):
        q = q_ref[:, h * LANE:(h + 1) * LANE]
        k = k_ref[:, k_slab[h] * LANE:(k_slab[h] + 1) * LANE]
        v = v_ref[:, v_slab[h] * LANE:(v_slab[h] + 1) * LANE]
        s = lax.dot_general(q, k, (((1,), (1,)), ((), ())), preferred_element_type=F32)
        m_prev = m_scr[h]
        m_new = jnp.maximum(m_prev, jnp.max(s, axis=1, keepdims=True))
        alpha = jnp.exp(m_prev - m_new)
        p = jnp.exp(s - m_new)
        l_scr[h] = alpha * l_scr[h] + jnp.sum(p, axis=1, keepdims=True)
        acc_scr[h] = alpha * acc_scr[h] + jnp.dot(p.astype(BF16), v, preferred_element_type=F32)
        m_scr[h] = m_new

    @pl.when(j == pl.num_programs(2) - 1)
    def _():
        low = lax.broadcasted_iota(jnp.int32, (1, LANE), 1) < half
        for p in range(nheads // 2):
            a0 = acc_scr[2 * p] / l_scr[2 * p]
            a1 = acc_scr[2 * p + 1] / l_scr[2 * p + 1]
            o_ref[:, p * LANE:(p + 1) * LANE] = jnp.where(low, a0, a1).astype(o_ref.dtype)


def _attention(q, k, v, B, S, k_slab, v_slab, tq, tk):
    n = q.shape[0]
    nheads = len(k_slab)
    tq, tk = _pick(S, tq), _pick(S, tk)
    nq, nk = S // tq, S // tk
    return pl.pallas_call(
        functools.partial(_attn_kernel, k_slab=tuple(k_slab), v_slab=tuple(v_slab), half=LANE // 2),
        grid=(B, nq, nk),
        in_specs=[pl.BlockSpec((tq, q.shape[1]), lambda b, i, j: (b * nq + i, 0)),
                  pl.BlockSpec((tk, k.shape[1]), lambda b, i, j: (b * nk + j, 0)),
                  pl.BlockSpec((tk, v.shape[1]), lambda b, i, j: (b * nk + j, 0))],
        out_specs=pl.BlockSpec((tq, nheads * LANE // 2), lambda b, i, j: (b * nq + i, 0)),
        out_shape=jax.ShapeDtypeStruct((n, nheads * LANE // 2), BF16),
        scratch_shapes=[pltpu.VMEM((nheads, tq, 1), F32), pltpu.VMEM((nheads, tq, 1), F32),
                        pltpu.VMEM((nheads, tq, LANE), F32)],
        compiler_params=pltpu.CompilerParams(
            dimension_semantics=("parallel", "parallel", "arbitrary"), vmem_limit_bytes=VMEM_LIMIT),
        name="attention",
    )(q, k, v)


def _mem_kv_kernel(mem_ref, gmem_ref, wkv_ref, gkn_ref, kx_ref, vx_ref):
    m = _rms_rows(mem_ref[...], gmem_ref[...]).astype(BF16)
    kv = jnp.dot(m, wkv_ref[...], preferred_element_type=F32)
    w = XA_HEADS * XA_HD
    g = gkn_ref[...]
    parts = [_rms_rows(kv[:, h * XA_HD:(h + 1) * XA_HD], g) for h in range(XA_HEADS)]
    kx_ref[...] = jnp.concatenate(parts, axis=1).astype(BF16)
    vx_ref[...] = kv[:, w:].astype(BF16)


def _mem_kv(mem2d, gmem, wkv, gkn, tm):
    n, d = mem2d.shape
    tm = _pick(n, tm)
    w = XA_HEADS * XA_HD
    full = lambda a: pl.BlockSpec(a.shape, lambda i: (0,) * a.ndim)
    return pl.pallas_call(
        _mem_kv_kernel,
        grid=(n // tm,),
        in_specs=[pl.BlockSpec((tm, d), lambda i: (i, 0)), full(gmem), full(wkv), full(gkn)],
        out_specs=[pl.BlockSpec((tm, w), lambda i: (i, 0))] * 2,
        out_shape=[jax.ShapeDtypeStruct((n, w), BF16)] * 2,
        compiler_params=pltpu.CompilerParams(dimension_semantics=("parallel",), vmem_limit_bytes=VMEM_LIMIT),
        name="mem_kv",
    )(mem2d, gmem, wkv, gkn)


def _post_kernel(x_ref, oa_ref, ob_ref, ga_ref, gb_ref, wbra_ref, wbrb_ref, wmix_ref, gcross_ref, wq_ref,
                 gqn_ref, kx_ref, vx_ref, wo_ref, gmoe_ref, wrhi_ref, wrlo_ref, x2_ref, hm_ref, aff_ref):
    ba = jnp.dot(oa_ref[...], wbra_ref[...], preferred_element_type=F32)
    bb = jnp.dot(ob_ref[...], wbrb_ref[...], preferred_element_type=F32)
    mixed = ga_ref[...].astype(F32) * ba + gb_ref[...].astype(F32) * bb
    x1 = x_ref[...] + jnp.dot(mixed.astype(BF16), wmix_ref[...], preferred_element_type=F32)

    hc = _rms_rows(x1, gcross_ref[...]).astype(BF16)
    q = jnp.dot(hc, wq_ref[...], preferred_element_type=F32)
    gqn = gqn_ref[...]
    outs = []
    for h in range(XA_HEADS):
        sl = slice(h * XA_HD, (h + 1) * XA_HD)
        qh = (_rms_rows(q[:, sl], gqn) * (XA_HD ** -0.5)).astype(BF16)
        s = lax.dot_general(qh, kx_ref[:, sl], (((1,), (1,)), ((), ())), preferred_element_type=F32)
        p = jnp.exp(s - jnp.max(s, axis=1, keepdims=True))
        l = jnp.sum(p, axis=1, keepdims=True)
        outs.append(jnp.dot(p.astype(BF16), vx_ref[:, sl], preferred_element_type=F32) / l)
    ox = jnp.concatenate(outs, axis=1).astype(BF16)
    x2 = x1 + jnp.dot(ox, wo_ref[...], preferred_element_type=F32)
    x2_ref[...] = x2

    hm = _rms_rows(x2, gmoe_ref[...])
    hi = hm.astype(BF16)
    lo = (hm - hi.astype(F32)).astype(BF16)
    hm_ref[...] = hi
    dn = (((1,), (1,)), ((), ()))
    wrhi = wrhi_ref[...]
    logits = (lax.dot_general(wrhi, hi, dn, preferred_element_type=F32)
              + lax.dot_general(wrhi, lo, dn, preferred_element_type=F32)
              + lax.dot_general(wrlo_ref[...], hi, dn, preferred_element_type=F32))
    e = jnp.exp(logits - jnp.max(logits, axis=0, keepdims=True))
    aff_ref[...] = e / jnp.sum(e, axis=0, keepdims=True)


def _post(x2d, oa, ob, ga, gb, kx, vx, w, S, n_mem, tm):
    n, d = x2d.shape
    tm = _pick(S, tm)
    nt_s = S // tm
    ne = w["wrhi"].shape[0]
    full = lambda a: pl.BlockSpec(a.shape, lambda i: (0,) * a.ndim)
    rows = lambda wd: pl.BlockSpec((tm, wd), lambda i: (i, 0))
    memblk = pl.BlockSpec((n_mem, kx.shape[1]), lambda i: (i // nt_s, 0))
    c = [w[k] for k in ("wbra", "wbrb", "wmix", "gcross", "wq", "gqn")]
    c2 = [w[k] for k in ("wo", "gmoe", "wrhi", "wrlo")]
    return pl.pallas_call(
        _post_kernel,
        grid=(n // tm,),
        in_specs=[rows(d), rows(oa.shape[1]), rows(ob.shape[1]), rows(d), rows(d)] + [full(a) for a in c]
                 + [memblk, memblk] + [full(a) for a in c2],
        out_specs=[rows(d), rows(d), pl.BlockSpec((ne, tm), lambda i: (0, i))],
        out_shape=[jax.ShapeDtypeStruct((n, d), F32), jax.ShapeDtypeStruct((n, d), BF16),
                   jax.ShapeDtypeStruct((ne, n), F32)],
        compiler_params=pltpu.CompilerParams(dimension_semantics=("parallel",), vmem_limit_bytes=VMEM_LIMIT),
        name="post_mix",
    )(x2d, oa, ob, ga, gb, *c, kx, vx, *c2)


_SEL_CHUNK = 2048
_TIE_CHUNK = 512


def _topc_kernel(aff_ref, tri_ref, wsel_ref, *, cap):
    ne, n = aff_ref.shape
    chunk = min(_SEL_CHUNK, n)
    nchunks = n // chunk

    def bits_at(c, width):
        return pltpu.bitcast(aff_ref[:, pl.ds(pl.multiple_of(c * width, width), width)], jnp.int32)

    def count(pred_fn):
        def body(c, acc):
            return acc + pred_fn(bits_at(c, chunk)).astype(jnp.int32)
        acc = lax.fori_loop(0, nchunks, body, jnp.zeros((ne, chunk), jnp.int32))
        return jnp.sum(acc, axis=1, keepdims=True)

    def bit_step(i, t):
        cand = t | jnp.left_shift(jnp.int32(1), 30 - i)
        return jnp.where(count(lambda b: b >= cand) >= cap, cand, t)

    thr = lax.fori_loop(0, 31, bit_step, jnp.zeros((ne, 1), jnp.int32))
    need = (cap - count(lambda b: b > thr)).astype(F32)

    tchunk = min(_TIE_CHUNK, n)
    tri = tri_ref[...]

    def tie_step(c, run):
        sl = pl.ds(pl.multiple_of(c * tchunk, tchunk), tchunk)
        a = aff_ref[:, sl]
        b = pltpu.bitcast(a, jnp.int32)
        eq = b == thr
        eqf = jnp.where(eq, 1.0, 0.0)
        before = run + jnp.dot(eqf.astype(BF16), tri, preferred_element_type=F32)
        sel = (b > thr) | (eq & (before < need))
        wsel_ref[:, sl] = jnp.where(sel, a, 0.0)
        return run + jnp.sum(eqf, axis=1, keepdims=True)

    lax.fori_loop(0, n // tchunk, tie_step, jnp.zeros((ne, 1), F32))


def _topc(aff_t, cap):
    ne, n = aff_t.shape
    tchunk = min(_TIE_CHUNK, n)
    idx = jnp.arange(tchunk)
    tri = (idx[:, None] < idx[None, :]).astype(BF16)
    return pl.pallas_call(
        functools.partial(_topc_kernel, cap=cap),
        out_shape=jax.ShapeDtypeStruct((ne, n), F32),
        compiler_params=pltpu.CompilerParams(vmem_limit_bytes=VMEM_LIMIT),
        name="expert_choice_select",
    )(aff_t, tri)


def _moe_dense_kernel(hm_ref, w_ref, x2_ref, wg_ref, wu_ref, wd_ref, o_ref):
    e = pl.program_id(1)

    @pl.when(e == 0)
    def _():
        o_ref[...] = x2_ref[...]

    h = hm_ref[...]
    g = jnp.dot(h, wg_ref[0], preferred_element_type=F32)
    u = jnp.dot(h, wu_ref[0], preferred_element_type=F32)
    hid = (g * _sigmoid(g) * u).astype(BF16)
    out = jnp.dot(hid, wd_ref[0], preferred_element_type=F32)
    lane = lax.broadcasted_iota(jnp.int32, w_ref.shape, 1)
    wcol = jnp.sum(jnp.where(lane == e, w_ref[...], 0.0), axis=1, keepdims=True)
    o_ref[...] += out * wcol


def _moe_dense(hm, wsel, x2, wg, wu, wd, tm):
    n, d = hm.shape
    ne, _, ff = wg.shape
    tm = _pick(n, tm)
    return pl.pallas_call(
        _moe_dense_kernel,
        grid=(n // tm, ne),
        in_specs=[pl.BlockSpec((tm, d), lambda i, e: (i, 0)),
                  pl.BlockSpec((tm, ne), lambda i, e: (i, 0)),
                  pl.BlockSpec((tm, d), lambda i, e: (i, 0)),
                  pl.BlockSpec((1, d, ff), lambda i, e: (e, 0, 0)),
                  pl.BlockSpec((1, d, ff), lambda i, e: (e, 0, 0)),
                  pl.BlockSpec((1, ff, d), lambda i, e: (e, 0, 0))],
        out_specs=pl.BlockSpec((tm, d), lambda i, e: (i, 0)),
        out_shape=jax.ShapeDtypeStruct((n, d), F32),
        compiler_params=pltpu.CompilerParams(
            dimension_semantics=("parallel", "arbitrary"), vmem_limit_bytes=VMEM_LIMIT),
        name="moe_dense",
    )(hm, wsel, x2, wg, wu, wd)


def _prep_post_weights(p):
    row = lambda v: v.reshape(1, -1).astype(F32)
    wr = p["w_router"].T.astype(F32)
    wrhi = wr.astype(BF16)
    wrlo = (wr - wrhi.astype(F32)).astype(BF16)
    return dict(
        wbra=p["mla_w_br"].astype(BF16), wbrb=p["gqa_w_br"].astype(BF16), wmix=p["w_mix_out"].astype(BF16),
        gcross=row(p["g_cross"]), wq=p["xa_w_q"].astype(BF16), gqn=row(p["xa_gqn"]),
        wo=p["xa_w_o"].astype(BF16), gmoe=row(p["g_moe"]), wrhi=wrhi, wrlo=wrlo,
        gmem=row(p["g_mem"]), wkv=p["xa_w_kv"].astype(BF16), gkn=row(p["xa_gkn"]),
        wg=p["moe_w_gate"].astype(BF16), wu=p["moe_w_up"].astype(BF16), wd=p["moe_w_down"].astype(BF16))


def _encoder_layer(x, mem, wm, wp):
    B, S, D = x.shape
    n = B * S
    n_mem = mem.shape[1]
    x2d = x.reshape(n, D)
    qa, ka, va, qg, kg, vg, ga, gb = _mixer_in(x2d, wm, S, tm=256)
    oa = _attention(qa, ka, va, B, S, k_slab=range(MLA_HEADS), v_slab=[h // 2 for h in range(MLA_HEADS)],
                    tq=256, tk=256)
    rep = GQA_HEADS // GQA_KV_HEADS
    ob = _attention(qg, kg, vg, B, S, k_slab=[h // rep for h in range(GQA_HEADS)],
                    v_slab=[h // rep for h in range(GQA_HEADS)], tq=256, tk=256)
    kx, vx = _mem_kv(mem.reshape(B * n_mem, D), wp["gmem"], wp["wkv"], wp["gkn"], tm=256)
    x2, hm, aff_t = _post(x2d, oa, ob, ga, gb, kx, vx, wp, S, n_mem, tm=256)
    cap = EC_FACTOR * n // N_EXPERTS
    wsel_t = _topc(aff_t, cap)
    out = _moe_dense(hm, wsel_t.T, x2, wp["wg"], wp["wu"], wp["wd"], tm=1024)
    return out.reshape(B, S, D)


def kernel(x_prompt, x_sample, mem_prompt, mem_sample, g_mix, w_in, mla_gq, mla_w_uq, mla_gkv, mla_w_ukv, mla_gqn, mla_gkn, mla_w_br, gqa_gqn, gqa_gkn, gqa_w_br, w_mix_out, g_cross, g_mem, xa_w_q, xa_w_kv, xa_gqn, xa_gkn, xa_w_o, g_moe, w_router, moe_w_gate, moe_w_up, moe_w_down):
    names = ("g_mix", "w_in", "mla_gq", "mla_w_uq", "mla_gkv", "mla_w_ukv", "mla_gqn", "mla_gkn", "mla_w_br",
             "gqa_gqn", "gqa_gkn", "gqa_w_br", "w_mix_out", "g_cross", "g_mem", "xa_w_q", "xa_w_kv", "xa_gqn",
             "xa_gkn", "xa_w_o", "g_moe", "w_router", "moe_w_gate", "moe_w_up", "moe_w_down")
    vals = (g_mix, w_in, mla_gq, mla_w_uq, mla_gkv, mla_w_ukv, mla_gqn, mla_gkn, mla_w_br,
            gqa_gqn, gqa_gkn, gqa_w_br, w_mix_out, g_cross, g_mem, xa_w_q, xa_w_kv, xa_gqn,
            xa_gkn, xa_w_o, g_moe, w_router, moe_w_gate, moe_w_up, moe_w_down)
    y_prompt, y_sample = x_prompt, x_sample
    depth = w_in.shape[0]
    for l in range(depth):
        p = {k: v[l] for k, v in zip(names, vals)}
        wm = _prep_mixer_weights(p)
        wp = _prep_post_weights(p)
        y_prompt = _encoder_layer(y_prompt, mem_prompt, wm, wp)
        y_sample = _encoder_layer(y_sample, mem_sample, wm, wp)
    return (y_prompt, y_sample)
```

```python
import functools
import math

import jax
import jax.numpy as jnp
from jax import lax
from jax.experimental import pallas as pl
from jax.experimental.pallas import tpu as pltpu

EPS = 1e-6
GRID_W = 64
ROPE_THETA = 10000.0
MLA_HEADS = 8
Q_LORA = 384
KV_LORA = 256
MLA_NOPE = 64
MLA_ROPE = 32
MLA_V = 64
GQA_HEADS = 8
GQA_KV_HEADS = 2
GQA_HD = 64
XA_HEADS = 4
XA_HD = 128
N_EXPERTS = 16
EC_FACTOR = 2

LANE = 128
MXU = 256
VMEM_LIMIT = 56 * 1024 * 1024
LOG2E = math.log2(math.e)

F32 = jnp.float32
BF16 = jnp.bfloat16


def _pick(n, pref):
    t = min(n, pref)
    while n % t:
        t -= LANE
    return t


def _rms_rows(x, g):
    ms = jnp.mean(x * x, axis=-1, keepdims=True)
    return x * lax.rsqrt(ms + EPS) * g


def _group_mean_sq(x, bd):
    x2 = x * x
    hi = x2.astype(BF16)
    lo = (x2 - hi.astype(F32)).astype(BF16)
    w = x.shape[1]
    outs = []
    for c in range(0, w, MXU):
        cw = min(MXU, w - c)
        b = bd[:cw, :cw]
        outs.append(jnp.dot(hi[:, c:c + cw], b, preferred_element_type=F32)
                    + jnp.dot(lo[:, c:c + cw], b, preferred_element_type=F32))
    return outs[0] if len(outs) == 1 else jnp.concatenate(outs, axis=1)


def _group_rms(x, g, bd):
    return x * lax.rsqrt(_group_mean_sq(x, bd) + EPS) * g


def _rope_slabs(x, tab, shift):
    c, s1, s2 = tab[0], tab[1], tab[2]
    outs = []
    for a in range(0, x.shape[1], LANE):
        xs = x[:, a:a + LANE]
        outs.append(xs * c + pltpu.roll(xs, LANE - shift, 1) * s1 + pltpu.roll(xs, shift, 1) * s2)
    return outs[0] if len(outs) == 1 else jnp.concatenate(outs, axis=1)


def _sigmoid(x):
    return 1.0 / (1.0 + jnp.exp(-x))


_SEG_CQ = (0, Q_LORA)
_SEG_CKV = (_SEG_CQ[1], _SEG_CQ[1] + KV_LORA)
_SEG_KR = (_SEG_CKV[1], _SEG_CKV[1] + LANE)
_SEG_QB = (_SEG_KR[1], _SEG_KR[1] + GQA_HEADS * GQA_HD)
_SEG_KB = (_SEG_QB[1], _SEG_QB[1] + GQA_KV_HEADS * LANE)
_W_SLAB = MLA_HEADS * LANE

_NT = (((1,), (1,)), ((), ()))


def _mixer_in_kernel(x_ref, gmix_ref, win_ref, gq_ref, wuq_ref, gkv_ref, wuk_ref, wuvt_ref, wvbt_ref,
                     gqa_ref, gka_ref, gkr_ref, gqg_ref, gkg_ref, bda_ref, bdb_ref, taba_ref, tabb_ref,
                     qa_ref, ka_ref, vat_ref, qg_ref, kg_ref, vgt_ref, ga_ref, gb_ref, *, d_model):
    h = _rms_rows(x_ref[...], gmix_ref[...]).astype(BF16)

    def proj(seg):
        return jnp.dot(h, win_ref[:, seg[0]:seg[1]], preferred_element_type=F32)

    bda = bda_ref[...]
    bdb = bdb_ref[...]
    taba = taba_ref[...]
    tabb = tabb_ref[...]

    cq = _rms_rows(proj(_SEG_CQ), gq_ref[...]).astype(BF16)
    qa = jnp.dot(cq, wuq_ref[...], preferred_element_type=F32)
    qa = _rope_slabs(_group_rms(qa, gqa_ref[...], bda), taba, MLA_ROPE // 2)
    qa_ref[...] = (qa * ((MLA_NOPE + MLA_ROPE) ** -0.5 * LOG2E)).astype(BF16)

    ckv = _rms_rows(proj(_SEG_CKV), gkv_ref[...]).astype(BF16)
    kn = _group_rms(jnp.dot(ckv, wuk_ref[...], preferred_element_type=F32), gka_ref[...], bdb)
    kr = _rope_slabs(_group_rms(proj(_SEG_KR), gkr_ref[...], bda), taba, MLA_ROPE // 2)
    ka_ref[...] = (kn + jnp.concatenate([kr] * MLA_HEADS, axis=1)).astype(BF16)
    vat_ref[...] = lax.dot_general(wuvt_ref[...], ckv, _NT, preferred_element_type=F32).astype(BF16)

    qg = _rope_slabs(_group_rms(proj(_SEG_QB), gqg_ref[...], bdb), tabb, GQA_HD // 2) * (GQA_HD ** -0.5 * LOG2E)
    lane = lax.broadcasted_iota(jnp.int32, (1, LANE), 1)
    low = lane < GQA_HD
    parts = []
    for p in range(GQA_HEADS // 2):
        s = qg[:, p * LANE:(p + 1) * LANE]
        parts.append(jnp.where(low, s, 0.0))
        parts.append(jnp.where(low, 0.0, s))
    qg_ref[...] = jnp.concatenate(parts, axis=1).astype(BF16)
    kg = _rope_slabs(_group_rms(proj(_SEG_KB), gkg_ref[...], bdb), tabb, GQA_HD // 2)
    kg_ref[...] = kg.astype(BF16)
    vgt_ref[...] = lax.dot_general(wvbt_ref[...], h, _NT, preferred_element_type=F32).astype(BF16)

    g0 = _SEG_KB[1]
    ga_ref[...] = _sigmoid(proj((g0, g0 + d_model))).astype(BF16)
    gb_ref[...] = _sigmoid(proj((g0 + d_model, g0 + 2 * d_model))).astype(BF16)


def _rope_tables(S):
    rows = S // GRID_W
    row = jnp.repeat(jnp.arange(rows, dtype=F32), GRID_W)
    col = jnp.tile(jnp.arange(GRID_W, dtype=F32), rows)

    def cs(rot_dim):
        n_ax = rot_dim // 4
        freqs = 1.0 / (ROPE_THETA ** (jnp.arange(n_ax, dtype=F32) / n_ax))
        ang = jnp.concatenate([row[:, None] * freqs, col[:, None] * freqs], axis=-1)
        return jnp.cos(ang), jnp.sin(ang)

    z = lambda w: jnp.zeros((S, w), F32)
    o = lambda w: jnp.ones((S, w), F32)
    ca, sa = cs(MLA_ROPE)
    pad = LANE - MLA_NOPE - MLA_ROPE
    taba = jnp.stack([
        jnp.concatenate([o(MLA_NOPE), ca, ca, z(pad)], axis=1),
        jnp.concatenate([z(MLA_NOPE), -sa, z(MLA_ROPE // 2), z(pad)], axis=1),
        jnp.concatenate([z(MLA_NOPE), z(MLA_ROPE // 2), sa, z(pad)], axis=1)])
    cb, sb = cs(GQA_HD)
    hz = z(GQA_HD // 2)
    tabb = jnp.stack([
        jnp.concatenate([cb, cb, cb, cb], axis=1),
        jnp.concatenate([-sb, hz, -sb, hz], axis=1),
        jnp.concatenate([hz, sb, hz, sb], axis=1)])
    return taba, tabb


def _block_diag(groups):
    idx = jnp.arange(MXU)
    m = jnp.zeros((MXU, MXU), F32)
    for base in range(0, MXU, LANE):
        for start, size in groups:
            inside = (idx >= base + start) & (idx < base + start + size)
            m = m + jnp.where(inside[:, None] & inside[None, :], 1.0 / size, 0.0)
    return m.astype(BF16)


def _prep_mixer_weights(p):
    d = p["w_in"].shape[0]
    w = p["w_in"]
    o = 0
    cq = w[:, o:o + Q_LORA]; o += Q_LORA
    ckv = w[:, o:o + KV_LORA]; o += KV_LORA
    kr = w[:, o:o + MLA_ROPE]; o += MLA_ROPE
    qb = w[:, o:o + GQA_HEADS * GQA_HD]; o += GQA_HEADS * GQA_HD
    kb = w[:, o:o + GQA_KV_HEADS * GQA_HD]; o += GQA_KV_HEADS * GQA_HD
    vb = w[:, o:o + GQA_KV_HEADS * GQA_HD]; o += GQA_KV_HEADS * GQA_HD
    ga = w[:, o:o + d]; o += d
    gb = w[:, o:o + d]
    zc = lambda n: jnp.zeros((d, n), w.dtype)
    dup = lambda m: jnp.concatenate(
        [m[:, g * GQA_HD:(g + 1) * GQA_HD] for g in range(GQA_KV_HEADS) for _ in range(2)], axis=1)
    win = jnp.concatenate(
        [cq, ckv, zc(MLA_NOPE), kr, zc(LANE - MLA_NOPE - MLA_ROPE), qb, dup(kb), ga, gb], axis=1)

    dq = MLA_NOPE + MLA_ROPE
    wuq = p["mla_w_uq"].reshape(Q_LORA, MLA_HEADS, dq)
    wuq = jnp.pad(wuq, ((0, 0), (0, 0), (0, LANE - dq))).reshape(Q_LORA, _W_SLAB)
    wukv = p["mla_w_ukv"].reshape(KV_LORA, MLA_HEADS, MLA_NOPE + MLA_V)
    wk = jnp.pad(wukv[:, :, :MLA_NOPE], ((0, 0), (0, 0), (0, LANE - MLA_NOPE))).reshape(KV_LORA, _W_SLAB)
    wvt = wukv[:, :, MLA_NOPE:].reshape(KV_LORA, MLA_HEADS * MLA_V).T

    row = lambda v: v.reshape(1, -1).astype(F32)
    zl = lambda n: jnp.zeros((n,), F32)
    gqn, gkn = p["mla_gqn"], p["mla_gkn"]
    pad = LANE - dq
    gqa = jnp.tile(jnp.concatenate([gqn, zl(pad)]), MLA_HEADS)
    gka = jnp.tile(jnp.concatenate([gkn[:MLA_NOPE], zl(LANE - MLA_NOPE)]), MLA_HEADS)
    gkr = jnp.concatenate([zl(MLA_NOPE), gkn[MLA_NOPE:], zl(pad)])
    gqg = jnp.tile(p["gqa_gqn"], GQA_HEADS)
    gkg = jnp.tile(p["gqa_gkn"], 2 * GQA_KV_HEADS)
    return dict(
        gmix=row(p["g_mix"]), win=win.astype(BF16), gq=row(p["mla_gq"]), wuq=wuq.astype(BF16),
        gkv=row(p["mla_gkv"]), wuk=wk.astype(BF16), wuvt=wvt.astype(BF16), wvbt=vb.T.astype(BF16),
        gqa=row(gqa), gka=row(gka), gkr=row(gkr),
        gqg=row(gqg), gkg=row(gkg),
        bda=_block_diag([(0, MLA_NOPE), (MLA_NOPE, MLA_ROPE)]),
        bdb=_block_diag([(0, GQA_HD), (GQA_HD, GQA_HD)]))


def _mixer_in(x2d, w, S, tm):
    n, d = x2d.shape
    tm = _pick(S, tm)
    nt_s = S // tm
    taba, tabb = _rope_tables(S)
    full = lambda a: pl.BlockSpec(a.shape, lambda i: (0,) * a.ndim)
    rows = lambda wd: pl.BlockSpec((tm, wd), lambda i: (i, 0))
    tab = pl.BlockSpec((3, tm, LANE), lambda i: (0, i % nt_s, 0))
    consts = [w[k] for k in ("gmix", "win", "gq", "wuq", "gkv", "wuk", "wuvt", "wvbt", "gqa", "gka", "gkr",
                             "gqg", "gkg", "bda", "bdb")]
    cols = lambda ht: pl.BlockSpec((ht, tm), lambda i: (0, i))
    outs = ((_W_SLAB, False), (_W_SLAB, False), (MLA_HEADS * MLA_V, True), (_W_SLAB, False),
            (GQA_KV_HEADS * LANE, False), (GQA_KV_HEADS * GQA_HD, True), (d, False), (d, False))
    return pl.pallas_call(
        functools.partial(_mixer_in_kernel, d_model=d),
        grid=(n // tm,),
        in_specs=[rows(d)] + [full(c) for c in consts] + [tab, tab],
        out_specs=[cols(wd) if t else rows(wd) for wd, t in outs],
        out_shape=[jax.ShapeDtypeStruct((wd, n) if t else (n, wd), BF16) for wd, t in outs],
        compiler_params=pltpu.CompilerParams(dimension_semantics=("parallel",), vmem_limit_bytes=VMEM_LIMIT),
        name="mixer_in",
    )(x2d, *consts, taba, tabb)


def _attn_kernel(q_ref, k_ref, vt_ref, o_ref, m_scr, l_scr, acc_scr, *, k_slab, v_row, tk, hd):
    nheads = len(k_slab)
    m_scr[...] = jnp.full(m_scr.shape, -jnp.inf, F32)
    l_scr[...] = jnp.zeros(l_scr.shape, F32)
    acc_scr[...] = jnp.zeros(acc_scr.shape, F32)

    def step(j, carry):
        off = pl.multiple_of(j * tk, tk)

        def scores(h):
            q = q_ref[:, h * LANE:(h + 1) * LANE]
            k = k_ref[pl.ds(off, tk), k_slab[h] * LANE:(k_slab[h] + 1) * LANE]
            return lax.dot_general(k, q, _NT, preferred_element_type=F32)

        def softmax_update(h, s):
            m_prev = m_scr[h]
            m_new = jnp.maximum(m_prev, jnp.max(s, axis=0, keepdims=True))
            alpha = jnp.exp2(m_prev - m_new)
            p = jnp.exp2(s - m_new)
            l_scr[h] = alpha * l_scr[h] + jnp.sum(p, axis=0, keepdims=True)
            m_scr[h] = m_new
            return alpha, p.astype(BF16)

        def weighted_values(h, p):
            vt = vt_ref[v_row[h] * hd:(v_row[h] + 1) * hd, pl.ds(off, tk)]
            return jnp.dot(vt, p, preferred_element_type=F32)

        s_next = scores(0)
        pending = None
        for h in range(nheads):
            s = s_next
            if h + 1 < nheads:
                s_next = scores(h + 1)
            alpha, p = softmax_update(h, s)
            pv = weighted_values(h, p)
            if pending is not None:
                ph, palpha, ppv = pending
                acc_scr[ph] = palpha * acc_scr[ph] + ppv
            pending = (h, alpha, pv)
        ph, palpha, ppv = pending
        acc_scr[ph] = palpha * acc_scr[ph] + ppv
        return carry

    lax.fori_loop(0, k_ref.shape[0] // tk, step, 0)
    out_t = jnp.concatenate([acc_scr[h] / l_scr[h] for h in range(nheads)], axis=0)
    o_ref[...] = out_t.T.astype(o_ref.dtype)


def _attention(q, k, vt, B, S, k_slab, v_row, tq, tk):
    n = q.shape[0]
    nheads = len(k_slab)
    hd = LANE // 2
    tq, tk = _pick(S, tq), _pick(S, tk)
    nq = S // tq
    return pl.pallas_call(
        functools.partial(_attn_kernel, k_slab=tuple(k_slab), v_row=tuple(v_row), tk=tk, hd=hd),
        grid=(B, nq),
        in_specs=[pl.BlockSpec((tq, q.shape[1]), lambda b, i: (b * nq + i, 0)),
                  pl.BlockSpec((S, k.shape[1]), lambda b, i: (b, 0)),
                  pl.BlockSpec((vt.shape[0], S), lambda b, i: (0, b))],
        out_specs=pl.BlockSpec((tq, nheads * hd), lambda b, i: (b * nq + i, 0)),
        out_shape=jax.ShapeDtypeStruct((n, nheads * hd), BF16),
        scratch_shapes=[pltpu.VMEM((nheads, 1, tq), F32), pltpu.VMEM((nheads, 1, tq), F32),
                        pltpu.VMEM((nheads, hd, tq), F32)],
        compiler_params=pltpu.CompilerParams(
            dimension_semantics=("parallel", "arbitrary"), vmem_limit_bytes=VMEM_LIMIT),
        name="attention",
    )(q, k, vt)


def _mem_kv_kernel(mem_ref, gmem_ref, wkv_ref, gkn_ref, kx_ref, vx_ref):
    m = _rms_rows(mem_ref[...], gmem_ref[...]).astype(BF16)
    kv = jnp.dot(m, wkv_ref[...], preferred_element_type=F32)
    w = XA_HEADS * XA_HD
    g = gkn_ref[...]
    parts = [_rms_rows(kv[:, h * XA_HD:(h + 1) * XA_HD], g) for h in range(XA_HEADS)]
    kx_ref[...] = jnp.concatenate(parts, axis=1).astype(BF16)
    vx_ref[...] = kv[:, w:].astype(BF16)


def _mem_kv(mem2d, gmem, wkv, gkn, tm):
    n, d = mem2d.shape
    tm = _pick(n, tm)
    w = XA_HEADS * XA_HD
    full = lambda a: pl.BlockSpec(a.shape, lambda i: (0,) * a.ndim)
    return pl.pallas_call(
        _mem_kv_kernel,
        grid=(n // tm,),
        in_specs=[pl.BlockSpec((tm, d), lambda i: (i, 0)), full(gmem), full(wkv), full(gkn)],
        out_specs=[pl.BlockSpec((tm, w), lambda i: (i, 0))] * 2,
        out_shape=[jax.ShapeDtypeStruct((n, w), BF16)] * 2,
        compiler_params=pltpu.CompilerParams(dimension_semantics=("parallel",), vmem_limit_bytes=VMEM_LIMIT),
        name="mem_kv",
    )(mem2d, gmem, wkv, gkn)


def _post_kernel(x_ref, oa_ref, ob_ref, ga_ref, gb_ref, wbra_ref, wbrb_ref, wmix_ref, gcross_ref, wq_ref,
                 gqn_ref, kx_ref, vx_ref, wo_ref, gmoe_ref, wrhi_ref, wrlo_ref, x2_ref, hm_ref, aff_ref):
    ba = jnp.dot(oa_ref[...], wbra_ref[...], preferred_element_type=F32)
    bb = jnp.dot(ob_ref[...], wbrb_ref[...], preferred_element_type=F32)
    mixed = ga_ref[...].astype(F32) * ba + gb_ref[...].astype(F32) * bb
    x1 = x_ref[...] + jnp.dot(mixed.astype(BF16), wmix_ref[...], preferred_element_type=F32)

    hc = _rms_rows(x1, gcross_ref[...]).astype(BF16)
    q = jnp.dot(hc, wq_ref[...], preferred_element_type=F32)
    gqn = gqn_ref[...]
    outs = []
    for h in range(XA_HEADS):
        sl = slice(h * XA_HD, (h + 1) * XA_HD)
        qh = (_rms_rows(q[:, sl], gqn) * (XA_HD ** -0.5)).astype(BF16)
        s = lax.dot_general(qh, kx_ref[:, sl], (((1,), (1,)), ((), ())), preferred_element_type=F32)
        p = jnp.exp(s - jnp.max(s, axis=1, keepdims=True))
        l = jnp.sum(p, axis=1, keepdims=True)
        outs.append(jnp.dot(p.astype(BF16), vx_ref[:, sl], preferred_element_type=F32) / l)
    ox = jnp.concatenate(outs, axis=1).astype(BF16)
    x2 = x1 + jnp.dot(ox, wo_ref[...], preferred_element_type=F32)
    x2_ref[...] = x2

    hm = _rms_rows(x2, gmoe_ref[...])
    hi = hm.astype(BF16)
    lo = (hm - hi.astype(F32)).astype(BF16)
    hm_ref[...] = hi
    dn = (((1,), (1,)), ((), ()))
    wrhi = wrhi_ref[...]
    logits = (lax.dot_general(wrhi, hi, dn, preferred_element_type=F32)
              + lax.dot_general(wrhi, lo, dn, preferred_element_type=F32)
              + lax.dot_general(wrlo_ref[...], hi, dn, preferred_element_type=F32))
    e = jnp.exp(logits - jnp.max(logits, axis=0, keepdims=True))
    aff_ref[...] = e / jnp.sum(e, axis=0, keepdims=True)


def _post(x2d, oa, ob, ga, gb, kx, vx, w, S, n_mem, tm):
    n, d = x2d.shape
    tm = _pick(S, tm)
    nt_s = S // tm
    ne = w["wrhi"].shape[0]
    full = lambda a: pl.BlockSpec(a.shape, lambda i: (0,) * a.ndim)
    rows = lambda wd: pl.BlockSpec((tm, wd), lambda i: (i, 0))
    memblk = pl.BlockSpec((n_mem, kx.shape[1]), lambda i: (i // nt_s, 0))
    c = [w[k] for k in ("wbra", "wbrb", "wmix", "gcross", "wq", "gqn")]
    c2 = [w[k] for k in ("wo", "gmoe", "wrhi", "wrlo")]
    return pl.pallas_call(
        _post_kernel,
        grid=(n // tm,),
        in_specs=[rows(d), rows(oa.shape[1]), rows(ob.shape[1]), rows(d), rows(d)] + [full(a) for a in c]
                 + [memblk, memblk] + [full(a) for a in c2],
        out_specs=[rows(d), rows(d), pl.BlockSpec((ne, tm), lambda i: (0, i))],
        out_shape=[jax.ShapeDtypeStruct((n, d), F32), jax.ShapeDtypeStruct((n, d), BF16),
                   jax.ShapeDtypeStruct((ne, n), F32)],
        compiler_params=pltpu.CompilerParams(dimension_semantics=("parallel",), vmem_limit_bytes=VMEM_LIMIT),
        name="post_mix",
    )(x2d, oa, ob, ga, gb, *c, kx, vx, *c2)


_SEL_CHUNK = 2048
_TIE_CHUNK = 512


def _topc_kernel(aff_ref, tri_ref, wsel_ref, *, cap):
    ne, n = aff_ref.shape
    chunk = min(_SEL_CHUNK, n)
    nchunks = n // chunk

    def bits_at(c, width):
        return pltpu.bitcast(aff_ref[:, pl.ds(pl.multiple_of(c * width, width), width)], jnp.int32)

    def count(pred_fn):
        def body(c, acc):
            return acc + pred_fn(bits_at(c, chunk)).astype(jnp.int32)
        acc = lax.fori_loop(0, nchunks, body, jnp.zeros((ne, chunk), jnp.int32))
        return jnp.sum(acc, axis=1, keepdims=True)

    def bit_step(i, t):
        cand = t | jnp.left_shift(jnp.int32(1), 30 - i)
        return jnp.where(count(lambda b: b >= cand) >= cap, cand, t)

    thr = lax.fori_loop(0, 31, bit_step, jnp.zeros((ne, 1), jnp.int32))
    need = (cap - count(lambda b: b > thr)).astype(F32)

    tchunk = min(_TIE_CHUNK, n)
    tri = tri_ref[...]

    def tie_step(c, run):
        sl = pl.ds(pl.multiple_of(c * tchunk, tchunk), tchunk)
        a = aff_ref[:, sl]
        b = pltpu.bitcast(a, jnp.int32)
        eq = b == thr
        eqf = jnp.where(eq, 1.0, 0.0)
        before = run + jnp.dot(eqf.astype(BF16), tri, preferred_element_type=F32)
        sel = (b > thr) | (eq & (before < need))
        wsel_ref[:, sl] = jnp.where(sel, a, 0.0)
        return run + jnp.sum(eqf, axis=1, keepdims=True)

    lax.fori_loop(0, n // tchunk, tie_step, jnp.zeros((ne, 1), F32))


def _topc(aff_t, cap):
    ne, n = aff_t.shape
    tchunk = min(_TIE_CHUNK, n)
    idx = jnp.arange(tchunk)
    tri = (idx[:, None] < idx[None, :]).astype(BF16)
    return pl.pallas_call(
        functools.partial(_topc_kernel, cap=cap),
        out_shape=jax.ShapeDtypeStruct((ne, n), F32),
        compiler_params=pltpu.CompilerParams(vmem_limit_bytes=VMEM_LIMIT),
        name="expert_choice_select",
    )(aff_t, tri)


def _moe_dense_kernel(hm_ref, w_ref, x2_ref, wg_ref, wu_ref, wd_ref, o_ref):
    e = pl.program_id(1)

    @pl.when(e == 0)
    def _():
        o_ref[...] = x2_ref[...]

    h = hm_ref[...]
    g = jnp.dot(h, wg_ref[0], preferred_element_type=F32)
    u = jnp.dot(h, wu_ref[0], preferred_element_type=F32)
    hid = (g * _sigmoid(g) * u).astype(BF16)
    out = jnp.dot(hid, wd_ref[0], preferred_element_type=F32)
    lane = lax.broadcasted_iota(jnp.int32, w_ref.shape, 1)
    wcol = jnp.sum(jnp.where(lane == e, w_ref[...], 0.0), axis=1, keepdims=True)
    o_ref[...] += out * wcol


def _moe_dense(hm, wsel, x2, wg, wu, wd, tm):
    n, d = hm.shape
    ne, _, ff = wg.shape
    tm = _pick(n, tm)
    return pl.pallas_call(
        _moe_dense_kernel,
        grid=(n // tm, ne),
        in_specs=[pl.BlockSpec((tm, d), lambda i, e: (i, 0)),
                  pl.BlockSpec((tm, ne), lambda i, e: (i, 0)),
                  pl.BlockSpec((tm, d), lambda i, e: (i, 0)),
                  pl.BlockSpec((1, d, ff), lambda i, e: (e, 0, 0)),
                  pl.BlockSpec((1, d, ff), lambda i, e: (e, 0, 0)),
                  pl.BlockSpec((1, ff, d), lambda i, e: (e, 0, 0))],
        out_specs=pl.BlockSpec((tm, d), lambda i, e: (i, 0)),
        out_shape=jax.ShapeDtypeStruct((n, d), F32),
        compiler_params=pltpu.CompilerParams(
            dimension_semantics=("parallel", "arbitrary"), vmem_limit_bytes=VMEM_LIMIT),
        name="moe_dense",
    )(hm, wsel, x2, wg, wu, wd)


def _prep_post_weights(p):
    row = lambda v: v.reshape(1, -1).astype(F32)
    wr = p["w_router"].T.astype(F32)
    wrhi = wr.astype(BF16)
    wrlo = (wr - wrhi.astype(F32)).astype(BF16)
    return dict(
        wbra=p["mla_w_br"].astype(BF16), wbrb=p["gqa_w_br"].astype(BF16), wmix=p["w_mix_out"].astype(BF16),
        gcross=row(p["g_cross"]), wq=p["xa_w_q"].astype(BF16), gqn=row(p["xa_gqn"]),
        wo=p["xa_w_o"].astype(BF16), gmoe=row(p["g_moe"]), wrhi=wrhi, wrlo=wrlo,
        gmem=row(p["g_mem"]), wkv=p["xa_w_kv"].astype(BF16), gkn=row(p["xa_gkn"]),
        wg=p["moe_w_gate"].astype(BF16), wu=p["moe_w_up"].astype(BF16), wd=p["moe_w_down"].astype(BF16))


def _encoder_layer(x, mem, wm, wp):
    B, S, D = x.shape
    n = B * S
    n_mem = mem.shape[1]
    x2d = x.reshape(n, D)
    qa, ka, va, qg, kg, vg, ga, gb = _mixer_in(x2d, wm, S, tm=256)
    oa = _attention(qa, ka, va, B, S, k_slab=range(MLA_HEADS), v_row=range(MLA_HEADS), tq=256, tk=1024)
    rep = GQA_HEADS // GQA_KV_HEADS
    ob = _attention(qg, kg, vg, B, S, k_slab=[h // rep for h in range(GQA_HEADS)],
                    v_row=[h // rep for h in range(GQA_HEADS)], tq=256, tk=1024)
    kx, vx = _mem_kv(mem.reshape(B * n_mem, D), wp["gmem"], wp["wkv"], wp["gkn"], tm=256)
    x2, hm, aff_t = _post(x2d, oa, ob, ga, gb, kx, vx, wp, S, n_mem, tm=256)
    cap = EC_FACTOR * n // N_EXPERTS
    wsel_t = _topc(aff_t, cap)
    out = _moe_dense(hm, wsel_t.T, x2, wp["wg"], wp["wu"], wp["wd"], tm=1024)
    return out.reshape(B, S, D)


def kernel(x_prompt, x_sample, mem_prompt, mem_sample, g_mix, w_in, mla_gq, mla_w_uq, mla_gkv, mla_w_ukv, mla_gqn, mla_gkn, mla_w_br, gqa_gqn, gqa_gkn, gqa_w_br, w_mix_out, g_cross, g_mem, xa_w_q, xa_w_kv, xa_gqn, xa_gkn, xa_w_o, g_moe, w_router, moe_w_gate, moe_w_up, moe_w_down):
    names = ("g_mix", "w_in", "mla_gq", "mla_w_uq", "mla_gkv", "mla_w_ukv", "mla_gqn", "mla_gkn", "mla_w_br",
             "gqa_gqn", "gqa_gkn", "gqa_w_br", "w_mix_out", "g_cross", "g_mem", "xa_w_q", "xa_w_kv", "xa_gqn",
             "xa_gkn", "xa_w_o", "g_moe", "w_router", "moe_w_gate", "moe_w_up", "moe_w_down")
    vals = (g_mix, w_in, mla_gq, mla_w_uq, mla_gkv, mla_w_ukv, mla_gqn, mla_gkn, mla_w_br,
            gqa_gqn, gqa_gkn, gqa_w_br, w_mix_out, g_cross, g_mem, xa_w_q, xa_w_kv, xa_gqn,
            xa_gkn, xa_w_o, g_moe, w_router, moe_w_gate, moe_w_up, moe_w_down)
    y_prompt, y_sample = x_prompt, x_sample
    depth = w_in.shape[0]
    for l in range(depth):
        p = {k: v[l] for k, v in zip(names, vals)}
        wm = _prep_mixer_weights(p)
        wp = _prep_post_weights(p)
        y_prompt = _encoder_layer(y_prompt, mem_prompt, wm, wp)
        y_sample = _encoder_layer(y_sample, mem_sample, wm, wp)
    return (y_prompt, y_sample)
```

```python
import functools
import math

import jax
import jax.numpy as jnp
from jax import lax
from jax.experimental import pallas as pl
from jax.experimental.pallas import tpu as pltpu

EPS = 1e-6
GRID_W = 64
ROPE_THETA = 10000.0
MLA_HEADS = 8
Q_LORA = 384
KV_LORA = 256
MLA_NOPE = 64
MLA_ROPE = 32
MLA_V = 64
GQA_HEADS = 8
GQA_KV_HEADS = 2
GQA_HD = 64
XA_HEADS = 4
XA_HD = 128
N_EXPERTS = 16
EC_FACTOR = 2

LANE = 128
MXU = 256
VMEM_LIMIT = 56 * 1024 * 1024
LOG2E = math.log2(math.e)

F32 = jnp.float32
BF16 = jnp.bfloat16


def _pick(n, pref):
    t = min(n, pref)
    while n % t:
        t -= LANE
    return t


def _rms_rows(x, g):
    ms = jnp.mean(x * x, axis=-1, keepdims=True)
    return x * lax.rsqrt(ms + EPS) * g


def _group_mean_sq(x, bd):
    x2 = x * x
    hi = x2.astype(BF16)
    lo = (x2 - hi.astype(F32)).astype(BF16)
    w = x.shape[1]
    outs = []
    for c in range(0, w, MXU):
        cw = min(MXU, w - c)
        b = bd[:cw, :cw]
        outs.append(jnp.dot(hi[:, c:c + cw], b, preferred_element_type=F32)
                    + jnp.dot(lo[:, c:c + cw], b, preferred_element_type=F32))
    return outs[0] if len(outs) == 1 else jnp.concatenate(outs, axis=1)


def _group_rms(x, g, bd):
    return x * lax.rsqrt(_group_mean_sq(x, bd) + EPS) * g


def _rope_slabs(x, tab, shift):
    c, s1, s2 = tab[0], tab[1], tab[2]
    outs = []
    for a in range(0, x.shape[1], LANE):
        xs = x[:, a:a + LANE]
        outs.append(xs * c + pltpu.roll(xs, LANE - shift, 1) * s1 + pltpu.roll(xs, shift, 1) * s2)
    return outs[0] if len(outs) == 1 else jnp.concatenate(outs, axis=1)


def _sigmoid(x):
    return 1.0 / (1.0 + jnp.exp(-x))


_SEG_CQ = (0, Q_LORA)
_SEG_CKV = (_SEG_CQ[1], _SEG_CQ[1] + KV_LORA)
_SEG_KR = (_SEG_CKV[1], _SEG_CKV[1] + LANE)
_SEG_QB = (_SEG_KR[1], _SEG_KR[1] + GQA_HEADS * GQA_HD)
_SEG_KB = (_SEG_QB[1], _SEG_QB[1] + GQA_KV_HEADS * LANE)
_W_SLAB = MLA_HEADS * LANE

_NT = (((1,), (1,)), ((), ()))


def _mixer_in_kernel(x_ref, gmix_ref, win_ref, gq_ref, wuq_ref, gkv_ref, wuk_ref, wuvt_ref, wvbt_ref,
                     gqa_ref, gka_ref, gkr_ref, gqg_ref, gkg_ref, bda_ref, bdb_ref, taba_ref, tabb_ref,
                     qa_ref, ka_ref, vat_ref, qg_ref, kg_ref, vgt_ref, ga_ref, gb_ref, *, d_model):
    h = _rms_rows(x_ref[...], gmix_ref[...]).astype(BF16)

    def proj(seg):
        return jnp.dot(h, win_ref[:, seg[0]:seg[1]], preferred_element_type=F32)

    bda = bda_ref[...]
    bdb = bdb_ref[...]
    taba = taba_ref[...]
    tabb = tabb_ref[...]

    cq = _rms_rows(proj(_SEG_CQ), gq_ref[...]).astype(BF16)
    qa = jnp.dot(cq, wuq_ref[...], preferred_element_type=F32)
    qa = _rope_slabs(_group_rms(qa, gqa_ref[...], bda), taba, MLA_ROPE // 2)
    qa_ref[...] = (qa * ((MLA_NOPE + MLA_ROPE) ** -0.5 * LOG2E)).astype(BF16)

    ckv = _rms_rows(proj(_SEG_CKV), gkv_ref[...]).astype(BF16)
    kn = _group_rms(jnp.dot(ckv, wuk_ref[...], preferred_element_type=F32), gka_ref[...], bdb)
    kr = _rope_slabs(_group_rms(proj(_SEG_KR), gkr_ref[...], bda), taba, MLA_ROPE // 2)
    ka_ref[...] = (kn + jnp.concatenate([kr] * MLA_HEADS, axis=1)).astype(BF16)
    vat_ref[...] = lax.dot_general(wuvt_ref[...], ckv, _NT, preferred_element_type=F32).astype(BF16)

    qg = _rope_slabs(_group_rms(proj(_SEG_QB), gqg_ref[...], bdb), tabb, GQA_HD // 2) * (GQA_HD ** -0.5 * LOG2E)
    lane = lax.broadcasted_iota(jnp.int32, (1, LANE), 1)
    low = lane < GQA_HD
    parts = []
    for p in range(GQA_HEADS // 2):
        s = qg[:, p * LANE:(p + 1) * LANE]
        parts.append(jnp.where(low, s, 0.0))
        parts.append(jnp.where(low, 0.0, s))
    qg_ref[...] = jnp.concatenate(parts, axis=1).astype(BF16)
    kg = _rope_slabs(_group_rms(proj(_SEG_KB), gkg_ref[...], bdb), tabb, GQA_HD // 2)
    kg_ref[...] = kg.astype(BF16)
    vgt_ref[...] = lax.dot_general(wvbt_ref[...], h, _NT, preferred_element_type=F32).astype(BF16)

    g0 = _SEG_KB[1]
    ga_ref[...] = _sigmoid(proj((g0, g0 + d_model))).astype(BF16)
    gb_ref[...] = _sigmoid(proj((g0 + d_model, g0 + 2 * d_model))).astype(BF16)


def _rope_tables(S):
    rows = S // GRID_W
    row = jnp.repeat(jnp.arange(rows, dtype=F32), GRID_W)
    col = jnp.tile(jnp.arange(GRID_W, dtype=F32), rows)

    def cs(rot_dim):
        n_ax = rot_dim // 4
        freqs = 1.0 / (ROPE_THETA ** (jnp.arange(n_ax, dtype=F32) / n_ax))
        ang = jnp.concatenate([row[:, None] * freqs, col[:, None] * freqs], axis=-1)
        return jnp.cos(ang), jnp.sin(ang)

    z = lambda w: jnp.zeros((S, w), F32)
    o = lambda w: jnp.ones((S, w), F32)
    ca, sa = cs(MLA_ROPE)
    pad = LANE - MLA_NOPE - MLA_ROPE
    taba = jnp.stack([
        jnp.concatenate([o(MLA_NOPE), ca, ca, z(pad)], axis=1),
        jnp.concatenate([z(MLA_NOPE), -sa, z(MLA_ROPE // 2), z(pad)], axis=1),
        jnp.concatenate([z(MLA_NOPE), z(MLA_ROPE // 2), sa, z(pad)], axis=1)])
    cb, sb = cs(GQA_HD)
    hz = z(GQA_HD // 2)
    tabb = jnp.stack([
        jnp.concatenate([cb, cb, cb, cb], axis=1),
        jnp.concatenate([-sb, hz, -sb, hz], axis=1),
        jnp.concatenate([hz, sb, hz, sb], axis=1)])
    return taba, tabb


def _block_diag(groups):
    idx = jnp.arange(MXU)
    m = jnp.zeros((MXU, MXU), F32)
    for base in range(0, MXU, LANE):
        for start, size in groups:
            inside = (idx >= base + start) & (idx < base + start + size)
            m = m + jnp.where(inside[:, None] & inside[None, :], 1.0 / size, 0.0)
    return m.astype(BF16)


def _prep_mixer_weights(p):
    d = p["w_in"].shape[0]
    w = p["w_in"]
    o = 0
    cq = w[:, o:o + Q_LORA]; o += Q_LORA
    ckv = w[:, o:o + KV_LORA]; o += KV_LORA
    kr = w[:, o:o + MLA_ROPE]; o += MLA_ROPE
    qb = w[:, o:o + GQA_HEADS * GQA_HD]; o += GQA_HEADS * GQA_HD
    kb = w[:, o:o + GQA_KV_HEADS * GQA_HD]; o += GQA_KV_HEADS * GQA_HD
    vb = w[:, o:o + GQA_KV_HEADS * GQA_HD]; o += GQA_KV_HEADS * GQA_HD
    ga = w[:, o:o + d]; o += d
    gb = w[:, o:o + d]
    zc = lambda n: jnp.zeros((d, n), w.dtype)
    dup = lambda m: jnp.concatenate(
        [m[:, g * GQA_HD:(g + 1) * GQA_HD] for g in range(GQA_KV_HEADS) for _ in range(2)], axis=1)
    win = jnp.concatenate(
        [cq, ckv, zc(MLA_NOPE), kr, zc(LANE - MLA_NOPE - MLA_ROPE), qb, dup(kb), ga, gb], axis=1)

    dq = MLA_NOPE + MLA_ROPE
    wuq = p["mla_w_uq"].reshape(Q_LORA, MLA_HEADS, dq)
    wuq = jnp.pad(wuq, ((0, 0), (0, 0), (0, LANE - dq))).reshape(Q_LORA, _W_SLAB)
    wukv = p["mla_w_ukv"].reshape(KV_LORA, MLA_HEADS, MLA_NOPE + MLA_V)
    wk = jnp.pad(wukv[:, :, :MLA_NOPE], ((0, 0), (0, 0), (0, LANE - MLA_NOPE))).reshape(KV_LORA, _W_SLAB)
    wvt = wukv[:, :, MLA_NOPE:].reshape(KV_LORA, MLA_HEADS * MLA_V).T

    row = lambda v: v.reshape(1, -1).astype(F32)
    zl = lambda n: jnp.zeros((n,), F32)
    gqn, gkn = p["mla_gqn"], p["mla_gkn"]
    pad = LANE - dq
    gqa = jnp.tile(jnp.concatenate([gqn, zl(pad)]), MLA_HEADS)
    gka = jnp.tile(jnp.concatenate([gkn[:MLA_NOPE], zl(LANE - MLA_NOPE)]), MLA_HEADS)
    gkr = jnp.concatenate([zl(MLA_NOPE), gkn[MLA_NOPE:], zl(pad)])
    gqg = jnp.tile(p["gqa_gqn"], GQA_HEADS)
    gkg = jnp.tile(p["gqa_gkn"], 2 * GQA_KV_HEADS)
    return dict(
        gmix=row(p["g_mix"]), win=win.astype(BF16), gq=row(p["mla_gq"]), wuq=wuq.astype(BF16),
        gkv=row(p["mla_gkv"]), wuk=wk.astype(BF16), wuvt=wvt.astype(BF16), wvbt=vb.T.astype(BF16),
        gqa=row(gqa), gka=row(gka), gkr=row(gkr),
        gqg=row(gqg), gkg=row(gkg),
        bda=_block_diag([(0, MLA_NOPE), (MLA_NOPE, MLA_ROPE)]),
        bdb=_block_diag([(0, GQA_HD), (GQA_HD, GQA_HD)]))


def _mixer_in(x2d, w, S, tm):
    n, d = x2d.shape
    tm = _pick(S, tm)
    nt_s = S // tm
    taba, tabb = _rope_tables(S)
    full = lambda a: pl.BlockSpec(a.shape, lambda i: (0,) * a.ndim)
    rows = lambda wd: pl.BlockSpec((tm, wd), lambda i: (i, 0))
    tab = pl.BlockSpec((3, tm, LANE), lambda i: (0, i % nt_s, 0))
    consts = [w[k] for k in ("gmix", "win", "gq", "wuq", "gkv", "wuk", "wuvt", "wvbt", "gqa", "gka", "gkr",
                             "gqg", "gkg", "bda", "bdb")]
    cols = lambda ht: pl.BlockSpec((ht, tm), lambda i: (0, i))
    outs = ((_W_SLAB, False), (_W_SLAB, False), (MLA_HEADS * MLA_V, True), (_W_SLAB, False),
            (GQA_KV_HEADS * LANE, False), (GQA_KV_HEADS * GQA_HD, True), (d, False), (d, False))
    return pl.pallas_call(
        functools.partial(_mixer_in_kernel, d_model=d),
        grid=(n // tm,),
        in_specs=[rows(d)] + [full(c) for c in consts] + [tab, tab],
        out_specs=[cols(wd) if t else rows(wd) for wd, t in outs],
        out_shape=[jax.ShapeDtypeStruct((wd, n) if t else (n, wd), BF16) for wd, t in outs],
        compiler_params=pltpu.CompilerParams(dimension_semantics=("parallel",), vmem_limit_bytes=VMEM_LIMIT),
        name="mixer_in",
    )(x2d, *consts, taba, tabb)


def _attn_kernel(q_ref, k_ref, vt_ref, o_ref, m_scr, l_scr, acc_scr, s_scr, *, k_slab, v_row, tk, hd):
    nheads = len(k_slab)
    m_scr[...] = jnp.full(m_scr.shape, -jnp.inf, F32)
    l_scr[...] = jnp.zeros(l_scr.shape, F32)
    acc_scr[...] = jnp.zeros(acc_scr.shape, F32)

    def step(j, carry):
        off = pl.multiple_of(j * tk, tk)

        def scores(h):
            q = q_ref[:, h * LANE:(h + 1) * LANE]
            k = k_ref[pl.ds(off, tk), k_slab[h] * LANE:(k_slab[h] + 1) * LANE]
            return lax.dot_general(k, q, _NT, preferred_element_type=F32)

        def weighted_values(h, p):
            vt = vt_ref[v_row[h] * hd:(v_row[h] + 1) * hd, pl.ds(off, tk)]
            return jnp.dot(vt, p, preferred_element_type=F32)

        stats = []
        for h in range(nheads):
            s = scores(h)
            s_scr[h] = s
            m_prev = m_scr[h]
            m_new = jnp.maximum(m_prev, jnp.max(s, axis=0, keepdims=True))
            m_scr[h] = m_new
            stats.append((m_prev, m_new))
        for h in range(nheads):
            m_prev, m_new = stats[h]
            alpha = jnp.exp2(m_prev - m_new)
            p = jnp.exp2(s_scr[h] - m_new)
            l_scr[h] = alpha * l_scr[h] + jnp.sum(p, axis=0, keepdims=True)
            acc_scr[h] = alpha * acc_scr[h] + weighted_values(h, p.astype(BF16))
        return carry

    lax.fori_loop(0, k_ref.shape[0] // tk, step, 0)
    out_t = jnp.concatenate([acc_scr[h] / l_scr[h] for h in range(nheads)], axis=0)
    o_ref[...] = out_t.T.astype(o_ref.dtype)


def _attention(q, k, vt, B, S, k_slab, v_row, tq, tk):
    n = q.shape[0]
    nheads = len(k_slab)
    hd = LANE // 2
    tq, tk = _pick(S, tq), _pick(S, tk)
    nq = S // tq
    return pl.pallas_call(
        functools.partial(_attn_kernel, k_slab=tuple(k_slab), v_row=tuple(v_row), tk=tk, hd=hd),
        grid=(B, nq),
        in_specs=[pl.BlockSpec((tq, q.shape[1]), lambda b, i: (b * nq + i, 0)),
                  pl.BlockSpec((S, k.shape[1]), lambda b, i: (b, 0)),
                  pl.BlockSpec((vt.shape[0], S), lambda b, i: (0, b))],
        out_specs=pl.BlockSpec((tq, nheads * hd), lambda b, i: (b * nq + i, 0)),
        out_shape=jax.ShapeDtypeStruct((n, nheads * hd), BF16),
        scratch_shapes=[pltpu.VMEM((nheads, 1, tq), F32), pltpu.VMEM((nheads, 1, tq), F32),
                        pltpu.VMEM((nheads, hd, tq), F32), pltpu.VMEM((nheads, tk, tq), F32)],
        compiler_params=pltpu.CompilerParams(
            dimension_semantics=("parallel", "arbitrary"), vmem_limit_bytes=VMEM_LIMIT),
        name="attention",
    )(q, k, vt)


def _mem_kv_kernel(mem_ref, gmem_ref, wkv_ref, gkn_ref, kx_ref, vx_ref):
    m = _rms_rows(mem_ref[...], gmem_ref[...]).astype(BF16)
    kv = jnp.dot(m, wkv_ref[...], preferred_element_type=F32)
    w = XA_HEADS * XA_HD
    g = gkn_ref[...]
    parts = [_rms_rows(kv[:, h * XA_HD:(h + 1) * XA_HD], g) for h in range(XA_HEADS)]
    kx_ref[...] = jnp.concatenate(parts, axis=1).astype(BF16)
    vx_ref[...] = kv[:, w:].astype(BF16)


def _mem_kv(mem2d, gmem, wkv, gkn, tm):
    n, d = mem2d.shape
    tm = _pick(n, tm)
    w = XA_HEADS * XA_HD
    full = lambda a: pl.BlockSpec(a.shape, lambda i: (0,) * a.ndim)
    return pl.pallas_call(
        _mem_kv_kernel,
        grid=(n // tm,),
        in_specs=[pl.BlockSpec((tm, d), lambda i: (i, 0)), full(gmem), full(wkv), full(gkn)],
        out_specs=[pl.BlockSpec((tm, w), lambda i: (i, 0))] * 2,
        out_shape=[jax.ShapeDtypeStruct((n, w), BF16)] * 2,
        compiler_params=pltpu.CompilerParams(dimension_semantics=("parallel",), vmem_limit_bytes=VMEM_LIMIT),
        name="mem_kv",
    )(mem2d, gmem, wkv, gkn)


def _post_kernel(x_ref, oa_ref, ob_ref, ga_ref, gb_ref, wbra_ref, wbrb_ref, wmix_ref, gcross_ref, wq_ref,
                 gqn_ref, kx_ref, vx_ref, wo_ref, gmoe_ref, wrhi_ref, wrlo_ref, x2_ref, hm_ref, aff_ref):
    ba = jnp.dot(oa_ref[...], wbra_ref[...], preferred_element_type=F32)
    bb = jnp.dot(ob_ref[...], wbrb_ref[...], preferred_element_type=F32)
    mixed = ga_ref[...].astype(F32) * ba + gb_ref[...].astype(F32) * bb
    x1 = x_ref[...] + jnp.dot(mixed.astype(BF16), wmix_ref[...], preferred_element_type=F32)

    hc = _rms_rows(x1, gcross_ref[...]).astype(BF16)
    q = jnp.dot(hc, wq_ref[...], preferred_element_type=F32)
    gqn = gqn_ref[...]
    outs = []
    for h in range(XA_HEADS):
        sl = slice(h * XA_HD, (h + 1) * XA_HD)
        qh = (_rms_rows(q[:, sl], gqn) * (XA_HD ** -0.5)).astype(BF16)
        s = lax.dot_general(qh, kx_ref[:, sl], (((1,), (1,)), ((), ())), preferred_element_type=F32)
        p = jnp.exp(s - jnp.max(s, axis=1, keepdims=True))
        l = jnp.sum(p, axis=1, keepdims=True)
        outs.append(jnp.dot(p.astype(BF16), vx_ref[:, sl], preferred_element_type=F32) / l)
    ox = jnp.concatenate(outs, axis=1).astype(BF16)
    x2 = x1 + jnp.dot(ox, wo_ref[...], preferred_element_type=F32)
    x2_ref[...] = x2

    hm = _rms_rows(x2, gmoe_ref[...])
    hi = hm.astype(BF16)
    lo = (hm - hi.astype(F32)).astype(BF16)
    hm_ref[...] = hi
    dn = (((1,), (1,)), ((), ()))
    wrhi = wrhi_ref[...]
    logits = (lax.dot_general(wrhi, hi, dn, preferred_element_type=F32)
              + lax.dot_general(wrhi, lo, dn, preferred_element_type=F32)
              + lax.dot_general(wrlo_ref[...], hi, dn, preferred_element_type=F32))
    e = jnp.exp(logits - jnp.max(logits, axis=0, keepdims=True))
    aff_ref[...] = e / jnp.sum(e, axis=0, keepdims=True)


def _post(x2d, oa, ob, ga, gb, kx, vx, w, S, n_mem, tm):
    n, d = x2d.shape
    tm = _pick(S, tm)
    nt_s = S // tm
    ne = w["wrhi"].shape[0]
    full = lambda a: pl.BlockSpec(a.shape, lambda i: (0,) * a.ndim)
    rows = lambda wd: pl.BlockSpec((tm, wd), lambda i: (i, 0))
    memblk = pl.BlockSpec((n_mem, kx.shape[1]), lambda i: (i // nt_s, 0))
    c = [w[k] for k in ("wbra", "wbrb", "wmix", "gcross", "wq", "gqn")]
    c2 = [w[k] for k in ("wo", "gmoe", "wrhi", "wrlo")]
    return pl.pallas_call(
        _post_kernel,
        grid=(n // tm,),
        in_specs=[rows(d), rows(oa.shape[1]), rows(ob.shape[1]), rows(d), rows(d)] + [full(a) for a in c]
                 + [memblk, memblk] + [full(a) for a in c2],
        out_specs=[rows(d), rows(d), pl.BlockSpec((ne, tm), lambda i: (0, i))],
        out_shape=[jax.ShapeDtypeStruct((n, d), F32), jax.ShapeDtypeStruct((n, d), BF16),
                   jax.ShapeDtypeStruct((ne, n), F32)],
        compiler_params=pltpu.CompilerParams(dimension_semantics=("parallel",), vmem_limit_bytes=VMEM_LIMIT),
        name="post_mix",
    )(x2d, oa, ob, ga, gb, *c, kx, vx, *c2)


_SEL_CHUNK = 2048
_TIE_CHUNK = 512


def _topc_kernel(aff_ref, tri_ref, wsel_ref, pos_ref, *, cap):
    ne, n = aff_ref.shape
    chunk = min(_SEL_CHUNK, n)
    nchunks = n // chunk

    def bits_at(c, width):
        return pltpu.bitcast(aff_ref[:, pl.ds(pl.multiple_of(c * width, width), width)], jnp.int32)

    def count(pred_fn):
        def body(c, acc):
            return acc + pred_fn(bits_at(c, chunk)).astype(jnp.int32)
        acc = lax.fori_loop(0, nchunks, body, jnp.zeros((ne, chunk), jnp.int32))
        return jnp.sum(acc, axis=1, keepdims=True)

    def bit_step(i, t):
        cand = t | jnp.left_shift(jnp.int32(1), 30 - i)
        return jnp.where(count(lambda b: b >= cand) >= cap, cand, t)

    thr = lax.fori_loop(0, 31, bit_step, jnp.zeros((ne, 1), jnp.int32))
    need = (cap - count(lambda b: b > thr)).astype(F32)

    tchunk = min(_TIE_CHUNK, n)
    tri = tri_ref[...]

    def tie_step(c, carry):
        run_eq, run_sel = carry
        sl = pl.ds(pl.multiple_of(c * tchunk, tchunk), tchunk)
        a = aff_ref[:, sl]
        b = pltpu.bitcast(a, jnp.int32)
        eq = b == thr
        eqf = jnp.where(eq, 1.0, 0.0)
        before = run_eq + jnp.dot(eqf.astype(BF16), tri, preferred_element_type=F32)
        sel = (b > thr) | (eq & (before < need))
        self_ = jnp.where(sel, 1.0, 0.0)
        rank = run_sel + jnp.dot(self_.astype(BF16), tri, preferred_element_type=F32)
        wsel_ref[:, sl] = jnp.where(sel, a, 0.0)
        pos_ref[:, sl] = jnp.where(sel, rank, -1.0).astype(jnp.int32)
        return (run_eq + jnp.sum(eqf, axis=1, keepdims=True), run_sel + jnp.sum(self_, axis=1, keepdims=True))

    zero = jnp.zeros((ne, 1), F32)
    lax.fori_loop(0, n // tchunk, tie_step, (zero, zero))


def _topc(aff_t, cap):
    ne, n = aff_t.shape
    tchunk = min(_TIE_CHUNK, n)
    idx = jnp.arange(tchunk)
    tri = (idx[:, None] < idx[None, :]).astype(BF16)
    return pl.pallas_call(
        functools.partial(_topc_kernel, cap=cap),
        out_shape=[jax.ShapeDtypeStruct((ne, n), F32), jax.ShapeDtypeStruct((ne, n), jnp.int32)],
        compiler_params=pltpu.CompilerParams(vmem_limit_bytes=VMEM_LIMIT),
        name="expert_choice_select",
    )(aff_t, tri)


_MOE_SUB = 256
_MOE_TILE = 1024
_GRANULE = 16


def _moe_ffn_kernel(off_ref, hm_ref, pos_ref, w_ref, wg_ref, wu_ref, wd_ref, o_ref, stage, vstage,
                    *, nsub, tr, nc1):
    e = pl.program_id(0)
    c = pl.program_id(1)
    sub = hm_ref.shape[0] // nsub
    win = sub + _GRANULE

    @pl.when(c == 0)
    def _():
        stage[...] = jnp.zeros(stage.shape, stage.dtype)
        vstage[...] = jnp.zeros(vstage.shape, vstage.dtype)

    base = e * nc1 + c * nsub
    flushed = (off_ref[base] // tr) * tr
    riota = lax.broadcasted_iota(jnp.int32, (win, sub), 0)
    for i in range(nsub):
        k = off_ref[base + i] - flushed
        k16 = pl.multiple_of((k // _GRANULE) * _GRANULE, _GRANULE)
        cols = slice(i * sub, (i + 1) * sub)
        hit = (pos_ref[0, :, cols] - (flushed + k16)) == riota
        onehot = jnp.where(hit, 1.0, 0.0).astype(BF16)
        x = jnp.dot(onehot, hm_ref[cols, :], preferred_element_type=F32)
        rows = pl.ds(k16, win)
        stage[rows, :] = stage[rows, :] + x.astype(BF16)
        v = jnp.sum(jnp.where(hit, w_ref[0, :, cols], 0.0), axis=1, keepdims=True)
        vstage[rows, :] = vstage[rows, :] + v

    @pl.when(off_ref[base + nsub] - flushed >= tr)
    def _():
        xs = stage[0:tr, :]
        g = jnp.dot(xs, wg_ref[0], preferred_element_type=F32)
        u = jnp.dot(xs, wu_ref[0], preferred_element_type=F32)
        hid = (g * _sigmoid(g) * u).astype(BF16)
        out = jnp.dot(hid, wd_ref[0], preferred_element_type=F32)
        o_ref[...] = (out * vstage[0:tr, :]).astype(o_ref.dtype)
        rest = stage.shape[0] - tr
        stage[0:rest, :] = stage[tr:, :]
        stage[rest:, :] = jnp.zeros((tr, stage.shape[1]), stage.dtype)
        vstage[0:rest, :] = vstage[tr:, :]
        vstage[rest:, :] = jnp.zeros((tr, 1), vstage.dtype)


def _moe_ffn(off, hm, pos3, wsel3, wg, wu, wd, cap, tr, nc1):
    n, d = hm.shape
    ne, _, ff = wg.shape
    ts = tr
    nsub = ts // _MOE_SUB
    tiles = cap // tr
    grid_spec = pltpu.PrefetchScalarGridSpec(
        num_scalar_prefetch=1,
        grid=(ne, n // ts),
        in_specs=[pl.BlockSpec((ts, d), lambda e, c, off: (c, 0)),
                  pl.BlockSpec((1, 1, ts), lambda e, c, off: (e, 0, c)),
                  pl.BlockSpec((1, 1, ts), lambda e, c, off: (e, 0, c)),
                  pl.BlockSpec((1, d, ff), lambda e, c, off: (e, 0, 0)),
                  pl.BlockSpec((1, d, ff), lambda e, c, off: (e, 0, 0)),
                  pl.BlockSpec((1, ff, d), lambda e, c, off: (e, 0, 0))],
        out_specs=pl.BlockSpec(
            (tr, d), lambda e, c, off: (e * tiles + jnp.minimum(off[e * nc1 + c * nsub] // tr, tiles - 1), 0)),
        scratch_shapes=[pltpu.VMEM((tr + ts + _GRANULE, d), BF16), pltpu.VMEM((tr + ts + _GRANULE, 1), F32)])
    return pl.pallas_call(
        functools.partial(_moe_ffn_kernel, nsub=nsub, tr=tr, nc1=nc1),
        grid_spec=grid_spec,
        out_shape=jax.ShapeDtypeStruct((ne * cap, d), BF16),
        compiler_params=pltpu.CompilerParams(
            dimension_semantics=("arbitrary", "arbitrary"), vmem_limit_bytes=VMEM_LIMIT),
        name="moe_ffn",
    )(off, hm, pos3, wsel3, wg, wu, wd)


def _moe_combine_kernel(off_ref, pos_ref, x2_ref, rows_hbm, y_ref, gbuf, sem, *, ne, nc1, cap):
    c = pl.program_id(0)
    sub = x2_ref.shape[0]
    gran = _GRANULE

    @pl.when(c == 0)
    def _():
        gbuf[...] = jnp.zeros(gbuf.shape, gbuf.dtype)

    def granule_copy(src_row, dst_row):
        return pltpu.make_async_copy(rows_hbm.at[pl.ds(src_row, gran), :], gbuf.at[pl.ds(dst_row, gran), :], sem)

    gb = jnp.int32(0)
    shifts = []
    for e in range(ne):
        a = off_ref[e * nc1 + c]
        b = off_ref[e * nc1 + c + 1]
        a16 = (a // gran) * gran
        ng = jnp.where(b > a, (b - a16 + gran - 1) // gran, 0)

        def issue(g, carry, e=e, a16=a16, gb=gb):
            granule_copy(pl.multiple_of(e * cap + a16 + g * gran, gran),
                         pl.multiple_of(gb + g * gran, gran)).start()
            return carry

        lax.fori_loop(0, ng, issue, 0)
        shifts.append(gb - a16)
        gb = gb + ng * gran

    def wait_one(g, carry):
        granule_copy(0, 0).wait()
        return carry

    lax.fori_loop(0, gb // gran, wait_one, 0)

    y_ref[...] = x2_ref[...]
    blk = MXU

    def kblock(kb, carry):
        r0 = pl.multiple_of(kb * blk, blk)
        riota = lax.broadcasted_iota(jnp.int32, (blk, sub), 0) + r0
        st = jnp.zeros((blk, sub), F32)
        for e in range(ne):
            p = pos_ref[e]
            st = jnp.where(((p + shifts[e]) == riota) & (p >= 0), 1.0, st)
        y_ref[...] += jnp.dot(st.T.astype(BF16), gbuf[pl.ds(r0, blk), :], preferred_element_type=F32)
        return carry

    lax.fori_loop(0, (gb + blk - 1) // blk, kblock, 0)


def _moe_combine(off, pos3, x2, rows, cap, nc1):
    n, d = x2.shape
    ne = pos3.shape[0]
    sub = _MOE_SUB
    gmax = ne * (sub + 2 * _GRANULE)
    gmax = (gmax + MXU - 1) // MXU * MXU
    grid_spec = pltpu.PrefetchScalarGridSpec(
        num_scalar_prefetch=1,
        grid=(n // sub,),
        in_specs=[pl.BlockSpec((ne, 1, sub), lambda c, off: (0, 0, c)),
                  pl.BlockSpec((sub, d), lambda c, off: (c, 0)),
                  pl.BlockSpec(memory_space=pl.ANY)],
        out_specs=pl.BlockSpec((sub, d), lambda c, off: (c, 0)),
        scratch_shapes=[pltpu.VMEM((gmax, d), BF16), pltpu.SemaphoreType.DMA])
    return pl.pallas_call(
        functools.partial(_moe_combine_kernel, ne=ne, nc1=nc1, cap=cap),
        grid_spec=grid_spec,
        out_shape=jax.ShapeDtypeStruct((n, d), F32),
        compiler_params=pltpu.CompilerParams(dimension_semantics=("arbitrary",), vmem_limit_bytes=VMEM_LIMIT),
        name="moe_combine",
    )(off, pos3, x2, rows)


def _moe(hm, x2, aff_t, wg, wu, wd):
    n, d = hm.shape
    ne = aff_t.shape[0]
    cap = EC_FACTOR * n // ne
    tr = min(_MOE_TILE, cap)
    assert cap % tr == 0 and n % tr == 0 and tr % _MOE_SUB == 0, (n, cap, tr)
    wsel, pos = _topc(aff_t, cap)
    nc = n // _MOE_SUB
    cnt = jnp.sum((pos >= 0).reshape(ne, nc, _MOE_SUB), axis=-1, dtype=jnp.int32)
    off = jnp.concatenate([jnp.zeros((ne, 1), jnp.int32), jnp.cumsum(cnt, axis=1, dtype=jnp.int32)], axis=1)
    off = off.reshape(-1)
    pos3 = pos.reshape(ne, 1, n)
    rows = _moe_ffn(off, hm, pos3, wsel.reshape(ne, 1, n), wg, wu, wd, cap, tr, nc + 1)
    return _moe_combine(off, pos3, x2, rows, cap, nc + 1)


def _prep_post_weights(p):
    row = lambda v: v.reshape(1, -1).astype(F32)
    wr = p["w_router"].T.astype(F32)
    wrhi = wr.astype(BF16)
    wrlo = (wr - wrhi.astype(F32)).astype(BF16)
    return dict(
        wbra=p["mla_w_br"].astype(BF16), wbrb=p["gqa_w_br"].astype(BF16), wmix=p["w_mix_out"].astype(BF16),
        gcross=row(p["g_cross"]), wq=p["xa_w_q"].astype(BF16), gqn=row(p["xa_gqn"]),
        wo=p["xa_w_o"].astype(BF16), gmoe=row(p["g_moe"]), wrhi=wrhi, wrlo=wrlo,
        gmem=row(p["g_mem"]), wkv=p["xa_w_kv"].astype(BF16), gkn=row(p["xa_gkn"]),
        wg=p["moe_w_gate"].astype(BF16), wu=p["moe_w_up"].astype(BF16), wd=p["moe_w_down"].astype(BF16))


def _encoder_layer(x, mem, wm, wp):
    B, S, D = x.shape
    n = B * S
    n_mem = mem.shape[1]
    x2d = x.reshape(n, D)
    qa, ka, va, qg, kg, vg, ga, gb = _mixer_in(x2d, wm, S, tm=256)
    oa = _attention(qa, ka, va, B, S, k_slab=range(MLA_HEADS), v_row=range(MLA_HEADS), tq=256, tk=1024)
    rep = GQA_HEADS // GQA_KV_HEADS
    ob = _attention(qg, kg, vg, B, S, k_slab=[h // rep for h in range(GQA_HEADS)],
                    v_row=[h // rep for h in range(GQA_HEADS)], tq=256, tk=1024)
    kx, vx = _mem_kv(mem.reshape(B * n_mem, D), wp["gmem"], wp["wkv"], wp["gkn"], tm=256)
    x2, hm, aff_t = _post(x2d, oa, ob, ga, gb, kx, vx, wp, S, n_mem, tm=256)
    out = _moe(hm, x2, aff_t, wp["wg"], wp["wu"], wp["wd"])
    return out.reshape(B, S, D)


def kernel(x_prompt, x_sample, mem_prompt, mem_sample, g_mix, w_in, mla_gq, mla_w_uq, mla_gkv, mla_w_ukv, mla_gqn, mla_gkn, mla_w_br, gqa_gqn, gqa_gkn, gqa_w_br, w_mix_out, g_cross, g_mem, xa_w_q, xa_w_kv, xa_gqn, xa_gkn, xa_w_o, g_moe, w_router, moe_w_gate, moe_w_up, moe_w_down):
    names = ("g_mix", "w_in", "mla_gq", "mla_w_uq", "mla_gkv", "mla_w_ukv", "mla_gqn", "mla_gkn", "mla_w_br",
             "gqa_gqn", "gqa_gkn", "gqa_w_br", "w_mix_out", "g_cross", "g_mem", "xa_w_q", "xa_w_kv", "xa_gqn",
             "xa_gkn", "xa_w_o", "g_moe", "w_router", "moe_w_gate", "moe_w_up", "moe_w_down")
    vals = (g_mix, w_in, mla_gq, mla_w_uq, mla_gkv, mla_w_ukv, mla_gqn, mla_gkn, mla_w_br,
            gqa_gqn, gqa_gkn, gqa_w_br, w_mix_out, g_cross, g_mem, xa_w_q, xa_w_kv, xa_gqn,
            xa_gkn, xa_w_o, g_moe, w_router, moe_w_gate, moe_w_up, moe_w_down)
    y_prompt, y_sample = x_prompt, x_sample
    depth = w_in.shape[0]
    for l in range(depth):
        p = {k: v[l] for k, v in zip(names, vals)}
        wm = _prep_mixer_weights(p)
        wp = _prep_post_weights(p)
        y_prompt = _encoder_layer(y_prompt, mem_prompt, wm, wp)
        y_sample = _encoder_layer(y_sample, mem_sample, wm, wp)
    return (y_prompt, y_sample)
```

```python
import functools
import math

import jax
import jax.numpy as jnp
from jax import lax
from jax.experimental import pallas as pl
from jax.experimental.pallas import tpu as pltpu

EPS = 1e-6
GRID_W = 64
ROPE_THETA = 10000.0
MLA_HEADS = 8
Q_LORA = 384
KV_LORA = 256
MLA_NOPE = 64
MLA_ROPE = 32
MLA_V = 64
GQA_HEADS = 8
GQA_KV_HEADS = 2
GQA_HD = 64
XA_HEADS = 4
XA_HD = 128
N_EXPERTS = 16
EC_FACTOR = 2

LANE = 128
MXU = 256
VMEM_LIMIT = 56 * 1024 * 1024
LOG2E = math.log2(math.e)

F32 = jnp.float32
BF16 = jnp.bfloat16


def _pick(n, pref):
    t = min(n, pref)
    while n % t:
        t -= LANE
    return t


def _rms_rows(x, g):
    ms = jnp.mean(x * x, axis=-1, keepdims=True)
    return x * lax.rsqrt(ms + EPS) * g


def _group_mean_sq(x, bd):
    x2 = x * x
    hi = x2.astype(BF16)
    lo = (x2 - hi.astype(F32)).astype(BF16)
    w = x.shape[1]
    outs = []
    for c in range(0, w, MXU):
        cw = min(MXU, w - c)
        b = bd[:cw, :cw]
        outs.append(jnp.dot(hi[:, c:c + cw], b, preferred_element_type=F32)
                    + jnp.dot(lo[:, c:c + cw], b, preferred_element_type=F32))
    return outs[0] if len(outs) == 1 else jnp.concatenate(outs, axis=1)


def _group_rms(x, g, bd):
    return x * lax.rsqrt(_group_mean_sq(x, bd) + EPS) * g


def _rope_slabs(x, tab, shift):
    c, s1, s2 = tab[0], tab[1], tab[2]
    outs = []
    for a in range(0, x.shape[1], LANE):
        xs = x[:, a:a + LANE]
        outs.append(xs * c + pltpu.roll(xs, LANE - shift, 1) * s1 + pltpu.roll(xs, shift, 1) * s2)
    return outs[0] if len(outs) == 1 else jnp.concatenate(outs, axis=1)


def _sigmoid(x):
    return 1.0 / (1.0 + jnp.exp(-x))


_SEG_CQ = (0, Q_LORA)
_SEG_CKV = (_SEG_CQ[1], _SEG_CQ[1] + KV_LORA)
_SEG_KR = (_SEG_CKV[1], _SEG_CKV[1] + LANE)
_SEG_QB = (_SEG_KR[1], _SEG_KR[1] + GQA_HEADS * GQA_HD)
_SEG_KB = (_SEG_QB[1], _SEG_QB[1] + GQA_KV_HEADS * LANE)
_W_SLAB = MLA_HEADS * LANE

_NT = (((1,), (1,)), ((), ()))


def _mixer_in_kernel(x_ref, gmix_ref, win_ref, gq_ref, wuq_ref, gkv_ref, wuk_ref, wuvt_ref, wvbt_ref,
                     gqa_ref, gka_ref, gkr_ref, gqg_ref, gkg_ref, bda_ref, bdb_ref, taba_ref, tabb_ref,
                     qa_ref, ka_ref, vat_ref, qg_ref, kg_ref, vgt_ref, ga_ref, gb_ref, *, d_model):
    h = _rms_rows(x_ref[...], gmix_ref[...]).astype(BF16)

    def proj(seg):
        return jnp.dot(h, win_ref[:, seg[0]:seg[1]], preferred_element_type=F32)

    bda = bda_ref[...]
    bdb = bdb_ref[...]
    taba = taba_ref[...]
    tabb = tabb_ref[...]

    cq = _rms_rows(proj(_SEG_CQ), gq_ref[...]).astype(BF16)
    qa = jnp.dot(cq, wuq_ref[...], preferred_element_type=F32)
    qa = _rope_slabs(_group_rms(qa, gqa_ref[...], bda), taba, MLA_ROPE // 2)
    qa_ref[...] = (qa * ((MLA_NOPE + MLA_ROPE) ** -0.5 * LOG2E)).astype(BF16)

    ckv = _rms_rows(proj(_SEG_CKV), gkv_ref[...]).astype(BF16)
    kn = _group_rms(jnp.dot(ckv, wuk_ref[...], preferred_element_type=F32), gka_ref[...], bdb)
    kr = _rope_slabs(_group_rms(proj(_SEG_KR), gkr_ref[...], bda), taba, MLA_ROPE // 2)
    ka_ref[...] = (kn + jnp.concatenate([kr] * MLA_HEADS, axis=1)).astype(BF16)
    vat_ref[...] = lax.dot_general(wuvt_ref[...], ckv, _NT, preferred_element_type=F32).astype(BF16)

    qg = _rope_slabs(_group_rms(proj(_SEG_QB), gqg_ref[...], bdb), tabb, GQA_HD // 2) * (GQA_HD ** -0.5 * LOG2E)
    lane = lax.broadcasted_iota(jnp.int32, (1, LANE), 1)
    low = lane < GQA_HD
    parts = []
    for p in range(GQA_HEADS // 2):
        s = qg[:, p * LANE:(p + 1) * LANE]
        parts.append(jnp.where(low, s, 0.0))
        parts.append(jnp.where(low, 0.0, s))
    qg_ref[...] = jnp.concatenate(parts, axis=1).astype(BF16)
    kg = _rope_slabs(_group_rms(proj(_SEG_KB), gkg_ref[...], bdb), tabb, GQA_HD // 2)
    kg_ref[...] = kg.astype(BF16)
    vgt_ref[...] = lax.dot_general(wvbt_ref[...], h, _NT, preferred_element_type=F32).astype(BF16)

    g0 = _SEG_KB[1]
    ga_ref[...] = _sigmoid(proj((g0, g0 + d_model))).astype(BF16)
    gb_ref[...] = _sigmoid(proj((g0 + d_model, g0 + 2 * d_model))).astype(BF16)


def _rope_tables(S):
    rows = S // GRID_W
    row = jnp.repeat(jnp.arange(rows, dtype=F32), GRID_W)
    col = jnp.tile(jnp.arange(GRID_W, dtype=F32), rows)

    def cs(rot_dim):
        n_ax = rot_dim // 4
        freqs = 1.0 / (ROPE_THETA ** (jnp.arange(n_ax, dtype=F32) / n_ax))
        ang = jnp.concatenate([row[:, None] * freqs, col[:, None] * freqs], axis=-1)
        return jnp.cos(ang), jnp.sin(ang)

    z = lambda w: jnp.zeros((S, w), F32)
    o = lambda w: jnp.ones((S, w), F32)
    ca, sa = cs(MLA_ROPE)
    pad = LANE - MLA_NOPE - MLA_ROPE
    taba = jnp.stack([
        jnp.concatenate([o(MLA_NOPE), ca, ca, z(pad)], axis=1),
        jnp.concatenate([z(MLA_NOPE), -sa, z(MLA_ROPE // 2), z(pad)], axis=1),
        jnp.concatenate([z(MLA_NOPE), z(MLA_ROPE // 2), sa, z(pad)], axis=1)])
    cb, sb = cs(GQA_HD)
    hz = z(GQA_HD // 2)
    tabb = jnp.stack([
        jnp.concatenate([cb, cb, cb, cb], axis=1),
        jnp.concatenate([-sb, hz, -sb, hz], axis=1),
        jnp.concatenate([hz, sb, hz, sb], axis=1)])
    return taba, tabb


def _block_diag(groups):
    idx = jnp.arange(MXU)
    m = jnp.zeros((MXU, MXU), F32)
    for base in range(0, MXU, LANE):
        for start, size in groups:
            inside = (idx >= base + start) & (idx < base + start + size)
            m = m + jnp.where(inside[:, None] & inside[None, :], 1.0 / size, 0.0)
    return m.astype(BF16)


def _prep_mixer_weights(p):
    d = p["w_in"].shape[0]
    w = p["w_in"]
    o = 0
    cq = w[:, o:o + Q_LORA]; o += Q_LORA
    ckv = w[:, o:o + KV_LORA]; o += KV_LORA
    kr = w[:, o:o + MLA_ROPE]; o += MLA_ROPE
    qb = w[:, o:o + GQA_HEADS * GQA_HD]; o += GQA_HEADS * GQA_HD
    kb = w[:, o:o + GQA_KV_HEADS * GQA_HD]; o += GQA_KV_HEADS * GQA_HD
    vb = w[:, o:o + GQA_KV_HEADS * GQA_HD]; o += GQA_KV_HEADS * GQA_HD
    ga = w[:, o:o + d]; o += d
    gb = w[:, o:o + d]
    zc = lambda n: jnp.zeros((d, n), w.dtype)
    dup = lambda m: jnp.concatenate(
        [m[:, g * GQA_HD:(g + 1) * GQA_HD] for g in range(GQA_KV_HEADS) for _ in range(2)], axis=1)
    win = jnp.concatenate(
        [cq, ckv, zc(MLA_NOPE), kr, zc(LANE - MLA_NOPE - MLA_ROPE), qb, dup(kb), ga, gb], axis=1)

    dq = MLA_NOPE + MLA_ROPE
    wuq = p["mla_w_uq"].reshape(Q_LORA, MLA_HEADS, dq)
    wuq = jnp.pad(wuq, ((0, 0), (0, 0), (0, LANE - dq))).reshape(Q_LORA, _W_SLAB)
    wukv = p["mla_w_ukv"].reshape(KV_LORA, MLA_HEADS, MLA_NOPE + MLA_V)
    wk = jnp.pad(wukv[:, :, :MLA_NOPE], ((0, 0), (0, 0), (0, LANE - MLA_NOPE))).reshape(KV_LORA, _W_SLAB)
    wvt = wukv[:, :, MLA_NOPE:].reshape(KV_LORA, MLA_HEADS * MLA_V).T

    row = lambda v: v.reshape(1, -1).astype(F32)
    zl = lambda n: jnp.zeros((n,), F32)
    gqn, gkn = p["mla_gqn"], p["mla_gkn"]
    pad = LANE - dq
    gqa = jnp.tile(jnp.concatenate([gqn, zl(pad)]), MLA_HEADS)
    gka = jnp.tile(jnp.concatenate([gkn[:MLA_NOPE], zl(LANE - MLA_NOPE)]), MLA_HEADS)
    gkr = jnp.concatenate([zl(MLA_NOPE), gkn[MLA_NOPE:], zl(pad)])
    gqg = jnp.tile(p["gqa_gqn"], GQA_HEADS)
    gkg = jnp.tile(p["gqa_gkn"], 2 * GQA_KV_HEADS)
    return dict(
        gmix=row(p["g_mix"]), win=win.astype(BF16), gq=row(p["mla_gq"]), wuq=wuq.astype(BF16),
        gkv=row(p["mla_gkv"]), wuk=wk.astype(BF16), wuvt=wvt.astype(BF16), wvbt=vb.T.astype(BF16),
        gqa=row(gqa), gka=row(gka), gkr=row(gkr),
        gqg=row(gqg), gkg=row(gkg),
        bda=_block_diag([(0, MLA_NOPE), (MLA_NOPE, MLA_ROPE)]),
        bdb=_block_diag([(0, GQA_HD), (GQA_HD, GQA_HD)]))


def _mixer_in(x2d, w, S, tm):
    n, d = x2d.shape
    tm = _pick(S, tm)
    nt_s = S // tm
    taba, tabb = _rope_tables(S)
    full = lambda a: pl.BlockSpec(a.shape, lambda i: (0,) * a.ndim)
    rows = lambda wd: pl.BlockSpec((tm, wd), lambda i: (i, 0))
    tab = pl.BlockSpec((3, tm, LANE), lambda i: (0, i % nt_s, 0))
    consts = [w[k] for k in ("gmix", "win", "gq", "wuq", "gkv", "wuk", "wuvt", "wvbt", "gqa", "gka", "gkr",
                             "gqg", "gkg", "bda", "bdb")]
    cols = lambda ht: pl.BlockSpec((ht, tm), lambda i: (0, i))
    outs = ((_W_SLAB, False), (_W_SLAB, False), (MLA_HEADS * MLA_V, True), (_W_SLAB, False),
            (GQA_KV_HEADS * LANE, False), (GQA_KV_HEADS * GQA_HD, True), (d, False), (d, False))
    return pl.pallas_call(
        functools.partial(_mixer_in_kernel, d_model=d),
        grid=(n // tm,),
        in_specs=[rows(d)] + [full(c) for c in consts] + [tab, tab],
        out_specs=[cols(wd) if t else rows(wd) for wd, t in outs],
        out_shape=[jax.ShapeDtypeStruct((wd, n) if t else (n, wd), BF16) for wd, t in outs],
        compiler_params=pltpu.CompilerParams(dimension_semantics=("parallel",), vmem_limit_bytes=VMEM_LIMIT),
        name="mixer_in",
    )(x2d, *consts, taba, tabb)


def _attn_kernel(q_ref, k_ref, vt_ref, o_ref, m_scr, l_scr, acc_scr, s_scr, *, k_slab, v_row, tk, hd):
    nheads = len(k_slab)
    m_scr[...] = jnp.full(m_scr.shape, -jnp.inf, F32)
    l_scr[...] = jnp.zeros(l_scr.shape, F32)
    acc_scr[...] = jnp.zeros(acc_scr.shape, F32)

    def step(j, carry):
        off = pl.multiple_of(j * tk, tk)

        def scores(h):
            q = q_ref[:, h * LANE:(h + 1) * LANE]
            k = k_ref[pl.ds(off, tk), k_slab[h] * LANE:(k_slab[h] + 1) * LANE]
            return lax.dot_general(k, q, _NT, preferred_element_type=F32)

        def weighted_values(h, p):
            vt = vt_ref[v_row[h] * hd:(v_row[h] + 1) * hd, pl.ds(off, tk)]
            return jnp.dot(vt, p, preferred_element_type=F32)

        stats = []
        for h in range(nheads):
            s = scores(h)
            s_scr[h] = s
            m_prev = m_scr[h]
            m_new = jnp.maximum(m_prev, jnp.max(s, axis=0, keepdims=True))
            m_scr[h] = m_new
            stats.append((m_prev, m_new))
        for h in range(nheads):
            m_prev, m_new = stats[h]
            alpha = jnp.exp2(m_prev - m_new)
            p = jnp.exp2(s_scr[h] - m_new)
            l_scr[h] = alpha * l_scr[h] + jnp.sum(p, axis=0, keepdims=True)
            acc_scr[h] = alpha * acc_scr[h] + weighted_values(h, p.astype(BF16))
        return carry

    lax.fori_loop(0, k_ref.shape[0] // tk, step, 0)
    out_t = jnp.concatenate([acc_scr[h] / l_scr[h] for h in range(nheads)], axis=0)
    o_ref[...] = out_t.T.astype(o_ref.dtype)


def _attention(q, k, vt, B, S, k_slab, v_row, tq, tk):
    n = q.shape[0]
    nheads = len(k_slab)
    hd = LANE // 2
    tq, tk = _pick(S, tq), _pick(S, tk)
    nq = S // tq
    return pl.pallas_call(
        functools.partial(_attn_kernel, k_slab=tuple(k_slab), v_row=tuple(v_row), tk=tk, hd=hd),
        grid=(B, nq),
        in_specs=[pl.BlockSpec((tq, q.shape[1]), lambda b, i: (b * nq + i, 0)),
                  pl.BlockSpec((S, k.shape[1]), lambda b, i: (b, 0)),
                  pl.BlockSpec((vt.shape[0], S), lambda b, i: (0, b))],
        out_specs=pl.BlockSpec((tq, nheads * hd), lambda b, i: (b * nq + i, 0)),
        out_shape=jax.ShapeDtypeStruct((n, nheads * hd), BF16),
        scratch_shapes=[pltpu.VMEM((nheads, 1, tq), F32), pltpu.VMEM((nheads, 1, tq), F32),
                        pltpu.VMEM((nheads, hd, tq), F32), pltpu.VMEM((nheads, tk, tq), F32)],
        compiler_params=pltpu.CompilerParams(
            dimension_semantics=("parallel", "arbitrary"), vmem_limit_bytes=VMEM_LIMIT),
        name="attention",
    )(q, k, vt)


def _mem_kv_kernel(mem_ref, gmem_ref, wkv_ref, gkn_ref, kx_ref, vx_ref):
    m = _rms_rows(mem_ref[...], gmem_ref[...]).astype(BF16)
    kv = jnp.dot(m, wkv_ref[...], preferred_element_type=F32)
    w = XA_HEADS * XA_HD
    g = gkn_ref[...]
    parts = [_rms_rows(kv[:, h * XA_HD:(h + 1) * XA_HD], g) for h in range(XA_HEADS)]
    kx_ref[...] = jnp.concatenate(parts, axis=1).astype(BF16)
    vx_ref[...] = kv[:, w:].astype(BF16)


def _mem_kv(mem2d, gmem, wkv, gkn, tm):
    n, d = mem2d.shape
    tm = _pick(n, tm)
    w = XA_HEADS * XA_HD
    full = lambda a: pl.BlockSpec(a.shape, lambda i: (0,) * a.ndim)
    return pl.pallas_call(
        _mem_kv_kernel,
        grid=(n // tm,),
        in_specs=[pl.BlockSpec((tm, d), lambda i: (i, 0)), full(gmem), full(wkv), full(gkn)],
        out_specs=[pl.BlockSpec((tm, w), lambda i: (i, 0))] * 2,
        out_shape=[jax.ShapeDtypeStruct((n, w), BF16)] * 2,
        compiler_params=pltpu.CompilerParams(dimension_semantics=("parallel",), vmem_limit_bytes=VMEM_LIMIT),
        name="mem_kv",
    )(mem2d, gmem, wkv, gkn)


def _post_kernel(x_ref, oa_ref, ob_ref, ga_ref, gb_ref, wbra_ref, wbrb_ref, wmix_ref, gcross_ref, wq_ref,
                 gqn_ref, kx_ref, vx_ref, wo_ref, gmoe_ref, wrhi_ref, wrlo_ref, x2_ref, hm_ref, aff_ref):
    ba = jnp.dot(oa_ref[...], wbra_ref[...], preferred_element_type=F32)
    bb = jnp.dot(ob_ref[...], wbrb_ref[...], preferred_element_type=F32)
    mixed = ga_ref[...].astype(F32) * ba + gb_ref[...].astype(F32) * bb
    x1 = x_ref[...] + jnp.dot(mixed.astype(BF16), wmix_ref[...], preferred_element_type=F32)

    hc = _rms_rows(x1, gcross_ref[...]).astype(BF16)
    q = jnp.dot(hc, wq_ref[...], preferred_element_type=F32)
    gqn = gqn_ref[...]
    outs = []
    for h in range(XA_HEADS):
        sl = slice(h * XA_HD, (h + 1) * XA_HD)
        qh = (_rms_rows(q[:, sl], gqn) * (XA_HD ** -0.5)).astype(BF16)
        s = lax.dot_general(qh, kx_ref[:, sl], (((1,), (1,)), ((), ())), preferred_element_type=F32)
        p = jnp.exp(s - jnp.max(s, axis=1, keepdims=True))
        l = jnp.sum(p, axis=1, keepdims=True)
        outs.append(jnp.dot(p.astype(BF16), vx_ref[:, sl], preferred_element_type=F32) / l)
    ox = jnp.concatenate(outs, axis=1).astype(BF16)
    x2 = x1 + jnp.dot(ox, wo_ref[...], preferred_element_type=F32)
    x2_ref[...] = x2

    hm = _rms_rows(x2, gmoe_ref[...])
    hi = hm.astype(BF16)
    lo = (hm - hi.astype(F32)).astype(BF16)
    hm_ref[...] = hi
    dn = (((1,), (1,)), ((), ()))
    wrhi = wrhi_ref[...]
    logits = (lax.dot_general(wrhi, hi, dn, preferred_element_type=F32)
              + lax.dot_general(wrhi, lo, dn, preferred_element_type=F32)
              + lax.dot_general(wrlo_ref[...], hi, dn, preferred_element_type=F32))
    e = jnp.exp(logits - jnp.max(logits, axis=0, keepdims=True))
    aff_ref[...] = e / jnp.sum(e, axis=0, keepdims=True)


def _post(x2d, oa, ob, ga, gb, kx, vx, w, S, n_mem, tm):
    n, d = x2d.shape
    tm = _pick(S, tm)
    nt_s = S // tm
    ne = w["wrhi"].shape[0]
    full = lambda a: pl.BlockSpec(a.shape, lambda i: (0,) * a.ndim)
    rows = lambda wd: pl.BlockSpec((tm, wd), lambda i: (i, 0))
    memblk = pl.BlockSpec((n_mem, kx.shape[1]), lambda i: (i // nt_s, 0))
    c = [w[k] for k in ("wbra", "wbrb", "wmix", "gcross", "wq", "gqn")]
    c2 = [w[k] for k in ("wo", "gmoe", "wrhi", "wrlo")]
    return pl.pallas_call(
        _post_kernel,
        grid=(n // tm,),
        in_specs=[rows(d), rows(oa.shape[1]), rows(ob.shape[1]), rows(d), rows(d)] + [full(a) for a in c]
                 + [memblk, memblk] + [full(a) for a in c2],
        out_specs=[rows(d), rows(d), pl.BlockSpec((ne, tm), lambda i: (0, i))],
        out_shape=[jax.ShapeDtypeStruct((n, d), F32), jax.ShapeDtypeStruct((n, d), BF16),
                   jax.ShapeDtypeStruct((ne, n), F32)],
        compiler_params=pltpu.CompilerParams(dimension_semantics=("parallel",), vmem_limit_bytes=VMEM_LIMIT),
        name="post_mix",
    )(x2d, oa, ob, ga, gb, *c, kx, vx, *c2)


_SEL_CHUNK = 2048
_TIE_CHUNK = 512


def _topc_kernel(aff_ref, tri_ref, wsel_ref, pos_ref, *, cap):
    ne, n = aff_ref.shape
    chunk = min(_SEL_CHUNK, n)
    nchunks = n // chunk

    def bits_at(c, width):
        return pltpu.bitcast(aff_ref[:, pl.ds(pl.multiple_of(c * width, width), width)], jnp.int32)

    def count(pred_fn):
        def body(c, acc):
            return acc + pred_fn(bits_at(c, chunk)).astype(jnp.int32)
        acc = lax.fori_loop(0, nchunks, body, jnp.zeros((ne, chunk), jnp.int32))
        return jnp.sum(acc, axis=1, keepdims=True)

    def bit_step(i, t):
        cand = t | jnp.left_shift(jnp.int32(1), 30 - i)
        return jnp.where(count(lambda b: b >= cand) >= cap, cand, t)

    thr = lax.fori_loop(0, 31, bit_step, jnp.zeros((ne, 1), jnp.int32))
    need = (cap - count(lambda b: b > thr)).astype(F32)

    tchunk = min(_TIE_CHUNK, n)
    tri = tri_ref[...]

    def tie_step(c, carry):
        run_eq, run_sel = carry
        sl = pl.ds(pl.multiple_of(c * tchunk, tchunk), tchunk)
        a = aff_ref[:, sl]
        b = pltpu.bitcast(a, jnp.int32)
        eq = b == thr
        eqf = jnp.where(eq, 1.0, 0.0)
        before = run_eq + jnp.dot(eqf.astype(BF16), tri, preferred_element_type=F32)
        sel = (b > thr) | (eq & (before < need))
        self_ = jnp.where(sel, 1.0, 0.0)
        rank = run_sel + jnp.dot(self_.astype(BF16), tri, preferred_element_type=F32)
        wsel_ref[:, sl] = jnp.where(sel, a, 0.0)
        pos_ref[:, sl] = jnp.where(sel, rank, -1.0).astype(jnp.int32)
        return (run_eq + jnp.sum(eqf, axis=1, keepdims=True), run_sel + jnp.sum(self_, axis=1, keepdims=True))

    zero = jnp.zeros((ne, 1), F32)
    lax.fori_loop(0, n // tchunk, tie_step, (zero, zero))


def _topc(aff_t, cap):
    ne, n = aff_t.shape
    tchunk = min(_TIE_CHUNK, n)
    idx = jnp.arange(tchunk)
    tri = (idx[:, None] < idx[None, :]).astype(BF16)
    return pl.pallas_call(
        functools.partial(_topc_kernel, cap=cap),
        out_shape=[jax.ShapeDtypeStruct((ne, n), F32), jax.ShapeDtypeStruct((ne, n), jnp.int32)],
        compiler_params=pltpu.CompilerParams(vmem_limit_bytes=VMEM_LIMIT),
        name="expert_choice_select",
    )(aff_t, tri)


_MOE_SUB = 256
_MOE_TILE = 1024
_GRANULE = 16
_MOE_SHORT_WIN = 64


def _moe_ffn_kernel(off_ref, hm_ref, pos_ref, w_ref, wg_ref, wu_ref, wd_ref, o_ref, stage, vstage,
                    *, nsub, tr, nc1):
    e = pl.program_id(0)
    c = pl.program_id(1)
    sub = hm_ref.shape[0] // nsub
    win = sub + _GRANULE

    @pl.when(c == 0)
    def _():
        stage[...] = jnp.zeros(stage.shape, stage.dtype)
        vstage[...] = jnp.zeros(vstage.shape, vstage.dtype)

    base = e * nc1 + c * nsub
    flushed = (off_ref[base] // tr) * tr

    def append(i, k16, rows_in_window):
        cols = slice(i * sub, (i + 1) * sub)
        riota = lax.broadcasted_iota(jnp.int32, (rows_in_window, sub), 0)
        hit = (pos_ref[0, :, cols] - (flushed + k16)) == riota
        onehot = jnp.where(hit, 1.0, 0.0).astype(BF16)
        x = jnp.dot(onehot, hm_ref[cols, :], preferred_element_type=F32)
        rows = pl.ds(k16, rows_in_window)
        stage[rows, :] = stage[rows, :] + x.astype(BF16)
        v = jnp.sum(jnp.where(hit, w_ref[0, :, cols], 0.0), axis=1, keepdims=True)
        vstage[rows, :] = vstage[rows, :] + v

    for i in range(nsub):
        k = off_ref[base + i] - flushed
        k16 = pl.multiple_of((k // _GRANULE) * _GRANULE, _GRANULE)
        span = off_ref[base + i + 1] - flushed - k16
        lax.cond(span <= _MOE_SHORT_WIN,
                 functools.partial(append, i, k16, _MOE_SHORT_WIN),
                 functools.partial(append, i, k16, win))

    @pl.when(off_ref[base + nsub] - flushed >= tr)
    def _():
        xs = stage[0:tr, :]
        g = jnp.dot(xs, wg_ref[0], preferred_element_type=F32)
        u = jnp.dot(xs, wu_ref[0], preferred_element_type=F32)
        hid = (g * _sigmoid(g) * u).astype(BF16)
        out = jnp.dot(hid, wd_ref[0], preferred_element_type=F32)
        o_ref[...] = (out * vstage[0:tr, :]).astype(o_ref.dtype)
        rest = stage.shape[0] - tr
        stage[0:rest, :] = stage[tr:, :]
        stage[rest:, :] = jnp.zeros((tr, stage.shape[1]), stage.dtype)
        vstage[0:rest, :] = vstage[tr:, :]
        vstage[rest:, :] = jnp.zeros((tr, 1), vstage.dtype)


def _moe_ffn(off, hm, pos3, wsel3, wg, wu, wd, cap, tr, nc1):
    n, d = hm.shape
    ne, _, ff = wg.shape
    ts = tr
    nsub = ts // _MOE_SUB
    tiles = cap // tr
    grid_spec = pltpu.PrefetchScalarGridSpec(
        num_scalar_prefetch=1,
        grid=(ne, n // ts),
        in_specs=[pl.BlockSpec((ts, d), lambda e, c, off: (c, 0)),
                  pl.BlockSpec((1, 1, ts), lambda e, c, off: (e, 0, c)),
                  pl.BlockSpec((1, 1, ts), lambda e, c, off: (e, 0, c)),
                  pl.BlockSpec((1, d, ff), lambda e, c, off: (e, 0, 0)),
                  pl.BlockSpec((1, d, ff), lambda e, c, off: (e, 0, 0)),
                  pl.BlockSpec((1, ff, d), lambda e, c, off: (e, 0, 0))],
        out_specs=pl.BlockSpec(
            (tr, d), lambda e, c, off: (e * tiles + jnp.minimum(off[e * nc1 + c * nsub] // tr, tiles - 1), 0)),
        scratch_shapes=[pltpu.VMEM((tr + ts + _GRANULE, d), BF16), pltpu.VMEM((tr + ts + _GRANULE, 1), F32)])
    return pl.pallas_call(
        functools.partial(_moe_ffn_kernel, nsub=nsub, tr=tr, nc1=nc1),
        grid_spec=grid_spec,
        out_shape=jax.ShapeDtypeStruct((ne * cap, d), BF16),
        compiler_params=pltpu.CompilerParams(
            dimension_semantics=("arbitrary", "arbitrary"), vmem_limit_bytes=VMEM_LIMIT),
        name="moe_ffn",
    )(off, hm, pos3, wsel3, wg, wu, wd)


def _moe_combine_kernel(off_ref, pos_ref, x2_ref, rows_hbm, y_ref, gbuf, st_scr, sem, *, ne, nc1, cap):
    c = pl.program_id(0)
    sub = x2_ref.shape[0]
    gran = _GRANULE

    @pl.when(c == 0)
    def _():
        gbuf[...] = jnp.zeros(gbuf.shape, gbuf.dtype)

    def granule_copy(src_row, dst_row):
        return pltpu.make_async_copy(rows_hbm.at[pl.ds(src_row, gran), :], gbuf.at[pl.ds(dst_row, gran), :], sem)

    gb = jnp.int32(0)
    shifts = []
    starts = [gb]
    for e in range(ne):
        a = off_ref[e * nc1 + c]
        b = off_ref[e * nc1 + c + 1]
        a16 = (a // gran) * gran
        ng = jnp.where(b > a, (b - a16 + gran - 1) // gran, 0)

        def issue(g, carry, e=e, a16=a16, gb=gb):
            granule_copy(pl.multiple_of(e * cap + a16 + g * gran, gran),
                         pl.multiple_of(gb + g * gran, gran)).start()
            return carry

        lax.fori_loop(0, ng, issue, 0)
        shifts.append(gb - a16)
        gb = gb + ng * gran
        starts.append(gb)

    def wait_one(g, carry):
        granule_copy(0, 0).wait()
        return carry

    lax.fori_loop(0, gb // gran, wait_one, 0)

    y_ref[...] = x2_ref[...]
    blk = MXU

    def kblock(kb, carry):
        r0 = pl.multiple_of(kb * blk, blk)
        riota = lax.broadcasted_iota(jnp.int32, (blk, sub), 0) + r0
        st_scr[...] = jnp.zeros(st_scr.shape, F32)
        for e in range(ne):
            @pl.when((starts[e] < r0 + blk) & (starts[e + 1] > r0))
            def _(e=e):
                p = pos_ref[e]
                st_scr[...] = jnp.where(((p + shifts[e]) == riota) & (p >= 0), 1.0, st_scr[...])
        y_ref[...] += jnp.dot(st_scr[...].T.astype(BF16), gbuf[pl.ds(r0, blk), :], preferred_element_type=F32)
        return carry

    lax.fori_loop(0, (gb + blk - 1) // blk, kblock, 0)


def _moe_combine(off, pos3, x2, rows, cap, nc1):
    n, d = x2.shape
    ne = pos3.shape[0]
    sub = _MOE_SUB
    gmax = ne * (sub + 2 * _GRANULE)
    gmax = (gmax + MXU - 1) // MXU * MXU
    grid_spec = pltpu.PrefetchScalarGridSpec(
        num_scalar_prefetch=1,
        grid=(n // sub,),
        in_specs=[pl.BlockSpec((ne, 1, sub), lambda c, off: (0, 0, c)),
                  pl.BlockSpec((sub, d), lambda c, off: (c, 0)),
                  pl.BlockSpec(memory_space=pl.ANY)],
        out_specs=pl.BlockSpec((sub, d), lambda c, off: (c, 0)),
        scratch_shapes=[pltpu.VMEM((gmax, d), BF16), pltpu.VMEM((MXU, sub), F32), pltpu.SemaphoreType.DMA])
    return pl.pallas_call(
        functools.partial(_moe_combine_kernel, ne=ne, nc1=nc1, cap=cap),
        grid_spec=grid_spec,
        out_shape=jax.ShapeDtypeStruct((n, d), F32),
        compiler_params=pltpu.CompilerParams(dimension_semantics=("arbitrary",), vmem_limit_bytes=VMEM_LIMIT),
        name="moe_combine",
    )(off, pos3, x2, rows)


def _moe(hm, x2, aff_t, wg, wu, wd):
    n, d = hm.shape
    ne = aff_t.shape[0]
    cap = EC_FACTOR * n // ne
    tr = min(_MOE_TILE, cap)
    assert cap % tr == 0 and n % tr == 0 and tr % _MOE_SUB == 0, (n, cap, tr)
    wsel, pos = _topc(aff_t, cap)
    nc = n // _MOE_SUB
    cnt = jnp.sum((pos >= 0).reshape(ne, nc, _MOE_SUB), axis=-1, dtype=jnp.int32)
    off = jnp.concatenate([jnp.zeros((ne, 1), jnp.int32), jnp.cumsum(cnt, axis=1, dtype=jnp.int32)], axis=1)
    off = off.reshape(-1)
    pos3 = pos.reshape(ne, 1, n)
    rows = _moe_ffn(off, hm, pos3, wsel.reshape(ne, 1, n), wg, wu, wd, cap, tr, nc + 1)
    return _moe_combine(off, pos3, x2, rows, cap, nc + 1)


def _prep_post_weights(p):
    row = lambda v: v.reshape(1, -1).astype(F32)
    wr = p["w_router"].T.astype(F32)
    wrhi = wr.astype(BF16)
    wrlo = (wr - wrhi.astype(F32)).astype(BF16)
    return dict(
        wbra=p["mla_w_br"].astype(BF16), wbrb=p["gqa_w_br"].astype(BF16), wmix=p["w_mix_out"].astype(BF16),
        gcross=row(p["g_cross"]), wq=p["xa_w_q"].astype(BF16), gqn=row(p["xa_gqn"]),
        wo=p["xa_w_o"].astype(BF16), gmoe=row(p["g_moe"]), wrhi=wrhi, wrlo=wrlo,
        gmem=row(p["g_mem"]), wkv=p["xa_w_kv"].astype(BF16), gkn=row(p["xa_gkn"]),
        wg=p["moe_w_gate"].astype(BF16), wu=p["moe_w_up"].astype(BF16), wd=p["moe_w_down"].astype(BF16))


def _encoder_layer(x, mem, wm, wp):
    B, S, D = x.shape
    n = B * S
    n_mem = mem.shape[1]
    x2d = x.reshape(n, D)
    qa, ka, va, qg, kg, vg, ga, gb = _mixer_in(x2d, wm, S, tm=512)
    oa = _attention(qa, ka, va, B, S, k_slab=range(MLA_HEADS), v_row=range(MLA_HEADS), tq=256, tk=1024)
    rep = GQA_HEADS // GQA_KV_HEADS
    ob = _attention(qg, kg, vg, B, S, k_slab=[h // rep for h in range(GQA_HEADS)],
                    v_row=[h // rep for h in range(GQA_HEADS)], tq=256, tk=1024)
    kx, vx = _mem_kv(mem.reshape(B * n_mem, D), wp["gmem"], wp["wkv"], wp["gkn"], tm=256)
    x2, hm, aff_t = _post(x2d, oa, ob, ga, gb, kx, vx, wp, S, n_mem, tm=512)
    out = _moe(hm, x2, aff_t, wp["wg"], wp["wu"], wp["wd"])
    return out.reshape(B, S, D)


def kernel(x_prompt, x_sample, mem_prompt, mem_sample, g_mix, w_in, mla_gq, mla_w_uq, mla_gkv, mla_w_ukv, mla_gqn, mla_gkn, mla_w_br, gqa_gqn, gqa_gkn, gqa_w_br, w_mix_out, g_cross, g_mem, xa_w_q, xa_w_kv, xa_gqn, xa_gkn, xa_w_o, g_moe, w_router, moe_w_gate, moe_w_up, moe_w_down):
    names = ("g_mix", "w_in", "mla_gq", "mla_w_uq", "mla_gkv", "mla_w_ukv", "mla_gqn", "mla_gkn", "mla_w_br",
             "gqa_gqn", "gqa_gkn", "gqa_w_br", "w_mix_out", "g_cross", "g_mem", "xa_w_q", "xa_w_kv", "xa_gqn",
             "xa_gkn", "xa_w_o", "g_moe", "w_router", "moe_w_gate", "moe_w_up", "moe_w_down")
    vals = (g_mix, w_in, mla_gq, mla_w_uq, mla_gkv, mla_w_ukv, mla_gqn, mla_gkn, mla_w_br,
            gqa_gqn, gqa_gkn, gqa_w_br, w_mix_out, g_cross, g_mem, xa_w_q, xa_w_kv, xa_gqn,
            xa_gkn, xa_w_o, g_moe, w_router, moe_w_gate, moe_w_up, moe_w_down)
    y_prompt, y_sample = x_prompt, x_sample
    depth = w_in.shape[0]
    for l in range(depth):
        p = {k: v[l] for k, v in zip(names, vals)}
        wm = _prep_mixer_weights(p)
        wp = _prep_post_weights(p)
        y_prompt = _encoder_layer(y_prompt, mem_prompt, wm, wp)
        y_sample = _encoder_layer(y_sample, mem_sample, wm, wp)
    return (y_prompt, y_sample)
```

```python
import functools
import math

import jax
import jax.numpy as jnp
from jax import lax
from jax.experimental import pallas as pl
from jax.experimental.pallas import tpu as pltpu

EPS = 1e-6
GRID_W = 64
ROPE_THETA = 10000.0
MLA_HEADS = 8
Q_LORA = 384
KV_LORA = 256
MLA_NOPE = 64
MLA_ROPE = 32
MLA_V = 64
GQA_HEADS = 8
GQA_KV_HEADS = 2
GQA_HD = 64
XA_HEADS = 4
XA_HD = 128
N_EXPERTS = 16
EC_FACTOR = 2

LANE = 128
MXU = 256
VMEM_LIMIT = 56 * 1024 * 1024
LOG2E = math.log2(math.e)

F32 = jnp.float32
BF16 = jnp.bfloat16


def _pick(n, pref):
    t = min(n, pref)
    while n % t:
        t -= LANE
    return t


def _rms_rows(x, g):
    ms = jnp.mean(x * x, axis=-1, keepdims=True)
    return x * lax.rsqrt(ms + EPS) * g


def _group_mean_sq(x, bd):
    x2 = x * x
    hi = x2.astype(BF16)
    lo = (x2 - hi.astype(F32)).astype(BF16)
    w = x.shape[1]
    outs = []
    for c in range(0, w, MXU):
        cw = min(MXU, w - c)
        b = bd[:cw, :cw]
        outs.append(jnp.dot(hi[:, c:c + cw], b, preferred_element_type=F32)
                    + jnp.dot(lo[:, c:c + cw], b, preferred_element_type=F32))
    return outs[0] if len(outs) == 1 else jnp.concatenate(outs, axis=1)


def _group_rms(x, g, bd):
    return x * lax.rsqrt(_group_mean_sq(x, bd) + EPS) * g


def _rope_slabs(x, tab, shift):
    c, s1, s2 = tab[0], tab[1], tab[2]
    outs = []
    for a in range(0, x.shape[1], LANE):
        xs = x[:, a:a + LANE]
        outs.append(xs * c + pltpu.roll(xs, LANE - shift, 1) * s1 + pltpu.roll(xs, shift, 1) * s2)
    return outs[0] if len(outs) == 1 else jnp.concatenate(outs, axis=1)


def _sigmoid(x):
    return 1.0 / (1.0 + jnp.exp(-x))


_SEG_CQ = (0, Q_LORA)
_SEG_CKV = (_SEG_CQ[1], _SEG_CQ[1] + KV_LORA)
_SEG_KR = (_SEG_CKV[1], _SEG_CKV[1] + LANE)
_SEG_QB = (_SEG_KR[1], _SEG_KR[1] + GQA_HEADS * GQA_HD)
_SEG_KB = (_SEG_QB[1], _SEG_QB[1] + GQA_KV_HEADS * LANE)
_W_SLAB = MLA_HEADS * LANE

_NT = (((1,), (1,)), ((), ()))

_VROWS = MLA_V + 16


def _with_count_rows(vt, heads):
    tm = vt.shape[1]
    ones = jnp.where(lax.broadcasted_iota(jnp.int32, (16, tm), 0) == 0, 1.0, 0.0)
    parts = []
    for h in range(heads):
        parts += [vt[h * MLA_V:(h + 1) * MLA_V, :], ones]
    return jnp.concatenate(parts, axis=0).astype(BF16)


def _mixer_in_kernel(x_ref, gmix_ref, win_ref, gq_ref, wuq_ref, gkv_ref, wuk_ref, wuvt_ref, wvbt_ref,
                     gqa_ref, gka_ref, gkr_ref, gqg_ref, gkg_ref, bda_ref, bdb_ref, taba_ref, tabb_ref,
                     qa_ref, ka_ref, vat_ref, qg_ref, kg_ref, vgt_ref, ga_ref, gb_ref, *, d_model):
    h = _rms_rows(x_ref[...], gmix_ref[...]).astype(BF16)

    def proj(seg):
        return jnp.dot(h, win_ref[:, seg[0]:seg[1]], preferred_element_type=F32)

    bda = bda_ref[...]
    bdb = bdb_ref[...]
    taba = taba_ref[...]
    tabb = tabb_ref[...]

    cq = _rms_rows(proj(_SEG_CQ), gq_ref[...]).astype(BF16)
    qa = jnp.dot(cq, wuq_ref[...], preferred_element_type=F32)
    qa = _rope_slabs(_group_rms(qa, gqa_ref[...], bda), taba, MLA_ROPE // 2)
    qa_ref[...] = (qa * ((MLA_NOPE + MLA_ROPE) ** -0.5 * LOG2E)).astype(BF16)

    ckv = _rms_rows(proj(_SEG_CKV), gkv_ref[...]).astype(BF16)
    kn = _group_rms(jnp.dot(ckv, wuk_ref[...], preferred_element_type=F32), gka_ref[...], bdb)
    kr = _rope_slabs(_group_rms(proj(_SEG_KR), gkr_ref[...], bda), taba, MLA_ROPE // 2)
    ka_ref[...] = (kn + jnp.concatenate([kr] * MLA_HEADS, axis=1)).astype(BF16)
    vat_ref[...] = _with_count_rows(lax.dot_general(wuvt_ref[...], ckv, _NT, preferred_element_type=F32), MLA_HEADS)

    qg = _rope_slabs(_group_rms(proj(_SEG_QB), gqg_ref[...], bdb), tabb, GQA_HD // 2) * (GQA_HD ** -0.5 * LOG2E)
    lane = lax.broadcasted_iota(jnp.int32, (1, LANE), 1)
    low = lane < GQA_HD
    parts = []
    for p in range(GQA_HEADS // 2):
        s = qg[:, p * LANE:(p + 1) * LANE]
        parts.append(jnp.where(low, s, 0.0))
        parts.append(jnp.where(low, 0.0, s))
    qg_ref[...] = jnp.concatenate(parts, axis=1).astype(BF16)
    kg = _rope_slabs(_group_rms(proj(_SEG_KB), gkg_ref[...], bdb), tabb, GQA_HD // 2)
    kg_ref[...] = kg.astype(BF16)
    vgt_ref[...] = _with_count_rows(lax.dot_general(wvbt_ref[...], h, _NT, preferred_element_type=F32), GQA_KV_HEADS)

    g0 = _SEG_KB[1]
    ga_ref[...] = _sigmoid(proj((g0, g0 + d_model))).astype(BF16)
    gb_ref[...] = _sigmoid(proj((g0 + d_model, g0 + 2 * d_model))).astype(BF16)


def _rope_tables(S):
    rows = S // GRID_W
    row = jnp.repeat(jnp.arange(rows, dtype=F32), GRID_W)
    col = jnp.tile(jnp.arange(GRID_W, dtype=F32), rows)

    def cs(rot_dim):
        n_ax = rot_dim // 4
        freqs = 1.0 / (ROPE_THETA ** (jnp.arange(n_ax, dtype=F32) / n_ax))
        ang = jnp.concatenate([row[:, None] * freqs, col[:, None] * freqs], axis=-1)
        return jnp.cos(ang), jnp.sin(ang)

    z = lambda w: jnp.zeros((S, w), F32)
    o = lambda w: jnp.ones((S, w), F32)
    ca, sa = cs(MLA_ROPE)
    pad = LANE - MLA_NOPE - MLA_ROPE
    taba = jnp.stack([
        jnp.concatenate([o(MLA_NOPE), ca, ca, z(pad)], axis=1),
        jnp.concatenate([z(MLA_NOPE), -sa, z(MLA_ROPE // 2), z(pad)], axis=1),
        jnp.concatenate([z(MLA_NOPE), z(MLA_ROPE // 2), sa, z(pad)], axis=1)])
    cb, sb = cs(GQA_HD)
    hz = z(GQA_HD // 2)
    tabb = jnp.stack([
        jnp.concatenate([cb, cb, cb, cb], axis=1),
        jnp.concatenate([-sb, hz, -sb, hz], axis=1),
        jnp.concatenate([hz, sb, hz, sb], axis=1)])
    return taba, tabb


def _block_diag(groups):
    idx = jnp.arange(MXU)
    m = jnp.zeros((MXU, MXU), F32)
    for base in range(0, MXU, LANE):
        for start, size in groups:
            inside = (idx >= base + start) & (idx < base + start + size)
            m = m + jnp.where(inside[:, None] & inside[None, :], 1.0 / size, 0.0)
    return m.astype(BF16)


def _prep_mixer_weights(p):
    d = p["w_in"].shape[0]
    w = p["w_in"]
    o = 0
    cq = w[:, o:o + Q_LORA]; o += Q_LORA
    ckv = w[:, o:o + KV_LORA]; o += KV_LORA
    kr = w[:, o:o + MLA_ROPE]; o += MLA_ROPE
    qb = w[:, o:o + GQA_HEADS * GQA_HD]; o += GQA_HEADS * GQA_HD
    kb = w[:, o:o + GQA_KV_HEADS * GQA_HD]; o += GQA_KV_HEADS * GQA_HD
    vb = w[:, o:o + GQA_KV_HEADS * GQA_HD]; o += GQA_KV_HEADS * GQA_HD
    ga = w[:, o:o + d]; o += d
    gb = w[:, o:o + d]
    zc = lambda n: jnp.zeros((d, n), w.dtype)
    dup = lambda m: jnp.concatenate(
        [m[:, g * GQA_HD:(g + 1) * GQA_HD] for g in range(GQA_KV_HEADS) for _ in range(2)], axis=1)
    win = jnp.concatenate(
        [cq, ckv, zc(MLA_NOPE), kr, zc(LANE - MLA_NOPE - MLA_ROPE), qb, dup(kb), ga, gb], axis=1)

    dq = MLA_NOPE + MLA_ROPE
    wuq = p["mla_w_uq"].reshape(Q_LORA, MLA_HEADS, dq)
    wuq = jnp.pad(wuq, ((0, 0), (0, 0), (0, LANE - dq))).reshape(Q_LORA, _W_SLAB)
    wukv = p["mla_w_ukv"].reshape(KV_LORA, MLA_HEADS, MLA_NOPE + MLA_V)
    wk = jnp.pad(wukv[:, :, :MLA_NOPE], ((0, 0), (0, 0), (0, LANE - MLA_NOPE))).reshape(KV_LORA, _W_SLAB)
    wvt = wukv[:, :, MLA_NOPE:].reshape(KV_LORA, MLA_HEADS * MLA_V).T

    row = lambda v: v.reshape(1, -1).astype(F32)
    zl = lambda n: jnp.zeros((n,), F32)
    gqn, gkn = p["mla_gqn"], p["mla_gkn"]
    pad = LANE - dq
    gqa = jnp.tile(jnp.concatenate([gqn, zl(pad)]), MLA_HEADS)
    gka = jnp.tile(jnp.concatenate([gkn[:MLA_NOPE], zl(LANE - MLA_NOPE)]), MLA_HEADS)
    gkr = jnp.concatenate([zl(MLA_NOPE), gkn[MLA_NOPE:], zl(pad)])
    gqg = jnp.tile(p["gqa_gqn"], GQA_HEADS)
    gkg = jnp.tile(p["gqa_gkn"], 2 * GQA_KV_HEADS)
    return dict(
        gmix=row(p["g_mix"]), win=win.astype(BF16), gq=row(p["mla_gq"]), wuq=wuq.astype(BF16),
        gkv=row(p["mla_gkv"]), wuk=wk.astype(BF16), wuvt=wvt.astype(BF16), wvbt=vb.T.astype(BF16),
        gqa=row(gqa), gka=row(gka), gkr=row(gkr),
        gqg=row(gqg), gkg=row(gkg),
        bda=_block_diag([(0, MLA_NOPE), (MLA_NOPE, MLA_ROPE)]),
        bdb=_block_diag([(0, GQA_HD), (GQA_HD, GQA_HD)]))


def _mixer_in(x2d, w, S, tm):
    n, d = x2d.shape
    tm = _pick(S, tm)
    nt_s = S // tm
    taba, tabb = _rope_tables(S)
    full = lambda a: pl.BlockSpec(a.shape, lambda i: (0,) * a.ndim)
    rows = lambda wd: pl.BlockSpec((tm, wd), lambda i: (i, 0))
    tab = pl.BlockSpec((3, tm, LANE), lambda i: (0, i % nt_s, 0))
    consts = [w[k] for k in ("gmix", "win", "gq", "wuq", "gkv", "wuk", "wuvt", "wvbt", "gqa", "gka", "gkr",
                             "gqg", "gkg", "bda", "bdb")]
    cols = lambda ht: pl.BlockSpec((ht, tm), lambda i: (0, i))
    outs = ((_W_SLAB, False), (_W_SLAB, False), (MLA_HEADS * _VROWS, True), (_W_SLAB, False),
            (GQA_KV_HEADS * LANE, False), (GQA_KV_HEADS * _VROWS, True), (d, False), (d, False))
    return pl.pallas_call(
        functools.partial(_mixer_in_kernel, d_model=d),
        grid=(n // tm,),
        in_specs=[rows(d)] + [full(c) for c in consts] + [tab, tab],
        out_specs=[cols(wd) if t else rows(wd) for wd, t in outs],
        out_shape=[jax.ShapeDtypeStruct((wd, n) if t else (n, wd), BF16) for wd, t in outs],
        compiler_params=pltpu.CompilerParams(dimension_semantics=("parallel",), vmem_limit_bytes=VMEM_LIMIT),
        name="mixer_in",
    )(x2d, *consts, taba, tabb)


def _attn_kernel(q_ref, k_ref, vt_ref, o_ref, m_scr, mp_scr, acc_scr, s_scr, *, k_slab, v_row, tk, hd):
    nheads = len(k_slab)
    nk = k_ref.shape[0] // tk
    acc_scr[...] = jnp.zeros(acc_scr.shape, F32)

    def scores(h, j, slot, m_run):
        q = q_ref[:, h * LANE:(h + 1) * LANE]
        k = k_ref[pl.ds(j * tk, tk), k_slab[h] * LANE:(k_slab[h] + 1) * LANE]
        s = lax.dot_general(k, q, _NT, preferred_element_type=F32)
        s_scr[slot, h] = s
        mx = jnp.max(s, axis=0, keepdims=True)
        return mx if m_run is None else jnp.maximum(m_run, mx)

    def consume(h, j, slot):
        m_new = m_scr[h]
        alpha = jnp.exp2(mp_scr[h] - m_new)
        p = jnp.exp2(s_scr[slot, h] - m_new)
        vt = vt_ref[v_row[h] * _VROWS:(v_row[h] + 1) * _VROWS, pl.ds(j * tk, tk)]
        acc_scr[h] = alpha * acc_scr[h] + jnp.dot(vt, p.astype(BF16), preferred_element_type=F32)

    for h in range(nheads):
        m0 = scores(h, 0, 0, None)
        m_scr[h] = m0
        mp_scr[h] = m0

    for j in range(nk - 1):
        slot = j % 2
        for h in range(nheads):
            m_cur = m_scr[h]
            m_next = scores(h, j + 1, 1 - slot, m_cur)
            consume(h, j, slot)
            mp_scr[h] = m_cur
            m_scr[h] = m_next
    for h in range(nheads):
        consume(h, nk - 1, (nk - 1) % 2)
    out_t = jnp.concatenate([acc_scr[h, 0:hd, :] / acc_scr[h, hd:hd + 1, :] for h in range(nheads)], axis=0)
    o_ref[...] = out_t.T.astype(o_ref.dtype)


def _attention(q, k, vt, B, S, k_slab, v_row, tq, tk):
    n = q.shape[0]
    nheads = len(k_slab)
    hd = LANE // 2
    tq, tk = _pick(S, tq), _pick(S, tk)
    nq = S // tq
    return pl.pallas_call(
        functools.partial(_attn_kernel, k_slab=tuple(k_slab), v_row=tuple(v_row), tk=tk, hd=hd),
        grid=(B, nq),
        in_specs=[pl.BlockSpec((tq, q.shape[1]), lambda b, i: (b * nq + i, 0)),
                  pl.BlockSpec((S, k.shape[1]), lambda b, i: (b, 0)),
                  pl.BlockSpec((vt.shape[0], S), lambda b, i: (0, b))],
        out_specs=pl.BlockSpec((tq, nheads * hd), lambda b, i: (b * nq + i, 0)),
        out_shape=jax.ShapeDtypeStruct((n, nheads * hd), BF16),
        scratch_shapes=[pltpu.VMEM((nheads, 1, tq), F32), pltpu.VMEM((nheads, 1, tq), F32),
                        pltpu.VMEM((nheads, _VROWS, tq), F32), pltpu.VMEM((2, nheads, tk, tq), F32)],
        compiler_params=pltpu.CompilerParams(
            dimension_semantics=("parallel", "arbitrary"), vmem_limit_bytes=VMEM_LIMIT),
        name="attention",
    )(q, k, vt)


def _mem_kv_kernel(mem_ref, gmem_ref, wkv_ref, gkn_ref, kx_ref, vx_ref):
    m = _rms_rows(mem_ref[...], gmem_ref[...]).astype(BF16)
    kv = jnp.dot(m, wkv_ref[...], preferred_element_type=F32)
    w = XA_HEADS * XA_HD
    g = gkn_ref[...]
    parts = [_rms_rows(kv[:, h * XA_HD:(h + 1) * XA_HD], g) for h in range(XA_HEADS)]
    kx_ref[...] = jnp.concatenate(parts, axis=1).astype(BF16)
    vx_ref[...] = kv[:, w:].astype(BF16)


def _mem_kv(mem2d, gmem, wkv, gkn, tm):
    n, d = mem2d.shape
    tm = _pick(n, tm)
    w = XA_HEADS * XA_HD
    full = lambda a: pl.BlockSpec(a.shape, lambda i: (0,) * a.ndim)
    return pl.pallas_call(
        _mem_kv_kernel,
        grid=(n // tm,),
        in_specs=[pl.BlockSpec((tm, d), lambda i: (i, 0)), full(gmem), full(wkv), full(gkn)],
        out_specs=[pl.BlockSpec((tm, w), lambda i: (i, 0))] * 2,
        out_shape=[jax.ShapeDtypeStruct((n, w), BF16)] * 2,
        compiler_params=pltpu.CompilerParams(dimension_semantics=("parallel",), vmem_limit_bytes=VMEM_LIMIT),
        name="mem_kv",
    )(mem2d, gmem, wkv, gkn)


def _post_kernel(x_ref, oa_ref, ob_ref, ga_ref, gb_ref, wbra_ref, wbrb_ref, wmix_ref, gcross_ref, wq_ref,
                 gqn_ref, kx_ref, vx_ref, wo_ref, gmoe_ref, wrhi_ref, wrlo_ref, x2_ref, hm_ref, aff_ref):
    ba = jnp.dot(oa_ref[...], wbra_ref[...], preferred_element_type=F32)
    bb = jnp.dot(ob_ref[...], wbrb_ref[...], preferred_element_type=F32)
    mixed = ga_ref[...].astype(F32) * ba + gb_ref[...].astype(F32) * bb
    x1 = x_ref[...] + jnp.dot(mixed.astype(BF16), wmix_ref[...], preferred_element_type=F32)

    hc = _rms_rows(x1, gcross_ref[...]).astype(BF16)
    q = jnp.dot(hc, wq_ref[...], preferred_element_type=F32)
    gqn = gqn_ref[...]
    outs = []
    for h in range(XA_HEADS):
        sl = slice(h * XA_HD, (h + 1) * XA_HD)
        qh = (_rms_rows(q[:, sl], gqn) * (XA_HD ** -0.5)).astype(BF16)
        s = lax.dot_general(qh, kx_ref[:, sl], (((1,), (1,)), ((), ())), preferred_element_type=F32)
        p = jnp.exp(s - jnp.max(s, axis=1, keepdims=True))
        l = jnp.sum(p, axis=1, keepdims=True)
        outs.append(jnp.dot(p.astype(BF16), vx_ref[:, sl], preferred_element_type=F32) / l)
    ox = jnp.concatenate(outs, axis=1).astype(BF16)
    x2 = x1 + jnp.dot(ox, wo_ref[...], preferred_element_type=F32)
    x2_ref[...] = x2

    hm = _rms_rows(x2, gmoe_ref[...])
    hi = hm.astype(BF16)
    lo = (hm - hi.astype(F32)).astype(BF16)
    hm_ref[...] = hi
    dn = (((1,), (1,)), ((), ()))
    wrhi = wrhi_ref[...]
    logits = (lax.dot_general(wrhi, hi, dn, preferred_element_type=F32)
              + lax.dot_general(wrhi, lo, dn, preferred_element_type=F32)
              + lax.dot_general(wrlo_ref[...], hi, dn, preferred_element_type=F32))
    e = jnp.exp(logits - jnp.max(logits, axis=0, keepdims=True))
    aff_ref[...] = e / jnp.sum(e, axis=0, keepdims=True)


def _post(x2d, oa, ob, ga, gb, kx, vx, w, S, n_mem, tm):
    n, d = x2d.shape
    tm = _pick(S, tm)
    nt_s = S // tm
    ne = w["wrhi"].shape[0]
    full = lambda a: pl.BlockSpec(a.shape, lambda i: (0,) * a.ndim)
    rows = lambda wd: pl.BlockSpec((tm, wd), lambda i: (i, 0))
    memblk = pl.BlockSpec((n_mem, kx.shape[1]), lambda i: (i // nt_s, 0))
    c = [w[k] for k in ("wbra", "wbrb", "wmix", "gcross", "wq", "gqn")]
    c2 = [w[k] for k in ("wo", "gmoe", "wrhi", "wrlo")]
    return pl.pallas_call(
        _post_kernel,
        grid=(n // tm,),
        in_specs=[rows(d), rows(oa.shape[1]), rows(ob.shape[1]), rows(d), rows(d)] + [full(a) for a in c]
                 + [memblk, memblk] + [full(a) for a in c2],
        out_specs=[rows(d), rows(d), pl.BlockSpec((ne, tm), lambda i: (0, i))],
        out_shape=[jax.ShapeDtypeStruct((n, d), F32), jax.ShapeDtypeStruct((n, d), BF16),
                   jax.ShapeDtypeStruct((ne, n), F32)],
        compiler_params=pltpu.CompilerParams(dimension_semantics=("parallel",), vmem_limit_bytes=VMEM_LIMIT),
        name="post_mix",
    )(x2d, oa, ob, ga, gb, *c, kx, vx, *c2)


_SEL_CHUNK = 2048
_TIE_CHUNK = 512


def _topc_kernel(aff_ref, tri_ref, wsel_ref, pos_ref, *, cap):
    ne, n = aff_ref.shape
    chunk = min(_SEL_CHUNK, n)
    nchunks = n // chunk

    def bits_at(c, width):
        return pltpu.bitcast(aff_ref[:, pl.ds(pl.multiple_of(c * width, width), width)], jnp.int32)

    def count(pred_fn):
        def body(c, acc):
            return acc + pred_fn(bits_at(c, chunk)).astype(jnp.int32)
        acc = lax.fori_loop(0, nchunks, body, jnp.zeros((ne, chunk), jnp.int32))
        return jnp.sum(acc, axis=1, keepdims=True)

    def bit_step(i, t):
        cand = t | jnp.left_shift(jnp.int32(1), 30 - i)
        return jnp.where(count(lambda b: b >= cand) >= cap, cand, t)

    thr = lax.fori_loop(0, 31, bit_step, jnp.zeros((ne, 1), jnp.int32))
    need = (cap - count(lambda b: b > thr)).astype(F32)

    tchunk = min(_TIE_CHUNK, n)
    tri = tri_ref[...]

    def tie_step(c, carry):
        run_eq, run_sel = carry
        sl = pl.ds(pl.multiple_of(c * tchunk, tchunk), tchunk)
        a = aff_ref[:, sl]
        b = pltpu.bitcast(a, jnp.int32)
        eq = b == thr
        eqf = jnp.where(eq, 1.0, 0.0)
        before = run_eq + jnp.dot(eqf.astype(BF16), tri, preferred_element_type=F32)
        sel = (b > thr) | (eq & (before < need))
        self_ = jnp.where(sel, 1.0, 0.0)
        rank = run_sel + jnp.dot(self_.astype(BF16), tri, preferred_element_type=F32)
        wsel_ref[:, sl] = jnp.where(sel, a, 0.0)
        pos_ref[:, sl] = jnp.where(sel, rank, -1.0).astype(jnp.int32)
        return (run_eq + jnp.sum(eqf, axis=1, keepdims=True), run_sel + jnp.sum(self_, axis=1, keepdims=True))

    zero = jnp.zeros((ne, 1), F32)
    lax.fori_loop(0, n // tchunk, tie_step, (zero, zero))


def _topc(aff_t, cap):
    ne, n = aff_t.shape
    tchunk = min(_TIE_CHUNK, n)
    idx = jnp.arange(tchunk)
    tri = (idx[:, None] < idx[None, :]).astype(BF16)
    return pl.pallas_call(
        functools.partial(_topc_kernel, cap=cap),
        out_shape=[jax.ShapeDtypeStruct((ne, n), F32), jax.ShapeDtypeStruct((ne, n), jnp.int32)],
        compiler_params=pltpu.CompilerParams(vmem_limit_bytes=VMEM_LIMIT),
        name="expert_choice_select",
    )(aff_t, tri)


_MOE_SUB = 256
_MOE_TILE = 1024
_GRANULE = 16
_MOE_SHORT_WIN = 64


def _moe_ffn_kernel(off_ref, hm_ref, pos_ref, w_ref, wg_ref, wu_ref, wd_ref, o_ref, stage, vstage,
                    *, nsub, tr, nc1):
    e = pl.program_id(0)
    c = pl.program_id(1)
    sub = hm_ref.shape[0] // nsub
    win = sub + _GRANULE

    @pl.when(c == 0)
    def _():
        stage[...] = jnp.zeros(stage.shape, stage.dtype)
        vstage[...] = jnp.zeros(vstage.shape, vstage.dtype)

    base = e * nc1 + c * nsub
    flushed = (off_ref[base] // tr) * tr

    def append(i, k16, rows_in_window):
        cols = slice(i * sub, (i + 1) * sub)
        riota = lax.broadcasted_iota(jnp.int32, (rows_in_window, sub), 0)
        hit = (pos_ref[0, :, cols] - (flushed + k16)) == riota
        onehot = jnp.where(hit, 1.0, 0.0).astype(BF16)
        x = jnp.dot(onehot, hm_ref[cols, :], preferred_element_type=F32)
        rows = pl.ds(k16, rows_in_window)
        stage[rows, :] = stage[rows, :] + x.astype(BF16)
        v = jnp.sum(jnp.where(hit, w_ref[0, :, cols], 0.0), axis=1, keepdims=True)
        vstage[rows, :] = vstage[rows, :] + v

    for i in range(nsub):
        k = off_ref[base + i] - flushed
        k16 = pl.multiple_of((k // _GRANULE) * _GRANULE, _GRANULE)
        span = off_ref[base + i + 1] - flushed - k16
        lax.cond(span <= _MOE_SHORT_WIN,
                 functools.partial(append, i, k16, _MOE_SHORT_WIN),
                 functools.partial(append, i, k16, win))

    @pl.when(off_ref[base + nsub] - flushed >= tr)
    def _():
        xs = stage[0:tr, :]
        g = jnp.dot(xs, wg_ref[0], preferred_element_type=F32)
        u = jnp.dot(xs, wu_ref[0], preferred_element_type=F32)
        hid = (g * _sigmoid(g) * u).astype(BF16)
        out = jnp.dot(hid, wd_ref[0], preferred_element_type=F32)
        o_ref[...] = (out * vstage[0:tr, :]).astype(o_ref.dtype)
        rest = stage.shape[0] - tr
        stage[0:rest, :] = stage[tr:, :]
        stage[rest:, :] = jnp.zeros((tr, stage.shape[1]), stage.dtype)
        vstage[0:rest, :] = vstage[tr:, :]
        vstage[rest:, :] = jnp.zeros((tr, 1), vstage.dtype)


def _moe_ffn(off, hm, pos3, wsel3, wg, wu, wd, cap, tr, nc1):
    n, d = hm.shape
    ne, _, ff = wg.shape
    ts = tr
    nsub = ts // _MOE_SUB
    tiles = cap // tr
    grid_spec = pltpu.PrefetchScalarGridSpec(
        num_scalar_prefetch=1,
        grid=(ne, n // ts),
        in_specs=[pl.BlockSpec((ts, d), lambda e, c, off: (c, 0)),
                  pl.BlockSpec((1, 1, ts), lambda e, c, off: (e, 0, c)),
                  pl.BlockSpec((1, 1, ts), lambda e, c, off: (e, 0, c)),
                  pl.BlockSpec((1, d, ff), lambda e, c, off: (e, 0, 0)),
                  pl.BlockSpec((1, d, ff), lambda e, c, off: (e, 0, 0)),
                  pl.BlockSpec((1, ff, d), lambda e, c, off: (e, 0, 0))],
        out_specs=pl.BlockSpec(
            (tr, d), lambda e, c, off: (e * tiles + jnp.minimum(off[e * nc1 + c * nsub] // tr, tiles - 1), 0)),
        scratch_shapes=[pltpu.VMEM((tr + ts + _GRANULE, d), BF16), pltpu.VMEM((tr + ts + _GRANULE, 1), F32)])
    return pl.pallas_call(
        functools.partial(_moe_ffn_kernel, nsub=nsub, tr=tr, nc1=nc1),
        grid_spec=grid_spec,
        out_shape=jax.ShapeDtypeStruct((ne * cap, d), BF16),
        compiler_params=pltpu.CompilerParams(
            dimension_semantics=("arbitrary", "arbitrary"), vmem_limit_bytes=VMEM_LIMIT),
        name="moe_ffn",
    )(off, hm, pos3, wsel3, wg, wu, wd)


def _moe_combine_kernel(off_ref, pos_ref, x2_ref, rows_hbm, y_ref, gbuf, st_scr, sems, *, ne, nc1, cap):
    c = pl.program_id(0)
    nc = pl.num_programs(0)
    sub = x2_ref.shape[0]
    gran = _GRANULE
    slot = c % 2

    def granule_copy(src_row, dst_row, sl):
        return pltpu.make_async_copy(rows_hbm.at[pl.ds(src_row, gran), :],
                                     gbuf.at[sl, pl.ds(dst_row, gran), :], sems.at[sl])

    def layout(cc):
        gb = jnp.int32(0)
        shifts, starts, firsts, counts = [], [gb], [], []
        for e in range(ne):
            a = off_ref[e * nc1 + cc]
            b = off_ref[e * nc1 + cc + 1]
            a16 = (a // gran) * gran
            ng = jnp.where(b > a, (b - a16 + gran - 1) // gran, 0)
            shifts.append(gb - a16)
            firsts.append(e * cap + a16)
            counts.append(ng)
            gb = gb + ng * gran
            starts.append(gb)
        return shifts, starts, firsts, counts

    def fetch(cc, sl):
        _, starts, firsts, counts = layout(cc)
        for e in range(ne):
            def issue(g, carry, e=e):
                granule_copy(pl.multiple_of(firsts[e] + g * gran, gran),
                             pl.multiple_of(starts[e] + g * gran, gran), sl).start()
                return carry
            lax.fori_loop(0, counts[e], issue, 0)

    @pl.when(c == 0)
    def _():
        gbuf[...] = jnp.zeros(gbuf.shape, gbuf.dtype)
        fetch(c, slot)

    @pl.when(c + 1 < nc)
    def _():
        fetch(c + 1, 1 - slot)

    shifts, starts, _, _ = layout(c)
    gb = starts[ne]

    def wait_one(g, carry):
        granule_copy(0, 0, slot).wait()
        return carry

    lax.fori_loop(0, gb // gran, wait_one, 0)

    y_ref[...] = x2_ref[...]
    blk = MXU

    def kblock(kb, carry):
        r0 = pl.multiple_of(kb * blk, blk)
        riota = lax.broadcasted_iota(jnp.int32, (blk, sub), 0) + r0
        st_scr[...] = jnp.zeros(st_scr.shape, F32)
        for e in range(ne):
            @pl.when((starts[e] < r0 + blk) & (starts[e + 1] > r0))
            def _(e=e):
                p = pos_ref[e]
                st_scr[...] = jnp.where(((p + shifts[e]) == riota) & (p >= 0), 1.0, st_scr[...])
        y_ref[...] += jnp.dot(st_scr[...].T.astype(BF16), gbuf[slot, pl.ds(r0, blk), :],
                              preferred_element_type=F32)
        return carry

    lax.fori_loop(0, (gb + blk - 1) // blk, kblock, 0)


def _moe_combine(off, pos3, x2, rows, cap, nc1):
    n, d = x2.shape
    ne = pos3.shape[0]
    sub = _MOE_SUB
    gmax = ne * (sub + 2 * _GRANULE)
    gmax = (gmax + MXU - 1) // MXU * MXU
    grid_spec = pltpu.PrefetchScalarGridSpec(
        num_scalar_prefetch=1,
        grid=(n // sub,),
        in_specs=[pl.BlockSpec((ne, 1, sub), lambda c, off: (0, 0, c)),
                  pl.BlockSpec((sub, d), lambda c, off: (c, 0)),
                  pl.BlockSpec(memory_space=pl.ANY)],
        out_specs=pl.BlockSpec((sub, d), lambda c, off: (c, 0)),
        scratch_shapes=[pltpu.VMEM((2, gmax, d), BF16), pltpu.VMEM((MXU, sub), F32),
                        pltpu.SemaphoreType.DMA((2,))])
    return pl.pallas_call(
        functools.partial(_moe_combine_kernel, ne=ne, nc1=nc1, cap=cap),
        grid_spec=grid_spec,
        out_shape=jax.ShapeDtypeStruct((n, d), F32),
        compiler_params=pltpu.CompilerParams(dimension_semantics=("arbitrary",), vmem_limit_bytes=VMEM_LIMIT),
        name="moe_combine",
    )(off, pos3, x2, rows)


def _moe(hm, x2, aff_t, wg, wu, wd):
    n, d = hm.shape
    ne = aff_t.shape[0]
    cap = EC_FACTOR * n // ne
    tr = min(_MOE_TILE, cap)
    assert cap % tr == 0 and n % tr == 0 and tr % _MOE_SUB == 0, (n, cap, tr)
    wsel, pos = _topc(aff_t, cap)
    nc = n // _MOE_SUB
    cnt = jnp.sum((pos >= 0).reshape(ne, nc, _MOE_SUB), axis=-1, dtype=jnp.int32)
    off = jnp.concatenate([jnp.zeros((ne, 1), jnp.int32), jnp.cumsum(cnt, axis=1, dtype=jnp.int32)], axis=1)
    off = off.reshape(-1)
    pos3 = pos.reshape(ne, 1, n)
    rows = _moe_ffn(off, hm, pos3, wsel.reshape(ne, 1, n), wg, wu, wd, cap, tr, nc + 1)
    return _moe_combine(off, pos3, x2, rows, cap, nc + 1)


def _prep_post_weights(p):
    row = lambda v: v.reshape(1, -1).astype(F32)
    wr = p["w_router"].T.astype(F32)
    wrhi = wr.astype(BF16)
    wrlo = (wr - wrhi.astype(F32)).astype(BF16)
    return dict(
        wbra=p["mla_w_br"].astype(BF16), wbrb=p["gqa_w_br"].astype(BF16), wmix=p["w_mix_out"].astype(BF16),
        gcross=row(p["g_cross"]), wq=p["xa_w_q"].astype(BF16), gqn=row(p["xa_gqn"]),
        wo=p["xa_w_o"].astype(BF16), gmoe=row(p["g_moe"]), wrhi=wrhi, wrlo=wrlo,
        gmem=row(p["g_mem"]), wkv=p["xa_w_kv"].astype(BF16), gkn=row(p["xa_gkn"]),
        wg=p["moe_w_gate"].astype(BF16), wu=p["moe_w_up"].astype(BF16), wd=p["moe_w_down"].astype(BF16))


def _encoder_layer(x, mem, wm, wp):
    B, S, D = x.shape
    n = B * S
    n_mem = mem.shape[1]
    x2d = x.reshape(n, D)
    qa, ka, va, qg, kg, vg, ga, gb = _mixer_in(x2d, wm, S, tm=512)
    oa = _attention(qa, ka, va, B, S, k_slab=range(MLA_HEADS), v_row=range(MLA_HEADS), tq=256, tk=512)
    rep = GQA_HEADS // GQA_KV_HEADS
    ob = _attention(qg, kg, vg, B, S, k_slab=[h // rep for h in range(GQA_HEADS)],
                    v_row=[h // rep for h in range(GQA_HEADS)], tq=256, tk=512)
    kx, vx = _mem_kv(mem.reshape(B * n_mem, D), wp["gmem"], wp["wkv"], wp["gkn"], tm=256)
    x2, hm, aff_t = _post(x2d, oa, ob, ga, gb, kx, vx, wp, S, n_mem, tm=512)
    out = _moe(hm, x2, aff_t, wp["wg"], wp["wu"], wp["wd"])
    return out.reshape(B, S, D)


def kernel(x_prompt, x_sample, mem_prompt, mem_sample, g_mix, w_in, mla_gq, mla_w_uq, mla_gkv, mla_w_ukv, mla_gqn, mla_gkn, mla_w_br, gqa_gqn, gqa_gkn, gqa_w_br, w_mix_out, g_cross, g_mem, xa_w_q, xa_w_kv, xa_gqn, xa_gkn, xa_w_o, g_moe, w_router, moe_w_gate, moe_w_up, moe_w_down):
    names = ("g_mix", "w_in", "mla_gq", "mla_w_uq", "mla_gkv", "mla_w_ukv", "mla_gqn", "mla_gkn", "mla_w_br",
             "gqa_gqn", "gqa_gkn", "gqa_w_br", "w_mix_out", "g_cross", "g_mem", "xa_w_q", "xa_w_kv", "xa_gqn",
             "xa_gkn", "xa_w_o", "g_moe", "w_router", "moe_w_gate", "moe_w_up", "moe_w_down")
    vals = (g_mix, w_in, mla_gq, mla_w_uq, mla_gkv, mla_w_ukv, mla_gqn, mla_gkn, mla_w_br,
            gqa_gqn, gqa_gkn, gqa_w_br, w_mix_out, g_cross, g_mem, xa_w_q, xa_w_kv, xa_gqn,
            xa_gkn, xa_w_o, g_moe, w_router, moe_w_gate, moe_w_up, moe_w_down)
    y_prompt, y_sample = x_prompt, x_sample
    depth = w_in.shape[0]
    for l in range(depth):
        p = {k: v[l] for k, v in zip(names, vals)}
        wm = _prep_mixer_weights(p)
        wp = _prep_post_weights(p)
        y_prompt = _encoder_layer(y_prompt, mem_prompt, wm, wp)
        y_sample = _encoder_layer(y_sample, mem_sample, wm, wp)
    return (y_prompt, y_sample)
```

```python
import functools
import math

import jax
import jax.numpy as jnp
from jax import lax
from jax.experimental import pallas as pl
from jax.experimental.pallas import tpu as pltpu

EPS = 1e-6
GRID_W = 64
ROPE_THETA = 10000.0
MLA_HEADS = 8
Q_LORA = 384
KV_LORA = 256
MLA_NOPE = 64
MLA_ROPE = 32
MLA_V = 64
GQA_HEADS = 8
GQA_KV_HEADS = 2
GQA_HD = 64
XA_HEADS = 4
XA_HD = 128
N_EXPERTS = 16
EC_FACTOR = 2

LANE = 128
MXU = 256
VMEM_LIMIT = 56 * 1024 * 1024
LOG2E = math.log2(math.e)

F32 = jnp.float32
BF16 = jnp.bfloat16


def _pick(n, pref):
    t = min(n, pref)
    while n % t:
        t -= LANE
    return t


def _rms_rows(x, g):
    ms = jnp.mean(x * x, axis=-1, keepdims=True)
    return x * lax.rsqrt(ms + EPS) * g


def _group_mean_sq(x, bd):
    x2 = x * x
    hi = x2.astype(BF16)
    lo = (x2 - hi.astype(F32)).astype(BF16)
    w = x.shape[1]
    outs = []
    for c in range(0, w, MXU):
        cw = min(MXU, w - c)
        b = bd[:cw, :cw]
        outs.append(jnp.dot(hi[:, c:c + cw], b, preferred_element_type=F32)
                    + jnp.dot(lo[:, c:c + cw], b, preferred_element_type=F32))
    return outs[0] if len(outs) == 1 else jnp.concatenate(outs, axis=1)


def _group_rms(x, g, bd):
    return x * lax.rsqrt(_group_mean_sq(x, bd) + EPS) * g


def _rope_slabs(x, tab, shift):
    c, s1, s2 = tab[0], tab[1], tab[2]
    outs = []
    for a in range(0, x.shape[1], LANE):
        xs = x[:, a:a + LANE]
        outs.append(xs * c + pltpu.roll(xs, LANE - shift, 1) * s1 + pltpu.roll(xs, shift, 1) * s2)
    return outs[0] if len(outs) == 1 else jnp.concatenate(outs, axis=1)


def _sigmoid(x):
    return 1.0 / (1.0 + jnp.exp(-x))


_SEG_CQ = (0, Q_LORA)
_SEG_CKV = (_SEG_CQ[1], _SEG_CQ[1] + KV_LORA)
_SEG_KR = (_SEG_CKV[1], _SEG_CKV[1] + LANE)
_SEG_QB = (_SEG_KR[1], _SEG_KR[1] + GQA_HEADS * GQA_HD)
_SEG_KB = (_SEG_QB[1], _SEG_QB[1] + GQA_KV_HEADS * LANE)
_W_SLAB = MLA_HEADS * LANE

_NT = (((1,), (1,)), ((), ()))

_VROWS = MLA_V + 16


def _with_count_rows(vt, heads):
    tm = vt.shape[1]
    ones = jnp.where(lax.broadcasted_iota(jnp.int32, (16, tm), 0) == 0, 1.0, 0.0)
    parts = []
    for h in range(heads):
        parts += [vt[h * MLA_V:(h + 1) * MLA_V, :], ones]
    return jnp.concatenate(parts, axis=0).astype(BF16)


def _mixer_in_kernel(x_ref, gmix_ref, win_ref, gq_ref, wuq_ref, gkv_ref, wuk_ref, wuvt_ref, wvbt_ref,
                     gqa_ref, gka_ref, gkr_ref, gqg_ref, gkg_ref, bda_ref, bdb_ref, taba_ref, tabb_ref,
                     qa_ref, ka_ref, vat_ref, qg_ref, kg_ref, vgt_ref, ga_ref, gb_ref, *, d_model):
    h = _rms_rows(x_ref[...], gmix_ref[...]).astype(BF16)

    def proj(seg):
        return jnp.dot(h, win_ref[:, seg[0]:seg[1]], preferred_element_type=F32)

    bda = bda_ref[...]
    bdb = bdb_ref[...]
    taba = taba_ref[...]
    tabb = tabb_ref[...]

    cq = _rms_rows(proj(_SEG_CQ), gq_ref[...]).astype(BF16)
    qa = jnp.dot(cq, wuq_ref[...], preferred_element_type=F32)
    qa = _rope_slabs(_group_rms(qa, gqa_ref[...], bda), taba, MLA_ROPE // 2)
    qa_ref[...] = (qa * ((MLA_NOPE + MLA_ROPE) ** -0.5 * LOG2E)).astype(BF16)

    ckv = _rms_rows(proj(_SEG_CKV), gkv_ref[...]).astype(BF16)
    kn = _group_rms(jnp.dot(ckv, wuk_ref[...], preferred_element_type=F32), gka_ref[...], bdb)
    kr = _rope_slabs(_group_rms(proj(_SEG_KR), gkr_ref[...], bda), taba, MLA_ROPE // 2)
    ka_ref[...] = (kn + jnp.concatenate([kr] * MLA_HEADS, axis=1)).astype(BF16)
    vat_ref[...] = _with_count_rows(lax.dot_general(wuvt_ref[...], ckv, _NT, preferred_element_type=F32), MLA_HEADS)

    qg = _rope_slabs(_group_rms(proj(_SEG_QB), gqg_ref[...], bdb), tabb, GQA_HD // 2) * (GQA_HD ** -0.5 * LOG2E)
    lane = lax.broadcasted_iota(jnp.int32, (1, LANE), 1)
    low = lane < GQA_HD
    parts = []
    for p in range(GQA_HEADS // 2):
        s = qg[:, p * LANE:(p + 1) * LANE]
        parts.append(jnp.where(low, s, 0.0))
        parts.append(jnp.where(low, 0.0, s))
    qg_ref[...] = jnp.concatenate(parts, axis=1).astype(BF16)
    kg = _rope_slabs(_group_rms(proj(_SEG_KB), gkg_ref[...], bdb), tabb, GQA_HD // 2)
    kg_ref[...] = kg.astype(BF16)
    vgt_ref[...] = _with_count_rows(lax.dot_general(wvbt_ref[...], h, _NT, preferred_element_type=F32), GQA_KV_HEADS)

    g0 = _SEG_KB[1]
    ga_ref[...] = _sigmoid(proj((g0, g0 + d_model))).astype(BF16)
    gb_ref[...] = _sigmoid(proj((g0 + d_model, g0 + 2 * d_model))).astype(BF16)


def _rope_tables(S):
    rows = S // GRID_W
    row = jnp.repeat(jnp.arange(rows, dtype=F32), GRID_W)
    col = jnp.tile(jnp.arange(GRID_W, dtype=F32), rows)

    def cs(rot_dim):
        n_ax = rot_dim // 4
        freqs = 1.0 / (ROPE_THETA ** (jnp.arange(n_ax, dtype=F32) / n_ax))
        ang = jnp.concatenate([row[:, None] * freqs, col[:, None] * freqs], axis=-1)
        return jnp.cos(ang), jnp.sin(ang)

    z = lambda w: jnp.zeros((S, w), F32)
    o = lambda w: jnp.ones((S, w), F32)
    ca, sa = cs(MLA_ROPE)
    pad = LANE - MLA_NOPE - MLA_ROPE
    taba = jnp.stack([
        jnp.concatenate([o(MLA_NOPE), ca, ca, z(pad)], axis=1),
        jnp.concatenate([z(MLA_NOPE), -sa, z(MLA_ROPE // 2), z(pad)], axis=1),
        jnp.concatenate([z(MLA_NOPE), z(MLA_ROPE // 2), sa, z(pad)], axis=1)])
    cb, sb = cs(GQA_HD)
    hz = z(GQA_HD // 2)
    tabb = jnp.stack([
        jnp.concatenate([cb, cb, cb, cb], axis=1),
        jnp.concatenate([-sb, hz, -sb, hz], axis=1),
        jnp.concatenate([hz, sb, hz, sb], axis=1)])
    return taba, tabb


def _block_diag(groups):
    idx = jnp.arange(MXU)
    m = jnp.zeros((MXU, MXU), F32)
    for base in range(0, MXU, LANE):
        for start, size in groups:
            inside = (idx >= base + start) & (idx < base + start + size)
            m = m + jnp.where(inside[:, None] & inside[None, :], 1.0 / size, 0.0)
    return m.astype(BF16)


def _prep_mixer_weights(p):
    d = p["w_in"].shape[0]
    w = p["w_in"]
    o = 0
    cq = w[:, o:o + Q_LORA]; o += Q_LORA
    ckv = w[:, o:o + KV_LORA]; o += KV_LORA
    kr = w[:, o:o + MLA_ROPE]; o += MLA_ROPE
    qb = w[:, o:o + GQA_HEADS * GQA_HD]; o += GQA_HEADS * GQA_HD
    kb = w[:, o:o + GQA_KV_HEADS * GQA_HD]; o += GQA_KV_HEADS * GQA_HD
    vb = w[:, o:o + GQA_KV_HEADS * GQA_HD]; o += GQA_KV_HEADS * GQA_HD
    ga = w[:, o:o + d]; o += d
    gb = w[:, o:o + d]
    zc = lambda n: jnp.zeros((d, n), w.dtype)
    dup = lambda m: jnp.concatenate(
        [m[:, g * GQA_HD:(g + 1) * GQA_HD] for g in range(GQA_KV_HEADS) for _ in range(2)], axis=1)
    win = jnp.concatenate(
        [cq, ckv, zc(MLA_NOPE), kr, zc(LANE - MLA_NOPE - MLA_ROPE), qb, dup(kb), ga, gb], axis=1)

    dq = MLA_NOPE + MLA_ROPE
    wuq = p["mla_w_uq"].reshape(Q_LORA, MLA_HEADS, dq)
    wuq = jnp.pad(wuq, ((0, 0), (0, 0), (0, LANE - dq))).reshape(Q_LORA, _W_SLAB)
    wukv = p["mla_w_ukv"].reshape(KV_LORA, MLA_HEADS, MLA_NOPE + MLA_V)
    wk = jnp.pad(wukv[:, :, :MLA_NOPE], ((0, 0), (0, 0), (0, LANE - MLA_NOPE))).reshape(KV_LORA, _W_SLAB)
    wvt = wukv[:, :, MLA_NOPE:].reshape(KV_LORA, MLA_HEADS * MLA_V).T

    row = lambda v: v.reshape(1, -1).astype(F32)
    zl = lambda n: jnp.zeros((n,), F32)
    gqn, gkn = p["mla_gqn"], p["mla_gkn"]
    pad = LANE - dq
    gqa = jnp.tile(jnp.concatenate([gqn, zl(pad)]), MLA_HEADS)
    gka = jnp.tile(jnp.concatenate([gkn[:MLA_NOPE], zl(LANE - MLA_NOPE)]), MLA_HEADS)
    gkr = jnp.concatenate([zl(MLA_NOPE), gkn[MLA_NOPE:], zl(pad)])
    gqg = jnp.tile(p["gqa_gqn"], GQA_HEADS)
    gkg = jnp.tile(p["gqa_gkn"], 2 * GQA_KV_HEADS)
    return dict(
        gmix=row(p["g_mix"]), win=win.astype(BF16), gq=row(p["mla_gq"]), wuq=wuq.astype(BF16),
        gkv=row(p["mla_gkv"]), wuk=wk.astype(BF16), wuvt=wvt.astype(BF16), wvbt=vb.T.astype(BF16),
        gqa=row(gqa), gka=row(gka), gkr=row(gkr),
        gqg=row(gqg), gkg=row(gkg),
        bda=_block_diag([(0, MLA_NOPE), (MLA_NOPE, MLA_ROPE)]),
        bdb=_block_diag([(0, GQA_HD), (GQA_HD, GQA_HD)]))


def _mixer_in(x2d, w, S, tm):
    n, d = x2d.shape
    tm = _pick(S, tm)
    nt_s = S // tm
    taba, tabb = _rope_tables(S)
    full = lambda a: pl.BlockSpec(a.shape, lambda i: (0,) * a.ndim)
    rows = lambda wd: pl.BlockSpec((tm, wd), lambda i: (i, 0))
    tab = pl.BlockSpec((3, tm, LANE), lambda i: (0, i % nt_s, 0))
    consts = [w[k] for k in ("gmix", "win", "gq", "wuq", "gkv", "wuk", "wuvt", "wvbt", "gqa", "gka", "gkr",
                             "gqg", "gkg", "bda", "bdb")]
    cols = lambda ht: pl.BlockSpec((ht, tm), lambda i: (0, i))
    outs = ((_W_SLAB, False), (_W_SLAB, False), (MLA_HEADS * _VROWS, True), (_W_SLAB, False),
            (GQA_KV_HEADS * LANE, False), (GQA_KV_HEADS * _VROWS, True), (d, False), (d, False))
    return pl.pallas_call(
        functools.partial(_mixer_in_kernel, d_model=d),
        grid=(n // tm,),
        in_specs=[rows(d)] + [full(c) for c in consts] + [tab, tab],
        out_specs=[cols(wd) if t else rows(wd) for wd, t in outs],
        out_shape=[jax.ShapeDtypeStruct((wd, n) if t else (n, wd), BF16) for wd, t in outs],
        compiler_params=pltpu.CompilerParams(dimension_semantics=("parallel",), vmem_limit_bytes=VMEM_LIMIT),
        name="mixer_in",
    )(x2d, *consts, taba, tabb)


def _attn_kernel(q_ref, k_ref, vt_ref, o_ref, m_scr, mp_scr, acc_scr, s_scr, *, k_slab, v_row, tk, hd):
    nheads = len(k_slab)
    nk = k_ref.shape[0] // tk
    acc_scr[...] = jnp.zeros(acc_scr.shape, F32)

    def scores(h, j, slot, m_run):
        q = q_ref[:, h * LANE:(h + 1) * LANE]
        k = k_ref[pl.ds(j * tk, tk), k_slab[h] * LANE:(k_slab[h] + 1) * LANE]
        s = lax.dot_general(k, q, _NT, preferred_element_type=F32)
        s_scr[slot, h] = s
        mx = jnp.max(s, axis=0, keepdims=True)
        return mx if m_run is None else jnp.maximum(m_run, mx)

    def consume(h, j, slot):
        m_new = m_scr[h]
        alpha = jnp.exp2(mp_scr[h] - m_new)
        p = jnp.exp2(s_scr[slot, h] - m_new)
        vt = vt_ref[v_row[h] * _VROWS:(v_row[h] + 1) * _VROWS, pl.ds(j * tk, tk)]
        acc_scr[h] = alpha * acc_scr[h] + jnp.dot(vt, p.astype(BF16), preferred_element_type=F32)

    for h in range(nheads):
        m0 = scores(h, 0, 0, None)
        m_scr[h] = m0
        mp_scr[h] = m0

    for j in range(nk - 1):
        slot = j % 2
        for h in range(nheads):
            m_cur = m_scr[h]
            m_next = scores(h, j + 1, 1 - slot, m_cur)
            consume(h, j, slot)
            mp_scr[h] = m_cur
            m_scr[h] = m_next
    for h in range(nheads):
        consume(h, nk - 1, (nk - 1) % 2)
    out_t = jnp.concatenate([acc_scr[h, 0:hd, :] / acc_scr[h, hd:hd + 1, :] for h in range(nheads)], axis=0)
    o_ref[...] = out_t.T.astype(o_ref.dtype)


def _attention(q, k, vt, B, S, k_slab, v_row, tq, tk):
    n = q.shape[0]
    nheads = len(k_slab)
    hd = LANE // 2
    tq, tk = _pick(S, tq), _pick(S, tk)
    nq = S // tq
    return pl.pallas_call(
        functools.partial(_attn_kernel, k_slab=tuple(k_slab), v_row=tuple(v_row), tk=tk, hd=hd),
        grid=(B, nq),
        in_specs=[pl.BlockSpec((tq, q.shape[1]), lambda b, i: (b * nq + i, 0)),
                  pl.BlockSpec((S, k.shape[1]), lambda b, i: (b, 0)),
                  pl.BlockSpec((vt.shape[0], S), lambda b, i: (0, b))],
        out_specs=pl.BlockSpec((tq, nheads * hd), lambda b, i: (b * nq + i, 0)),
        out_shape=jax.ShapeDtypeStruct((n, nheads * hd), BF16),
        scratch_shapes=[pltpu.VMEM((nheads, 1, tq), F32), pltpu.VMEM((nheads, 1, tq), F32),
                        pltpu.VMEM((nheads, _VROWS, tq), F32), pltpu.VMEM((2, nheads, tk, tq), F32)],
        compiler_params=pltpu.CompilerParams(
            dimension_semantics=("parallel", "arbitrary"), vmem_limit_bytes=VMEM_LIMIT),
        name="attention",
    )(q, k, vt)


def _mem_kv_kernel(mem_ref, gmem_ref, wkv_ref, gkn_ref, kx_ref, vx_ref):
    m = _rms_rows(mem_ref[...], gmem_ref[...]).astype(BF16)
    kv = jnp.dot(m, wkv_ref[...], preferred_element_type=F32)
    w = XA_HEADS * XA_HD
    g = gkn_ref[...]
    parts = [_rms_rows(kv[:, h * XA_HD:(h + 1) * XA_HD], g) for h in range(XA_HEADS)]
    kx_ref[...] = jnp.concatenate(parts, axis=1).astype(BF16)
    vx_ref[...] = kv[:, w:].astype(BF16)


def _mem_kv(mem2d, gmem, wkv, gkn, tm):
    n, d = mem2d.shape
    tm = _pick(n, tm)
    w = XA_HEADS * XA_HD
    full = lambda a: pl.BlockSpec(a.shape, lambda i: (0,) * a.ndim)
    return pl.pallas_call(
        _mem_kv_kernel,
        grid=(n // tm,),
        in_specs=[pl.BlockSpec((tm, d), lambda i: (i, 0)), full(gmem), full(wkv), full(gkn)],
        out_specs=[pl.BlockSpec((tm, w), lambda i: (i, 0))] * 2,
        out_shape=[jax.ShapeDtypeStruct((n, w), BF16)] * 2,
        compiler_params=pltpu.CompilerParams(dimension_semantics=("parallel",), vmem_limit_bytes=VMEM_LIMIT),
        name="mem_kv",
    )(mem2d, gmem, wkv, gkn)


def _post_kernel(x_ref, oa_ref, ob_ref, ga_ref, gb_ref, wbra_ref, wbrb_ref, wmix_ref, gcross_ref, wq_ref,
                 gqn_ref, kx_ref, vx_ref, wo_ref, gmoe_ref, wrhi_ref, wrlo_ref, x2_ref, hm_ref, aff_ref):
    ba = jnp.dot(oa_ref[...], wbra_ref[...], preferred_element_type=F32)
    bb = jnp.dot(ob_ref[...], wbrb_ref[...], preferred_element_type=F32)
    mixed = ga_ref[...].astype(F32) * ba + gb_ref[...].astype(F32) * bb
    x1 = x_ref[...] + jnp.dot(mixed.astype(BF16), wmix_ref[...], preferred_element_type=F32)

    hc = _rms_rows(x1, gcross_ref[...]).astype(BF16)
    q = jnp.dot(hc, wq_ref[...], preferred_element_type=F32)
    gqn = gqn_ref[...]
    outs = []
    for h in range(XA_HEADS):
        sl = slice(h * XA_HD, (h + 1) * XA_HD)
        qh = (_rms_rows(q[:, sl], gqn) * (XA_HD ** -0.5)).astype(BF16)
        s = lax.dot_general(qh, kx_ref[:, sl], (((1,), (1,)), ((), ())), preferred_element_type=F32)
        p = jnp.exp(s - jnp.max(s, axis=1, keepdims=True))
        l = jnp.sum(p, axis=1, keepdims=True)
        outs.append(jnp.dot(p.astype(BF16), vx_ref[:, sl], preferred_element_type=F32) / l)
    ox = jnp.concatenate(outs, axis=1).astype(BF16)
    x2 = x1 + jnp.dot(ox, wo_ref[...], preferred_element_type=F32)
    x2_ref[...] = x2

    hm = _rms_rows(x2, gmoe_ref[...])
    hi = hm.astype(BF16)
    lo = (hm - hi.astype(F32)).astype(BF16)
    hm_ref[...] = hi
    dn = (((1,), (1,)), ((), ()))
    wrhi = wrhi_ref[...]
    logits = (lax.dot_general(wrhi, hi, dn, preferred_element_type=F32)
              + lax.dot_general(wrhi, lo, dn, preferred_element_type=F32)
              + lax.dot_general(wrlo_ref[...], hi, dn, preferred_element_type=F32))
    e = jnp.exp(logits - jnp.max(logits, axis=0, keepdims=True))
    aff_ref[...] = e / jnp.sum(e, axis=0, keepdims=True)


def _post(x2d, oa, ob, ga, gb, kx, vx, w, S, n_mem, tm):
    n, d = x2d.shape
    tm = _pick(S, tm)
    nt_s = S // tm
    ne = w["wrhi"].shape[0]
    full = lambda a: pl.BlockSpec(a.shape, lambda i: (0,) * a.ndim)
    rows = lambda wd: pl.BlockSpec((tm, wd), lambda i: (i, 0))
    memblk = pl.BlockSpec((n_mem, kx.shape[1]), lambda i: (i // nt_s, 0))
    c = [w[k] for k in ("wbra", "wbrb", "wmix", "gcross", "wq", "gqn")]
    c2 = [w[k] for k in ("wo", "gmoe", "wrhi", "wrlo")]
    return pl.pallas_call(
        _post_kernel,
        grid=(n // tm,),
        in_specs=[rows(d), rows(oa.shape[1]), rows(ob.shape[1]), rows(d), rows(d)] + [full(a) for a in c]
                 + [memblk, memblk] + [full(a) for a in c2],
        out_specs=[rows(d), rows(d), pl.BlockSpec((ne, tm), lambda i: (0, i))],
        out_shape=[jax.ShapeDtypeStruct((n, d), F32), jax.ShapeDtypeStruct((n, d), BF16),
                   jax.ShapeDtypeStruct((ne, n), F32)],
        compiler_params=pltpu.CompilerParams(dimension_semantics=("parallel",), vmem_limit_bytes=VMEM_LIMIT),
        name="post_mix",
    )(x2d, oa, ob, ga, gb, *c, kx, vx, *c2)


_SEL_CHUNK = 2048
_TIE_CHUNK = 512


def _topc_kernel(aff_ref, tri_ref, wsel_ref, pos_ref, *, cap):
    ne, n = aff_ref.shape
    chunk = min(_SEL_CHUNK, n)
    nchunks = n // chunk

    def bits_at(c, width):
        return pltpu.bitcast(aff_ref[:, pl.ds(pl.multiple_of(c * width, width), width)], jnp.int32)

    def count(pred_fn):
        def body(c, acc):
            return acc + pred_fn(bits_at(c, chunk)).astype(jnp.int32)
        acc = lax.fori_loop(0, nchunks, body, jnp.zeros((ne, chunk), jnp.int32))
        return jnp.sum(acc, axis=1, keepdims=True)

    def bit_step(i, t):
        cand = t | jnp.left_shift(jnp.int32(1), 30 - i)
        return jnp.where(count(lambda b: b >= cand) >= cap, cand, t)

    thr = lax.fori_loop(0, 31, bit_step, jnp.zeros((ne, 1), jnp.int32))
    need = (cap - count(lambda b: b > thr)).astype(F32)

    tchunk = min(_TIE_CHUNK, n)
    tri = tri_ref[...]

    def tie_step(c, carry):
        run_eq, run_sel = carry
        sl = pl.ds(pl.multiple_of(c * tchunk, tchunk), tchunk)
        a = aff_ref[:, sl]
        b = pltpu.bitcast(a, jnp.int32)
        eq = b == thr
        eqf = jnp.where(eq, 1.0, 0.0)
        before = run_eq + jnp.dot(eqf.astype(BF16), tri, preferred_element_type=F32)
        sel = (b > thr) | (eq & (before < need))
        self_ = jnp.where(sel, 1.0, 0.0)
        rank = run_sel + jnp.dot(self_.astype(BF16), tri, preferred_element_type=F32)
        wsel_ref[:, sl] = jnp.where(sel, a, 0.0)
        pos_ref[:, sl] = jnp.where(sel, rank, -1.0).astype(jnp.int32)
        return (run_eq + jnp.sum(eqf, axis=1, keepdims=True), run_sel + jnp.sum(self_, axis=1, keepdims=True))

    zero = jnp.zeros((ne, 1), F32)
    lax.fori_loop(0, n // tchunk, tie_step, (zero, zero))


def _topc(aff_t, cap):
    ne, n = aff_t.shape
    tchunk = min(_TIE_CHUNK, n)
    idx = jnp.arange(tchunk)
    tri = (idx[:, None] < idx[None, :]).astype(BF16)
    return pl.pallas_call(
        functools.partial(_topc_kernel, cap=cap),
        out_shape=[jax.ShapeDtypeStruct((ne, n), F32), jax.ShapeDtypeStruct((ne, n), jnp.int32)],
        compiler_params=pltpu.CompilerParams(vmem_limit_bytes=VMEM_LIMIT),
        name="expert_choice_select",
    )(aff_t, tri)


_MOE_SUB = 256
_MOE_TILE = 512
_MOE_GROUP = 4
_GRANULE = 16
_MOE_SHORT_WIN = 64


def _moe_ffn_kernel(off_ref, hm_ref, pos_ref, w_ref, wg_ref, wu_ref, wd_ref, *rest, nsub, tr, nc1, eb):
    o_refs, (stage, vstage) = rest[:eb], rest[eb:]
    grp = pl.program_id(0)
    c = pl.program_id(1)
    sub = hm_ref.shape[0] // nsub
    win = sub + _GRANULE

    @pl.when(c == 0)
    def _():
        stage[...] = jnp.zeros(stage.shape, stage.dtype)
        vstage[...] = jnp.zeros(vstage.shape, vstage.dtype)

    bases = [(grp * eb + j) * nc1 + c * nsub for j in range(eb)]
    flushed = [(off_ref[b] // tr) * tr for b in bases]

    def append(i, k16s, window):
        cols = slice(i * sub, (i + 1) * sub)
        riota = lax.broadcasted_iota(jnp.int32, (window, sub), 0)
        hits = [(pos_ref[j, :, cols] - (flushed[j] + k16s[j])) == riota for j in range(eb)]
        onehot = jnp.concatenate([jnp.where(h, 1.0, 0.0).astype(BF16) for h in hits], axis=0)
        x = jnp.dot(onehot, hm_ref[cols, :], preferred_element_type=F32)
        for j in range(eb):
            rows = pl.ds(k16s[j], window)
            xj = x[j * window:(j + 1) * window, :].astype(BF16)
            stage[j, rows, :] = stage[j, rows, :] + xj
            v = jnp.sum(jnp.where(hits[j], w_ref[j, :, cols], 0.0), axis=1, keepdims=True)
            vstage[j, rows, :] = vstage[j, rows, :] + v

    for i in range(nsub):
        k16s, widest = [], None
        for j in range(eb):
            k = off_ref[bases[j] + i] - flushed[j]
            k16 = pl.multiple_of((k // _GRANULE) * _GRANULE, _GRANULE)
            span = off_ref[bases[j] + i + 1] - flushed[j] - k16
            k16s.append(k16)
            widest = span if widest is None else jnp.maximum(widest, span)
        lax.cond(widest <= _MOE_SHORT_WIN,
                 functools.partial(append, i, k16s, _MOE_SHORT_WIN),
                 functools.partial(append, i, k16s, win))

    for j in range(eb):
        @pl.when(off_ref[bases[j] + nsub] - flushed[j] >= tr)
        def _(j=j):
            xs = stage[j, 0:tr, :]
            g = jnp.dot(xs, wg_ref[j], preferred_element_type=F32)
            u = jnp.dot(xs, wu_ref[j], preferred_element_type=F32)
            hid = (g * _sigmoid(g) * u).astype(BF16)
            out = jnp.dot(hid, wd_ref[j], preferred_element_type=F32)
            o_refs[j][...] = (out * vstage[j, 0:tr, :]).astype(o_refs[j].dtype)
            keep = stage.shape[1] - tr
            stage[j, 0:keep, :] = stage[j, tr:, :]
            stage[j, keep:, :] = jnp.zeros((tr, stage.shape[2]), stage.dtype)
            vstage[j, 0:keep, :] = vstage[j, tr:, :]
            vstage[j, keep:, :] = jnp.zeros((tr, 1), vstage.dtype)


def _moe_ffn(off, hm, pos3, wsel3, wg, wu, wd, cap, tr, nc1, eb):
    n, d = hm.shape
    ne, _, ff = wg.shape
    ts = tr
    nsub = ts // _MOE_SUB
    tiles = cap // tr
    once = pl.Buffered(1)

    def out_spec(j):
        return pl.BlockSpec((tr, d), lambda g, c, off: (
            g * tiles + jnp.minimum(off[(g * eb + j) * nc1 + c * nsub] // tr, tiles - 1), 0))

    grid_spec = pltpu.PrefetchScalarGridSpec(
        num_scalar_prefetch=1,
        grid=(ne // eb, n // ts),
        in_specs=[pl.BlockSpec((ts, d), lambda g, c, off: (c, 0)),
                  pl.BlockSpec((eb, 1, ts), lambda g, c, off: (g, 0, c)),
                  pl.BlockSpec((eb, 1, ts), lambda g, c, off: (g, 0, c)),
                  pl.BlockSpec((eb, d, ff), lambda g, c, off: (g, 0, 0), pipeline_mode=once),
                  pl.BlockSpec((eb, d, ff), lambda g, c, off: (g, 0, 0), pipeline_mode=once),
                  pl.BlockSpec((eb, ff, d), lambda g, c, off: (g, 0, 0), pipeline_mode=once)],
        out_specs=[out_spec(j) for j in range(eb)],
        scratch_shapes=[pltpu.VMEM((eb, tr + ts + _GRANULE, d), BF16),
                        pltpu.VMEM((eb, tr + ts + _GRANULE, 1), F32)])
    return pl.pallas_call(
        functools.partial(_moe_ffn_kernel, nsub=nsub, tr=tr, nc1=nc1, eb=eb),
        grid_spec=grid_spec,
        out_shape=[jax.ShapeDtypeStruct((ne // eb * cap, d), BF16)] * eb,
        compiler_params=pltpu.CompilerParams(
            dimension_semantics=("arbitrary", "arbitrary"), vmem_limit_bytes=VMEM_LIMIT),
        name="moe_ffn",
    )(off, hm, pos3, wsel3, wg, wu, wd)


def _moe_combine_kernel(off_ref, pos_ref, x2_ref, *rest, ne, nc1, cap, eb):
    rows_hbm, (y_ref, gbuf, st_scr, sems) = rest[:eb], rest[eb:]
    c = pl.program_id(0)
    nc = pl.num_programs(0)
    sub = x2_ref.shape[0]
    gran = _GRANULE
    slot = c % 2

    def granule_copy(e, src_row, dst_row, sl):
        return pltpu.make_async_copy(rows_hbm[e % eb].at[pl.ds(src_row, gran), :],
                                     gbuf.at[sl, pl.ds(dst_row, gran), :], sems.at[sl])

    def layout(cc):
        gb = jnp.int32(0)
        shifts, starts, firsts, counts = [], [gb], [], []
        for e in range(ne):
            a = off_ref[e * nc1 + cc]
            b = off_ref[e * nc1 + cc + 1]
            a16 = (a // gran) * gran
            ng = jnp.where(b > a, (b - a16 + gran - 1) // gran, 0)
            shifts.append(gb - a16)
            firsts.append((e // eb) * cap + a16)
            counts.append(ng)
            gb = gb + ng * gran
            starts.append(gb)
        return shifts, starts, firsts, counts

    def fetch(cc, sl):
        _, starts, firsts, counts = layout(cc)
        for e in range(ne):
            def issue(g, carry, e=e):
                granule_copy(e, pl.multiple_of(firsts[e] + g * gran, gran),
                             pl.multiple_of(starts[e] + g * gran, gran), sl).start()
                return carry
            lax.fori_loop(0, counts[e], issue, 0)

    @pl.when(c == 0)
    def _():
        gbuf[...] = jnp.zeros(gbuf.shape, gbuf.dtype)
        fetch(c, slot)

    @pl.when(c + 1 < nc)
    def _():
        fetch(c + 1, 1 - slot)

    shifts, starts, _, _ = layout(c)
    gb = starts[ne]

    def wait_one(g, carry):
        granule_copy(0, 0, 0, slot).wait()
        return carry

    lax.fori_loop(0, gb // gran, wait_one, 0)

    y_ref[...] = x2_ref[...]
    blk = MXU

    def kblock(kb, carry):
        r0 = pl.multiple_of(kb * blk, blk)
        riota = lax.broadcasted_iota(jnp.int32, (blk, sub), 0) + r0
        st_scr[...] = jnp.zeros(st_scr.shape, F32)
        for e in range(ne):
            @pl.when((starts[e] < r0 + blk) & (starts[e + 1] > r0))
            def _(e=e):
                p = pos_ref[e]
                st_scr[...] = jnp.where(((p + shifts[e]) == riota) & (p >= 0), 1.0, st_scr[...])
        y_ref[...] += jnp.dot(st_scr[...].T.astype(BF16), gbuf[slot, pl.ds(r0, blk), :],
                              preferred_element_type=F32)
        return carry

    lax.fori_loop(0, (gb + blk - 1) // blk, kblock, 0)


def _moe_combine(off, pos3, x2, rows, cap, nc1):
    eb = len(rows)
    n, d = x2.shape
    ne = pos3.shape[0]
    sub = _MOE_SUB
    gmax = ne * (sub + 2 * _GRANULE)
    gmax = (gmax + MXU - 1) // MXU * MXU
    grid_spec = pltpu.PrefetchScalarGridSpec(
        num_scalar_prefetch=1,
        grid=(n // sub,),
        in_specs=[pl.BlockSpec((ne, 1, sub), lambda c, off: (0, 0, c)),
                  pl.BlockSpec((sub, d), lambda c, off: (c, 0))]
                 + [pl.BlockSpec(memory_space=pl.ANY)] * eb,
        out_specs=pl.BlockSpec((sub, d), lambda c, off: (c, 0)),
        scratch_shapes=[pltpu.VMEM((2, gmax, d), BF16), pltpu.VMEM((MXU, sub), F32),
                        pltpu.SemaphoreType.DMA((2,))])
    return pl.pallas_call(
        functools.partial(_moe_combine_kernel, ne=ne, nc1=nc1, cap=cap, eb=eb),
        grid_spec=grid_spec,
        out_shape=jax.ShapeDtypeStruct((n, d), F32),
        compiler_params=pltpu.CompilerParams(dimension_semantics=("arbitrary",), vmem_limit_bytes=VMEM_LIMIT),
        name="moe_combine",
    )(off, pos3, x2, *rows)


def _moe(hm, x2, aff_t, wg, wu, wd):
    n, d = hm.shape
    ne = aff_t.shape[0]
    cap = EC_FACTOR * n // ne
    tr = min(_MOE_TILE, cap)
    assert cap % tr == 0 and n % tr == 0 and tr % _MOE_SUB == 0, (n, cap, tr)
    wsel, pos = _topc(aff_t, cap)
    nc = n // _MOE_SUB
    cnt = jnp.sum((pos >= 0).reshape(ne, nc, _MOE_SUB), axis=-1, dtype=jnp.int32)
    off = jnp.concatenate([jnp.zeros((ne, 1), jnp.int32), jnp.cumsum(cnt, axis=1, dtype=jnp.int32)], axis=1)
    off = off.reshape(-1)
    pos3 = pos.reshape(ne, 1, n)
    rows = _moe_ffn(off, hm, pos3, wsel.reshape(ne, 1, n), wg, wu, wd, cap, tr, nc + 1, _MOE_GROUP)
    return _moe_combine(off, pos3, x2, rows, cap, nc + 1)


def _prep_post_weights(p):
    row = lambda v: v.reshape(1, -1).astype(F32)
    wr = p["w_router"].T.astype(F32)
    wrhi = wr.astype(BF16)
    wrlo = (wr - wrhi.astype(F32)).astype(BF16)
    return dict(
        wbra=p["mla_w_br"].astype(BF16), wbrb=p["gqa_w_br"].astype(BF16), wmix=p["w_mix_out"].astype(BF16),
        gcross=row(p["g_cross"]), wq=p["xa_w_q"].astype(BF16), gqn=row(p["xa_gqn"]),
        wo=p["xa_w_o"].astype(BF16), gmoe=row(p["g_moe"]), wrhi=wrhi, wrlo=wrlo,
        gmem=row(p["g_mem"]), wkv=p["xa_w_kv"].astype(BF16), gkn=row(p["xa_gkn"]),
        wg=p["moe_w_gate"].astype(BF16), wu=p["moe_w_up"].astype(BF16), wd=p["moe_w_down"].astype(BF16))


def _encoder_layer(x, mem, wm, wp):
    B, S, D = x.shape
    n = B * S
    n_mem = mem.shape[1]
    x2d = x.reshape(n, D)
    qa, ka, va, qg, kg, vg, ga, gb = _mixer_in(x2d, wm, S, tm=512)
    oa = _attention(qa, ka, va, B, S, k_slab=range(MLA_HEADS), v_row=range(MLA_HEADS), tq=256, tk=512)
    rep = GQA_HEADS // GQA_KV_HEADS
    ob = _attention(qg, kg, vg, B, S, k_slab=[h // rep for h in range(GQA_HEADS)],
                    v_row=[h // rep for h in range(GQA_HEADS)], tq=256, tk=512)
    kx, vx = _mem_kv(mem.reshape(B * n_mem, D), wp["gmem"], wp["wkv"], wp["gkn"], tm=256)
    x2, hm, aff_t = _post(x2d, oa, ob, ga, gb, kx, vx, wp, S, n_mem, tm=512)
    out = _moe(hm, x2, aff_t, wp["wg"], wp["wu"], wp["wd"])
    return out.reshape(B, S, D)


def kernel(x_prompt, x_sample, mem_prompt, mem_sample, g_mix, w_in, mla_gq, mla_w_uq, mla_gkv, mla_w_ukv, mla_gqn, mla_gkn, mla_w_br, gqa_gqn, gqa_gkn, gqa_w_br, w_mix_out, g_cross, g_mem, xa_w_q, xa_w_kv, xa_gqn, xa_gkn, xa_w_o, g_moe, w_router, moe_w_gate, moe_w_up, moe_w_down):
    names = ("g_mix", "w_in", "mla_gq", "mla_w_uq", "mla_gkv", "mla_w_ukv", "mla_gqn", "mla_gkn", "mla_w_br",
             "gqa_gqn", "gqa_gkn", "gqa_w_br", "w_mix_out", "g_cross", "g_mem", "xa_w_q", "xa_w_kv", "xa_gqn",
             "xa_gkn", "xa_w_o", "g_moe", "w_router", "moe_w_gate", "moe_w_up", "moe_w_down")
    vals = (g_mix, w_in, mla_gq, mla_w_uq, mla_gkv, mla_w_ukv, mla_gqn, mla_gkn, mla_w_br,
            gqa_gqn, gqa_gkn, gqa_w_br, w_mix_out, g_cross, g_mem, xa_w_q, xa_w_kv, xa_gqn,
            xa_gkn, xa_w_o, g_moe, w_router, moe_w_gate, moe_w_up, moe_w_down)
    y_prompt, y_sample = x_prompt, x_sample
    depth = w_in.shape[0]
    for l in range(depth):
        p = {k: v[l] for k, v in zip(names, vals)}
        wm = _prep_mixer_weights(p)
        wp = _prep_post_weights(p)
        y_prompt = _encoder_layer(y_prompt, mem_prompt, wm, wp)
        y_sample = _encoder_layer(y_sample, mem_sample, wm, wp)
    return (y_prompt, y_sample)
```

```python
import functools
import math

import jax
import jax.numpy as jnp
from jax import lax
from jax.experimental import pallas as pl
from jax.experimental.pallas import tpu as pltpu

EPS = 1e-6
GRID_W = 64
ROPE_THETA = 10000.0
MLA_HEADS = 8
Q_LORA = 384
KV_LORA = 256
MLA_NOPE = 64
MLA_ROPE = 32
MLA_V = 64
GQA_HEADS = 8
GQA_KV_HEADS = 2
GQA_HD = 64
XA_HEADS = 4
XA_HD = 128
N_EXPERTS = 16
EC_FACTOR = 2

LANE = 128
MXU = 256
VMEM_LIMIT = 56 * 1024 * 1024
LOG2E = math.log2(math.e)

F32 = jnp.float32
BF16 = jnp.bfloat16


def _pick(n, pref):
    t = min(n, pref)
    while n % t:
        t -= LANE
    return t


def _rms_rows(x, g):
    ms = jnp.mean(x * x, axis=-1, keepdims=True)
    return x * lax.rsqrt(ms + EPS) * g


def _group_mean_sq(x, bd):
    x2 = x * x
    hi = x2.astype(BF16)
    lo = (x2 - hi.astype(F32)).astype(BF16)
    w = x.shape[1]
    outs = []
    for c in range(0, w, MXU):
        cw = min(MXU, w - c)
        b = bd[:cw, :cw]
        outs.append(jnp.dot(hi[:, c:c + cw], b, preferred_element_type=F32)
                    + jnp.dot(lo[:, c:c + cw], b, preferred_element_type=F32))
    return outs[0] if len(outs) == 1 else jnp.concatenate(outs, axis=1)


def _group_rms(x, g, bd):
    return x * lax.rsqrt(_group_mean_sq(x, bd) + EPS) * g


def _rope_slabs(x, tab, shift):
    c, s1, s2 = tab[0], tab[1], tab[2]
    outs = []
    for a in range(0, x.shape[1], LANE):
        xs = x[:, a:a + LANE]
        outs.append(xs * c + pltpu.roll(xs, LANE - shift, 1) * s1 + pltpu.roll(xs, shift, 1) * s2)
    return outs[0] if len(outs) == 1 else jnp.concatenate(outs, axis=1)


def _sigmoid(x):
    return 1.0 / (1.0 + jnp.exp(-x))


_SEG_CQ = (0, Q_LORA)
_SEG_CKV = (_SEG_CQ[1], _SEG_CQ[1] + KV_LORA)
_SEG_KR = (_SEG_CKV[1], _SEG_CKV[1] + LANE)
_SEG_QB = (_SEG_KR[1], _SEG_KR[1] + GQA_HEADS * GQA_HD)
_SEG_KB = (_SEG_QB[1], _SEG_QB[1] + GQA_KV_HEADS * LANE)
_W_SLAB = MLA_HEADS * LANE

_NT = (((1,), (1,)), ((), ()))

_VROWS = MLA_V + 16


def _with_count_rows(vt, heads):
    tm = vt.shape[1]
    ones = jnp.where(lax.broadcasted_iota(jnp.int32, (16, tm), 0) == 0, 1.0, 0.0)
    parts = []
    for h in range(heads):
        parts += [vt[h * MLA_V:(h + 1) * MLA_V, :], ones]
    return jnp.concatenate(parts, axis=0).astype(BF16)


def _mixer_in_kernel(x_ref, gmix_ref, win_ref, gq_ref, wuq_ref, gkv_ref, wuk_ref, wuvt_ref, wvbt_ref,
                     gqa_ref, gka_ref, gkr_ref, gqg_ref, gkg_ref, bda_ref, bdb_ref, taba_ref, tabb_ref,
                     qa_ref, ka_ref, vat_ref, qg_ref, kg_ref, vgt_ref, ga_ref, gb_ref, *, d_model):
    h = _rms_rows(x_ref[...], gmix_ref[...]).astype(BF16)

    def proj(seg):
        return jnp.dot(h, win_ref[:, seg[0]:seg[1]], preferred_element_type=F32)

    bda = bda_ref[...]
    bdb = bdb_ref[...]
    taba = taba_ref[...]
    tabb = tabb_ref[...]

    cq = _rms_rows(proj(_SEG_CQ), gq_ref[...]).astype(BF16)
    qa = jnp.dot(cq, wuq_ref[...], preferred_element_type=F32)
    qa = _rope_slabs(_group_rms(qa, gqa_ref[...], bda), taba, MLA_ROPE // 2)
    qa_ref[...] = (qa * ((MLA_NOPE + MLA_ROPE) ** -0.5 * LOG2E)).astype(BF16)

    ckv = _rms_rows(proj(_SEG_CKV), gkv_ref[...]).astype(BF16)
    kn = _group_rms(jnp.dot(ckv, wuk_ref[...], preferred_element_type=F32), gka_ref[...], bdb)
    kr = _rope_slabs(_group_rms(proj(_SEG_KR), gkr_ref[...], bda), taba, MLA_ROPE // 2)
    ka_ref[...] = (kn + jnp.concatenate([kr] * MLA_HEADS, axis=1)).astype(BF16)
    vat_ref[...] = _with_count_rows(lax.dot_general(wuvt_ref[...], ckv, _NT, preferred_element_type=F32), MLA_HEADS)

    qg = _rope_slabs(_group_rms(proj(_SEG_QB), gqg_ref[...], bdb), tabb, GQA_HD // 2) * (GQA_HD ** -0.5 * LOG2E)
    lane = lax.broadcasted_iota(jnp.int32, (1, LANE), 1)
    low = lane < GQA_HD
    parts = []
    for p in range(GQA_HEADS // 2):
        s = qg[:, p * LANE:(p + 1) * LANE]
        parts.append(jnp.where(low, s, 0.0))
        parts.append(jnp.where(low, 0.0, s))
    qg_ref[...] = jnp.concatenate(parts, axis=1).astype(BF16)
    kg = _rope_slabs(_group_rms(proj(_SEG_KB), gkg_ref[...], bdb), tabb, GQA_HD // 2)
    kg_ref[...] = kg.astype(BF16)
    vgt_ref[...] = _with_count_rows(lax.dot_general(wvbt_ref[...], h, _NT, preferred_element_type=F32), GQA_KV_HEADS)

    g0 = _SEG_KB[1]
    ga_ref[...] = _sigmoid(proj((g0, g0 + d_model))).astype(BF16)
    gb_ref[...] = _sigmoid(proj((g0 + d_model, g0 + 2 * d_model))).astype(BF16)


def _rope_tables(S):
    rows = S // GRID_W
    row = jnp.repeat(jnp.arange(rows, dtype=F32), GRID_W)
    col = jnp.tile(jnp.arange(GRID_W, dtype=F32), rows)

    def cs(rot_dim):
        n_ax = rot_dim // 4
        freqs = 1.0 / (ROPE_THETA ** (jnp.arange(n_ax, dtype=F32) / n_ax))
        ang = jnp.concatenate([row[:, None] * freqs, col[:, None] * freqs], axis=-1)
        return jnp.cos(ang), jnp.sin(ang)

    z = lambda w: jnp.zeros((S, w), F32)
    o = lambda w: jnp.ones((S, w), F32)
    ca, sa = cs(MLA_ROPE)
    pad = LANE - MLA_NOPE - MLA_ROPE
    taba = jnp.stack([
        jnp.concatenate([o(MLA_NOPE), ca, ca, z(pad)], axis=1),
        jnp.concatenate([z(MLA_NOPE), -sa, z(MLA_ROPE // 2), z(pad)], axis=1),
        jnp.concatenate([z(MLA_NOPE), z(MLA_ROPE // 2), sa, z(pad)], axis=1)])
    cb, sb = cs(GQA_HD)
    hz = z(GQA_HD // 2)
    tabb = jnp.stack([
        jnp.concatenate([cb, cb, cb, cb], axis=1),
        jnp.concatenate([-sb, hz, -sb, hz], axis=1),
        jnp.concatenate([hz, sb, hz, sb], axis=1)])
    return taba, tabb


def _block_diag(groups):
    idx = jnp.arange(MXU)
    m = jnp.zeros((MXU, MXU), F32)
    for base in range(0, MXU, LANE):
        for start, size in groups:
            inside = (idx >= base + start) & (idx < base + start + size)
            m = m + jnp.where(inside[:, None] & inside[None, :], 1.0 / size, 0.0)
    return m.astype(BF16)


def _prep_mixer_weights(p):
    d = p["w_in"].shape[0]
    w = p["w_in"]
    o = 0
    cq = w[:, o:o + Q_LORA]; o += Q_LORA
    ckv = w[:, o:o + KV_LORA]; o += KV_LORA
    kr = w[:, o:o + MLA_ROPE]; o += MLA_ROPE
    qb = w[:, o:o + GQA_HEADS * GQA_HD]; o += GQA_HEADS * GQA_HD
    kb = w[:, o:o + GQA_KV_HEADS * GQA_HD]; o += GQA_KV_HEADS * GQA_HD
    vb = w[:, o:o + GQA_KV_HEADS * GQA_HD]; o += GQA_KV_HEADS * GQA_HD
    ga = w[:, o:o + d]; o += d
    gb = w[:, o:o + d]
    zc = lambda n: jnp.zeros((d, n), w.dtype)
    dup = lambda m: jnp.concatenate(
        [m[:, g * GQA_HD:(g + 1) * GQA_HD] for g in range(GQA_KV_HEADS) for _ in range(2)], axis=1)
    win = jnp.concatenate(
        [cq, ckv, zc(MLA_NOPE), kr, zc(LANE - MLA_NOPE - MLA_ROPE), qb, dup(kb), ga, gb], axis=1)

    dq = MLA_NOPE + MLA_ROPE
    wuq = p["mla_w_uq"].reshape(Q_LORA, MLA_HEADS, dq)
    wuq = jnp.pad(wuq, ((0, 0), (0, 0), (0, LANE - dq))).reshape(Q_LORA, _W_SLAB)
    wukv = p["mla_w_ukv"].reshape(KV_LORA, MLA_HEADS, MLA_NOPE + MLA_V)
    wk = jnp.pad(wukv[:, :, :MLA_NOPE], ((0, 0), (0, 0), (0, LANE - MLA_NOPE))).reshape(KV_LORA, _W_SLAB)
    wvt = wukv[:, :, MLA_NOPE:].reshape(KV_LORA, MLA_HEADS * MLA_V).T

    row = lambda v: v.reshape(1, -1).astype(F32)
    zl = lambda n: jnp.zeros((n,), F32)
    gqn, gkn = p["mla_gqn"], p["mla_gkn"]
    pad = LANE - dq
    gqa = jnp.tile(jnp.concatenate([gqn, zl(pad)]), MLA_HEADS)
    gka = jnp.tile(jnp.concatenate([gkn[:MLA_NOPE], zl(LANE - MLA_NOPE)]), MLA_HEADS)
    gkr = jnp.concatenate([zl(MLA_NOPE), gkn[MLA_NOPE:], zl(pad)])
    gqg = jnp.tile(p["gqa_gqn"], GQA_HEADS)
    gkg = jnp.tile(p["gqa_gkn"], 2 * GQA_KV_HEADS)
    return dict(
        gmix=row(p["g_mix"]), win=win.astype(BF16), gq=row(p["mla_gq"]), wuq=wuq.astype(BF16),
        gkv=row(p["mla_gkv"]), wuk=wk.astype(BF16), wuvt=wvt.astype(BF16), wvbt=vb.T.astype(BF16),
        gqa=row(gqa), gka=row(gka), gkr=row(gkr),
        gqg=row(gqg), gkg=row(gkg),
        bda=_block_diag([(0, MLA_NOPE), (MLA_NOPE, MLA_ROPE)]),
        bdb=_block_diag([(0, GQA_HD), (GQA_HD, GQA_HD)]))


def _mixer_in(x2d, w, S, tm):
    n, d = x2d.shape
    tm = _pick(S, tm)
    nt_s = S // tm
    taba, tabb = _rope_tables(S)
    full = lambda a: pl.BlockSpec(a.shape, lambda i: (0,) * a.ndim)
    rows = lambda wd: pl.BlockSpec((tm, wd), lambda i: (i, 0))
    tab = pl.BlockSpec((3, tm, LANE), lambda i: (0, i % nt_s, 0))
    consts = [w[k] for k in ("gmix", "win", "gq", "wuq", "gkv", "wuk", "wuvt", "wvbt", "gqa", "gka", "gkr",
                             "gqg", "gkg", "bda", "bdb")]
    cols = lambda ht: pl.BlockSpec((ht, tm), lambda i: (0, i))
    outs = ((_W_SLAB, False), (_W_SLAB, False), (MLA_HEADS * _VROWS, True), (_W_SLAB, False),
            (GQA_KV_HEADS * LANE, False), (GQA_KV_HEADS * _VROWS, True), (d, False), (d, False))
    return pl.pallas_call(
        functools.partial(_mixer_in_kernel, d_model=d),
        grid=(n // tm,),
        in_specs=[rows(d)] + [full(c) for c in consts] + [tab, tab],
        out_specs=[cols(wd) if t else rows(wd) for wd, t in outs],
        out_shape=[jax.ShapeDtypeStruct((wd, n) if t else (n, wd), BF16) for wd, t in outs],
        compiler_params=pltpu.CompilerParams(dimension_semantics=("parallel",), vmem_limit_bytes=VMEM_LIMIT),
        name="mixer_in",
    )(x2d, *consts, taba, tabb)


def _attn_kernel(q_ref, k_ref, vt_ref, o_ref, m_scr, mp_scr, acc_scr, s_scr, *, k_slab, v_row, tk, hd):
    nheads = len(k_slab)
    nk = k_ref.shape[0] // tk
    acc_scr[...] = jnp.zeros(acc_scr.shape, F32)

    def scores(h, j, slot, m_run):
        q = q_ref[:, h * LANE:(h + 1) * LANE]
        k = k_ref[pl.ds(j * tk, tk), k_slab[h] * LANE:(k_slab[h] + 1) * LANE]
        s = lax.dot_general(k, q, _NT, preferred_element_type=F32)
        s_scr[slot, h] = s
        mx = jnp.max(s, axis=0, keepdims=True)
        return mx if m_run is None else jnp.maximum(m_run, mx)

    def consume(h, j, slot):
        m_new = m_scr[h]
        alpha = jnp.exp2(mp_scr[h] - m_new)
        p = jnp.exp2(s_scr[slot, h] - m_new)
        vt = vt_ref[v_row[h] * _VROWS:(v_row[h] + 1) * _VROWS, pl.ds(j * tk, tk)]
        acc_scr[h] = alpha * acc_scr[h] + jnp.dot(vt, p.astype(BF16), preferred_element_type=F32)

    for h in range(nheads):
        m0 = scores(h, 0, 0, None)
        m_scr[h] = m0
        mp_scr[h] = m0

    for j in range(nk - 1):
        slot = j % 2
        for h in range(nheads):
            m_cur = m_scr[h]
            m_next = scores(h, j + 1, 1 - slot, m_cur)
            consume(h, j, slot)
            mp_scr[h] = m_cur
            m_scr[h] = m_next
    for h in range(nheads):
        consume(h, nk - 1, (nk - 1) % 2)
    out_t = jnp.concatenate([acc_scr[h, 0:hd, :] / acc_scr[h, hd:hd + 1, :] for h in range(nheads)], axis=0)
    o_ref[...] = out_t.T.astype(o_ref.dtype)


def _attention(q, k, vt, B, S, k_slab, v_row, tq, tk):
    n = q.shape[0]
    nheads = len(k_slab)
    hd = LANE // 2
    tq, tk = _pick(S, tq), _pick(S, tk)
    nq = S // tq
    return pl.pallas_call(
        functools.partial(_attn_kernel, k_slab=tuple(k_slab), v_row=tuple(v_row), tk=tk, hd=hd),
        grid=(B, nq),
        in_specs=[pl.BlockSpec((tq, q.shape[1]), lambda b, i: (b * nq + i, 0)),
                  pl.BlockSpec((S, k.shape[1]), lambda b, i: (b, 0)),
                  pl.BlockSpec((vt.shape[0], S), lambda b, i: (0, b))],
        out_specs=pl.BlockSpec((tq, nheads * hd), lambda b, i: (b * nq + i, 0)),
        out_shape=jax.ShapeDtypeStruct((n, nheads * hd), BF16),
        scratch_shapes=[pltpu.VMEM((nheads, 1, tq), F32), pltpu.VMEM((nheads, 1, tq), F32),
                        pltpu.VMEM((nheads, _VROWS, tq), F32), pltpu.VMEM((2, nheads, tk, tq), F32)],
        compiler_params=pltpu.CompilerParams(
            dimension_semantics=("parallel", "arbitrary"), vmem_limit_bytes=VMEM_LIMIT),
        name="attention",
    )(q, k, vt)


def _mem_kv_kernel(mem_ref, gmem_ref, wkv_ref, gkn_ref, kx_ref, vx_ref):
    m = _rms_rows(mem_ref[...], gmem_ref[...]).astype(BF16)
    kv = jnp.dot(m, wkv_ref[...], preferred_element_type=F32)
    w = XA_HEADS * XA_HD
    g = gkn_ref[...]
    parts = [_rms_rows(kv[:, h * XA_HD:(h + 1) * XA_HD], g) for h in range(XA_HEADS)]
    kx_ref[...] = jnp.concatenate(parts, axis=1).astype(BF16)
    vx_ref[...] = kv[:, w:].astype(BF16)


def _mem_kv(mem2d, gmem, wkv, gkn, tm):
    n, d = mem2d.shape
    tm = _pick(n, tm)
    w = XA_HEADS * XA_HD
    full = lambda a: pl.BlockSpec(a.shape, lambda i: (0,) * a.ndim)
    return pl.pallas_call(
        _mem_kv_kernel,
        grid=(n // tm,),
        in_specs=[pl.BlockSpec((tm, d), lambda i: (i, 0)), full(gmem), full(wkv), full(gkn)],
        out_specs=[pl.BlockSpec((tm, w), lambda i: (i, 0))] * 2,
        out_shape=[jax.ShapeDtypeStruct((n, w), BF16)] * 2,
        compiler_params=pltpu.CompilerParams(dimension_semantics=("parallel",), vmem_limit_bytes=VMEM_LIMIT),
        name="mem_kv",
    )(mem2d, gmem, wkv, gkn)


def _post_kernel(x_ref, oa_ref, ob_ref, ga_ref, gb_ref, wbra_ref, wbrb_ref, wmix_ref, gcross_ref, wq_ref,
                 gqn_ref, kx_ref, vx_ref, wo_ref, gmoe_ref, wrhi_ref, wrlo_ref, x2_ref, hm_ref, aff_ref):
    ba = jnp.dot(oa_ref[...], wbra_ref[...], preferred_element_type=F32)
    bb = jnp.dot(ob_ref[...], wbrb_ref[...], preferred_element_type=F32)
    mixed = ga_ref[...].astype(F32) * ba + gb_ref[...].astype(F32) * bb
    x1 = x_ref[...] + jnp.dot(mixed.astype(BF16), wmix_ref[...], preferred_element_type=F32)

    hc = _rms_rows(x1, gcross_ref[...]).astype(BF16)
    q = jnp.dot(hc, wq_ref[...], preferred_element_type=F32)
    gqn = gqn_ref[...]
    outs = []
    for h in range(XA_HEADS):
        sl = slice(h * XA_HD, (h + 1) * XA_HD)
        qh = (_rms_rows(q[:, sl], gqn) * (XA_HD ** -0.5)).astype(BF16)
        s = lax.dot_general(qh, kx_ref[:, sl], (((1,), (1,)), ((), ())), preferred_element_type=F32)
        p = jnp.exp(s - jnp.max(s, axis=1, keepdims=True))
        l = jnp.sum(p, axis=1, keepdims=True)
        outs.append(jnp.dot(p.astype(BF16), vx_ref[:, sl], preferred_element_type=F32) / l)
    ox = jnp.concatenate(outs, axis=1).astype(BF16)
    x2 = x1 + jnp.dot(ox, wo_ref[...], preferred_element_type=F32)
    x2_ref[...] = x2

    hm = _rms_rows(x2, gmoe_ref[...])
    hi = hm.astype(BF16)
    lo = (hm - hi.astype(F32)).astype(BF16)
    hm_ref[...] = hi
    dn = (((1,), (1,)), ((), ()))
    wrhi = wrhi_ref[...]
    logits = (lax.dot_general(wrhi, hi, dn, preferred_element_type=F32)
              + lax.dot_general(wrhi, lo, dn, preferred_element_type=F32)
              + lax.dot_general(wrlo_ref[...], hi, dn, preferred_element_type=F32))
    e = jnp.exp(logits - jnp.max(logits, axis=0, keepdims=True))
    aff_ref[...] = e / jnp.sum(e, axis=0, keepdims=True)


def _post(x2d, oa, ob, ga, gb, kx, vx, w, S, n_mem, tm):
    n, d = x2d.shape
    tm = _pick(S, tm)
    nt_s = S // tm
    ne = w["wrhi"].shape[0]
    full = lambda a: pl.BlockSpec(a.shape, lambda i: (0,) * a.ndim)
    rows = lambda wd: pl.BlockSpec((tm, wd), lambda i: (i, 0))
    memblk = pl.BlockSpec((n_mem, kx.shape[1]), lambda i: (i // nt_s, 0))
    c = [w[k] for k in ("wbra", "wbrb", "wmix", "gcross", "wq", "gqn")]
    c2 = [w[k] for k in ("wo", "gmoe", "wrhi", "wrlo")]
    return pl.pallas_call(
        _post_kernel,
        grid=(n // tm,),
        in_specs=[rows(d), rows(oa.shape[1]), rows(ob.shape[1]), rows(d), rows(d)] + [full(a) for a in c]
                 + [memblk, memblk] + [full(a) for a in c2],
        out_specs=[rows(d), rows(d), pl.BlockSpec((ne, tm), lambda i: (0, i))],
        out_shape=[jax.ShapeDtypeStruct((n, d), F32), jax.ShapeDtypeStruct((n, d), BF16),
                   jax.ShapeDtypeStruct((ne, n), F32)],
        compiler_params=pltpu.CompilerParams(dimension_semantics=("parallel",), vmem_limit_bytes=VMEM_LIMIT),
        name="post_mix",
    )(x2d, oa, ob, ga, gb, *c, kx, vx, *c2)


_SEL_CHUNK = 2048
_TIE_CHUNK = 512


def _topc_kernel(aff_ref, tri_ref, wsel_ref, pos_ref, *, cap):
    ne, n = aff_ref.shape
    chunk = min(_SEL_CHUNK, n)
    nchunks = n // chunk

    def bits_at(c, width):
        return pltpu.bitcast(aff_ref[:, pl.ds(pl.multiple_of(c * width, width), width)], jnp.int32)

    def count(pred_fn):
        def body(c, acc):
            return acc + pred_fn(bits_at(c, chunk)).astype(jnp.int32)
        acc = lax.fori_loop(0, nchunks, body, jnp.zeros((ne, chunk), jnp.int32))
        return jnp.sum(acc, axis=1, keepdims=True)

    def bit_step(i, t):
        cand = t | jnp.left_shift(jnp.int32(1), 30 - i)
        return jnp.where(count(lambda b: b >= cand) >= cap, cand, t)

    thr = lax.fori_loop(0, 31, bit_step, jnp.zeros((ne, 1), jnp.int32))
    need = (cap - count(lambda b: b > thr)).astype(F32)

    tchunk = min(_TIE_CHUNK, n)
    tri = tri_ref[...]

    def tie_step(c, carry):
        run_eq, run_sel = carry
        sl = pl.ds(pl.multiple_of(c * tchunk, tchunk), tchunk)
        a = aff_ref[:, sl]
        b = pltpu.bitcast(a, jnp.int32)
        eq = b == thr
        eqf = jnp.where(eq, 1.0, 0.0)
        before = run_eq + jnp.dot(eqf.astype(BF16), tri, preferred_element_type=F32)
        sel = (b > thr) | (eq & (before < need))
        self_ = jnp.where(sel, 1.0, 0.0)
        rank = run_sel + jnp.dot(self_.astype(BF16), tri, preferred_element_type=F32)
        wsel_ref[:, sl] = jnp.where(sel, a, 0.0)
        pos_ref[:, sl] = jnp.where(sel, rank, -1.0).astype(jnp.int32)
        return (run_eq + jnp.sum(eqf, axis=1, keepdims=True), run_sel + jnp.sum(self_, axis=1, keepdims=True))

    zero = jnp.zeros((ne, 1), F32)
    lax.fori_loop(0, n // tchunk, tie_step, (zero, zero))


def _topc(aff_t, cap):
    ne, n = aff_t.shape
    tchunk = min(_TIE_CHUNK, n)
    idx = jnp.arange(tchunk)
    tri = (idx[:, None] < idx[None, :]).astype(BF16)
    return pl.pallas_call(
        functools.partial(_topc_kernel, cap=cap),
        out_shape=[jax.ShapeDtypeStruct((ne, n), F32), jax.ShapeDtypeStruct((ne, n), jnp.int32)],
        compiler_params=pltpu.CompilerParams(vmem_limit_bytes=VMEM_LIMIT),
        name="expert_choice_select",
    )(aff_t, tri)


_MOE_SUB = 256
_MOE_TILE = 512
_MOE_GROUP = 4
_GRANULE = 16
_MOE_SHORT_WIN = 64


def _moe_ffn_kernel(off_ref, hm_ref, pos_ref, w_ref, wg_ref, wu_ref, wd_ref, *rest, nsub, tr, nc1, eb):
    o_refs, (stage, vstage) = rest[:eb], rest[eb:]
    grp = pl.program_id(0)
    c = pl.program_id(1)
    sub = hm_ref.shape[0] // nsub
    win = sub + _GRANULE

    @pl.when(c == 0)
    def _():
        stage[...] = jnp.zeros(stage.shape, stage.dtype)
        vstage[...] = jnp.zeros(vstage.shape, vstage.dtype)

    bases = [(grp * eb + j) * nc1 + c * nsub for j in range(eb)]
    flushed = [(off_ref[b] // tr) * tr for b in bases]

    def append(i, k16s, window):
        cols = slice(i * sub, (i + 1) * sub)
        riota = lax.broadcasted_iota(jnp.int32, (window, sub), 0)
        hits = [(pos_ref[j, :, cols] - (flushed[j] + k16s[j])) == riota for j in range(eb)]
        onehot = jnp.concatenate([jnp.where(h, 1.0, 0.0).astype(BF16) for h in hits], axis=0)
        x = jnp.dot(onehot, hm_ref[cols, :], preferred_element_type=F32)
        for j in range(eb):
            rows = pl.ds(k16s[j], window)
            xj = x[j * window:(j + 1) * window, :].astype(BF16)
            stage[j, rows, :] = stage[j, rows, :] + xj
            v = jnp.sum(jnp.where(hits[j], w_ref[j, :, cols], 0.0), axis=1, keepdims=True)
            vstage[j, rows, :] = vstage[j, rows, :] + v

    for i in range(nsub):
        k16s, widest = [], None
        for j in range(eb):
            k = off_ref[bases[j] + i] - flushed[j]
            k16 = pl.multiple_of((k // _GRANULE) * _GRANULE, _GRANULE)
            span = off_ref[bases[j] + i + 1] - flushed[j] - k16
            k16s.append(k16)
            widest = span if widest is None else jnp.maximum(widest, span)
        lax.cond(widest <= _MOE_SHORT_WIN,
                 functools.partial(append, i, k16s, _MOE_SHORT_WIN),
                 functools.partial(append, i, k16s, win))

    for j in range(eb):
        @pl.when(off_ref[bases[j] + nsub] - flushed[j] >= tr)
        def _(j=j):
            xs = stage[j, 0:tr, :]
            g = jnp.dot(xs, wg_ref[j], preferred_element_type=F32)
            u = jnp.dot(xs, wu_ref[j], preferred_element_type=F32)
            hid = (g * _sigmoid(g) * u).astype(BF16)
            out = jnp.dot(hid, wd_ref[j], preferred_element_type=F32)
            o_refs[j][...] = (out * vstage[j, 0:tr, :]).astype(o_refs[j].dtype)
            keep = stage.shape[1] - tr
            stage[j, 0:keep, :] = stage[j, tr:, :]
            stage[j, keep:, :] = jnp.zeros((tr, stage.shape[2]), stage.dtype)
            vstage[j, 0:keep, :] = vstage[j, tr:, :]
            vstage[j, keep:, :] = jnp.zeros((tr, 1), vstage.dtype)


def _moe_ffn(off, hm, pos3, wsel3, wg, wu, wd, cap, tr, nc1, eb):
    n, d = hm.shape
    ne, _, ff = wg.shape
    ts = tr
    nsub = ts // _MOE_SUB
    tiles = cap // tr
    once = pl.Buffered(1)

    def out_spec(j):
        return pl.BlockSpec((tr, d), lambda g, c, off: (
            g * tiles + jnp.minimum(off[(g * eb + j) * nc1 + c * nsub] // tr, tiles - 1), 0))

    grid_spec = pltpu.PrefetchScalarGridSpec(
        num_scalar_prefetch=1,
        grid=(ne // eb, n // ts),
        in_specs=[pl.BlockSpec((ts, d), lambda g, c, off: (c, 0)),
                  pl.BlockSpec((eb, 1, ts), lambda g, c, off: (g, 0, c)),
                  pl.BlockSpec((eb, 1, ts), lambda g, c, off: (g, 0, c)),
                  pl.BlockSpec((eb, d, ff), lambda g, c, off: (g, 0, 0), pipeline_mode=once),
                  pl.BlockSpec((eb, d, ff), lambda g, c, off: (g, 0, 0), pipeline_mode=once),
                  pl.BlockSpec((eb, ff, d), lambda g, c, off: (g, 0, 0), pipeline_mode=once)],
        out_specs=[out_spec(j) for j in range(eb)],
        scratch_shapes=[pltpu.VMEM((eb, tr + ts + _GRANULE, d), BF16),
                        pltpu.VMEM((eb, tr + ts + _GRANULE, 1), F32)])
    return pl.pallas_call(
        functools.partial(_moe_ffn_kernel, nsub=nsub, tr=tr, nc1=nc1, eb=eb),
        grid_spec=grid_spec,
        out_shape=[jax.ShapeDtypeStruct((ne // eb * cap, d), BF16)] * eb,
        compiler_params=pltpu.CompilerParams(
            dimension_semantics=("arbitrary", "arbitrary"), vmem_limit_bytes=VMEM_LIMIT),
        name="moe_ffn",
    )(off, hm, pos3, wsel3, wg, wu, wd)


def _moe_combine_kernel(off_ref, pos_ref, x2_ref, *rest, ne, nc1, cap, eb):
    rows_hbm, (y_ref, gbuf, st_scr, sems) = rest[:eb], rest[eb:]
    c = pl.program_id(0)
    nc = pl.num_programs(0)
    sub = x2_ref.shape[0]
    gran = _GRANULE
    slot = c % 2

    def granule_copy(e, src_row, dst_row, sl):
        return pltpu.make_async_copy(rows_hbm[e % eb].at[pl.ds(src_row, gran), :],
                                     gbuf.at[sl, pl.ds(dst_row, gran), :], sems.at[sl])

    def layout(cc):
        gb = jnp.int32(0)
        shifts, starts, firsts, counts = [], [gb], [], []
        for e in range(ne):
            a = off_ref[e * nc1 + cc]
            b = off_ref[e * nc1 + cc + 1]
            a16 = (a // gran) * gran
            ng = jnp.where(b > a, (b - a16 + gran - 1) // gran, 0)
            shifts.append(gb - a16)
            firsts.append((e // eb) * cap + a16)
            counts.append(ng)
            gb = gb + ng * gran
            starts.append(gb)
        return shifts, starts, firsts, counts

    def fetch(cc, sl):
        _, starts, firsts, counts = layout(cc)
        for e in range(ne):
            def issue(g, carry, e=e):
                granule_copy(e, pl.multiple_of(firsts[e] + g * gran, gran),
                             pl.multiple_of(starts[e] + g * gran, gran), sl).start()
                return carry
            lax.fori_loop(0, counts[e], issue, 0)

    @pl.when(c == 0)
    def _():
        gbuf[...] = jnp.zeros(gbuf.shape, gbuf.dtype)
        fetch(c, slot)

    @pl.when(c + 1 < nc)
    def _():
        fetch(c + 1, 1 - slot)

    shifts, starts, _, _ = layout(c)
    gb = starts[ne]

    def wait_one(g, carry):
        granule_copy(0, 0, 0, slot).wait()
        return carry

    lax.fori_loop(0, gb // gran, wait_one, 0)

    y_ref[...] = x2_ref[...]
    blk = MXU

    def kblock(kb, carry):
        r0 = pl.multiple_of(kb * blk, blk)
        riota = lax.broadcasted_iota(jnp.int32, (blk, sub), 0) + r0
        st_scr[...] = jnp.zeros(st_scr.shape, F32)
        for e in range(ne):
            @pl.when((starts[e] < r0 + blk) & (starts[e + 1] > r0))
            def _(e=e):
                p = pos_ref[e]
                st_scr[...] = jnp.where(((p + shifts[e]) == riota) & (p >= 0), 1.0, st_scr[...])
        y_ref[...] += jnp.dot(st_scr[...].T.astype(BF16), gbuf[slot, pl.ds(r0, blk), :],
                              preferred_element_type=F32)
        return carry

    lax.fori_loop(0, (gb + blk - 1) // blk, kblock, 0)


def _moe_combine(off, pos3, x2, rows, cap, nc1):
    eb = len(rows)
    n, d = x2.shape
    ne = pos3.shape[0]
    sub = _MOE_SUB
    gmax = ne * (sub + 2 * _GRANULE)
    gmax = (gmax + MXU - 1) // MXU * MXU
    grid_spec = pltpu.PrefetchScalarGridSpec(
        num_scalar_prefetch=1,
        grid=(n // sub,),
        in_specs=[pl.BlockSpec((ne, 1, sub), lambda c, off: (0, 0, c)),
                  pl.BlockSpec((sub, d), lambda c, off: (c, 0))]
                 + [pl.BlockSpec(memory_space=pl.ANY)] * eb,
        out_specs=pl.BlockSpec((sub, d), lambda c, off: (c, 0)),
        scratch_shapes=[pltpu.VMEM((2, gmax, d), BF16), pltpu.VMEM((MXU, sub), F32),
                        pltpu.SemaphoreType.DMA((2,))])
    return pl.pallas_call(
        functools.partial(_moe_combine_kernel, ne=ne, nc1=nc1, cap=cap, eb=eb),
        grid_spec=grid_spec,
        out_shape=jax.ShapeDtypeStruct((n, d), F32),
        compiler_params=pltpu.CompilerParams(dimension_semantics=("arbitrary",), vmem_limit_bytes=VMEM_LIMIT),
        name="moe_combine",
    )(off, pos3, x2, *rows)


def _moe(hm, x2, aff_t, wg, wu, wd):
    n, d = hm.shape
    ne = aff_t.shape[0]
    cap = EC_FACTOR * n // ne
    tr = min(_MOE_TILE, cap)
    assert cap % tr == 0 and n % tr == 0 and tr % _MOE_SUB == 0, (n, cap, tr)
    wsel, pos = _topc(aff_t, cap)
    nc = n // _MOE_SUB
    cnt = jnp.sum((pos >= 0).reshape(ne, nc, _MOE_SUB), axis=-1, dtype=jnp.int32)
    off = jnp.concatenate([jnp.zeros((ne, 1), jnp.int32), jnp.cumsum(cnt, axis=1, dtype=jnp.int32)], axis=1)
    off = off.reshape(-1)
    pos3 = pos.reshape(ne, 1, n)
    rows = _moe_ffn(off, hm, pos3, wsel.reshape(ne, 1, n), wg, wu, wd, cap, tr, nc + 1, _MOE_GROUP)
    return _moe_combine(off, pos3, x2, rows, cap, nc + 1)


def _prep_post_weights(p):
    row = lambda v: v.reshape(1, -1).astype(F32)
    wr = p["w_router"].T.astype(F32)
    wrhi = wr.astype(BF16)
    wrlo = (wr - wrhi.astype(F32)).astype(BF16)
    return dict(
        wbra=p["mla_w_br"].astype(BF16), wbrb=p["gqa_w_br"].astype(BF16), wmix=p["w_mix_out"].astype(BF16),
        gcross=row(p["g_cross"]), wq=p["xa_w_q"].astype(BF16), gqn=row(p["xa_gqn"]),
        wo=p["xa_w_o"].astype(BF16), gmoe=row(p["g_moe"]), wrhi=wrhi, wrlo=wrlo,
        gmem=row(p["g_mem"]), wkv=p["xa_w_kv"].astype(BF16), gkn=row(p["xa_gkn"]),
        wg=p["moe_w_gate"].astype(BF16), wu=p["moe_w_up"].astype(BF16), wd=p["moe_w_down"].astype(BF16))


def _encoder_layer(x, mem, wm, wp):
    B, S, D = x.shape
    n = B * S
    n_mem = mem.shape[1]
    x2d = x.reshape(n, D)
    qa, ka, va, qg, kg, vg, ga, gb = _mixer_in(x2d, wm, S, tm=512)
    oa = _attention(qa, ka, va, B, S, k_slab=range(MLA_HEADS), v_row=range(MLA_HEADS), tq=256, tk=256)
    rep = GQA_HEADS // GQA_KV_HEADS
    ob = _attention(qg, kg, vg, B, S, k_slab=[h // rep for h in range(GQA_HEADS)],
                    v_row=[h // rep for h in range(GQA_HEADS)], tq=256, tk=256)
    kx, vx = _mem_kv(mem.reshape(B * n_mem, D), wp["gmem"], wp["wkv"], wp["gkn"], tm=256)
    x2, hm, aff_t = _post(x2d, oa, ob, ga, gb, kx, vx, wp, S, n_mem, tm=512)
    out = _moe(hm, x2, aff_t, wp["wg"], wp["wu"], wp["wd"])
    return out.reshape(B, S, D)


def kernel(x_prompt, x_sample, mem_prompt, mem_sample, g_mix, w_in, mla_gq, mla_w_uq, mla_gkv, mla_w_ukv, mla_gqn, mla_gkn, mla_w_br, gqa_gqn, gqa_gkn, gqa_w_br, w_mix_out, g_cross, g_mem, xa_w_q, xa_w_kv, xa_gqn, xa_gkn, xa_w_o, g_moe, w_router, moe_w_gate, moe_w_up, moe_w_down):
    names = ("g_mix", "w_in", "mla_gq", "mla_w_uq", "mla_gkv", "mla_w_ukv", "mla_gqn", "mla_gkn", "mla_w_br",
             "gqa_gqn", "gqa_gkn", "gqa_w_br", "w_mix_out", "g_cross", "g_mem", "xa_w_q", "xa_w_kv", "xa_gqn",
             "xa_gkn", "xa_w_o", "g_moe", "w_router", "moe_w_gate", "moe_w_up", "moe_w_down")
    vals = (g_mix, w_in, mla_gq, mla_w_uq, mla_gkv, mla_w_ukv, mla_gqn, mla_gkn, mla_w_br,
            gqa_gqn, gqa_gkn, gqa_w_br, w_mix_out, g_cross, g_mem, xa_w_q, xa_w_kv, xa_gqn,
            xa_gkn, xa_w_o, g_moe, w_router, moe_w_gate, moe_w_up, moe_w_down)
    y_prompt, y_sample = x_prompt, x_sample
    depth = w_in.shape[0]
    for l in range(depth):
        p = {k: v[l] for k, v in zip(names, vals)}
        wm = _prep_mixer_weights(p)
        wp = _prep_post_weights(p)
        y_prompt = _encoder_layer(y_prompt, mem_prompt, wm, wp)
        y_sample = _encoder_layer(y_sample, mem_sample, wm, wp)
    return (y_prompt, y_sample)
```

```python
import functools
import math

import jax
import jax.numpy as jnp
from jax import lax
from jax.experimental import pallas as pl
from jax.experimental.pallas import tpu as pltpu

EPS = 1e-6
GRID_W = 64
ROPE_THETA = 10000.0
MLA_HEADS = 8
Q_LORA = 384
KV_LORA = 256
MLA_NOPE = 64
MLA_ROPE = 32
MLA_V = 64
GQA_HEADS = 8
GQA_KV_HEADS = 2
GQA_HD = 64
XA_HEADS = 4
XA_HD = 128
N_EXPERTS = 16
EC_FACTOR = 2

LANE = 128
MXU = 256
VMEM_LIMIT = 56 * 1024 * 1024
LOG2E = math.log2(math.e)

F32 = jnp.float32
BF16 = jnp.bfloat16


def _pick(n, pref):
    t = min(n, pref)
    while n % t:
        t -= LANE
    return t


def _rms_rows(x, g):
    ms = jnp.mean(x * x, axis=-1, keepdims=True)
    return x * lax.rsqrt(ms + EPS) * g


def _group_mean_sq(x, bd):
    x2 = x * x
    hi = x2.astype(BF16)
    lo = (x2 - hi.astype(F32)).astype(BF16)
    w = x.shape[1]
    outs = []
    for c in range(0, w, MXU):
        cw = min(MXU, w - c)
        b = bd[:cw, :cw]
        outs.append(jnp.dot(hi[:, c:c + cw], b, preferred_element_type=F32)
                    + jnp.dot(lo[:, c:c + cw], b, preferred_element_type=F32))
    return outs[0] if len(outs) == 1 else jnp.concatenate(outs, axis=1)


def _group_rms(x, g, bd):
    return x * lax.rsqrt(_group_mean_sq(x, bd) + EPS) * g


def _rope_slabs(x, tab, shift):
    c, s1, s2 = tab[0], tab[1], tab[2]
    outs = []
    for a in range(0, x.shape[1], LANE):
        xs = x[:, a:a + LANE]
        outs.append(xs * c + pltpu.roll(xs, LANE - shift, 1) * s1 + pltpu.roll(xs, shift, 1) * s2)
    return outs[0] if len(outs) == 1 else jnp.concatenate(outs, axis=1)


def _sigmoid(x):
    return 1.0 / (1.0 + jnp.exp(-x))


_SEG_CQ = (0, Q_LORA)
_SEG_CKV = (_SEG_CQ[1], _SEG_CQ[1] + KV_LORA)
_SEG_KR = (_SEG_CKV[1], _SEG_CKV[1] + LANE)
_SEG_QB = (_SEG_KR[1], _SEG_KR[1] + GQA_HEADS * GQA_HD)
_SEG_KB = (_SEG_QB[1], _SEG_QB[1] + GQA_KV_HEADS * LANE)
_W_SLAB = MLA_HEADS * LANE

_NT = (((1,), (1,)), ((), ()))

_VROWS = MLA_V + 16


def _with_count_rows(vt, heads):
    tm = vt.shape[1]
    ones = jnp.where(lax.broadcasted_iota(jnp.int32, (16, tm), 0) == 0, 1.0, 0.0)
    parts = []
    for h in range(heads):
        parts += [vt[h * MLA_V:(h + 1) * MLA_V, :], ones]
    return jnp.concatenate(parts, axis=0).astype(BF16)


def _mixer_in_kernel(x_ref, gmix_ref, win_ref, gq_ref, wuq_ref, gkv_ref, wuk_ref, wuvt_ref, wvbt_ref,
                     gqa_ref, gka_ref, gkr_ref, gqg_ref, gkg_ref, bda_ref, bdb_ref, taba_ref, tabb_ref,
                     qa_ref, ka_ref, vat_ref, qg_ref, kg_ref, vgt_ref, ga_ref, gb_ref, *, d_model):
    h = _rms_rows(x_ref[...], gmix_ref[...]).astype(BF16)

    def proj(seg):
        return jnp.dot(h, win_ref[:, seg[0]:seg[1]], preferred_element_type=F32)

    bda = bda_ref[...]
    bdb = bdb_ref[...]
    taba = taba_ref[...]
    tabb = tabb_ref[...]

    g0 = _SEG_KB[1]
    z_cq = proj(_SEG_CQ)
    z_ckv = proj(_SEG_CKV)
    cq = _rms_rows(z_cq, gq_ref[...]).astype(BF16)
    z_kr = proj(_SEG_KR)
    qa = jnp.dot(cq, wuq_ref[...], preferred_element_type=F32)
    ckv = _rms_rows(z_ckv, gkv_ref[...]).astype(BF16)
    z_qb = proj(_SEG_QB)
    kn_raw = jnp.dot(ckv, wuk_ref[...], preferred_element_type=F32)

    qa = _rope_slabs(_group_rms(qa, gqa_ref[...], bda), taba, MLA_ROPE // 2)
    qa_ref[...] = (qa * ((MLA_NOPE + MLA_ROPE) ** -0.5 * LOG2E)).astype(BF16)
    z_kb = proj(_SEG_KB)
    vat = lax.dot_general(wuvt_ref[...], ckv, _NT, preferred_element_type=F32)

    kn = _group_rms(kn_raw, gka_ref[...], bdb)
    kr = _rope_slabs(_group_rms(z_kr, gkr_ref[...], bda), taba, MLA_ROPE // 2)
    ka_ref[...] = (kn + jnp.concatenate([kr] * MLA_HEADS, axis=1)).astype(BF16)
    z_ga = proj((g0, g0 + d_model))
    vat_ref[...] = _with_count_rows(vat, MLA_HEADS)

    qg = _rope_slabs(_group_rms(z_qb, gqg_ref[...], bdb), tabb, GQA_HD // 2) * (GQA_HD ** -0.5 * LOG2E)
    lane = lax.broadcasted_iota(jnp.int32, (1, LANE), 1)
    low = lane < GQA_HD
    parts = []
    for p in range(GQA_HEADS // 2):
        s = qg[:, p * LANE:(p + 1) * LANE]
        parts.append(jnp.where(low, s, 0.0))
        parts.append(jnp.where(low, 0.0, s))
    qg_ref[...] = jnp.concatenate(parts, axis=1).astype(BF16)
    z_gb = proj((g0 + d_model, g0 + 2 * d_model))
    kg = _rope_slabs(_group_rms(z_kb, gkg_ref[...], bdb), tabb, GQA_HD // 2)
    kg_ref[...] = kg.astype(BF16)
    vgt = lax.dot_general(wvbt_ref[...], h, _NT, preferred_element_type=F32)

    ga_ref[...] = _sigmoid(z_ga).astype(BF16)
    vgt_ref[...] = _with_count_rows(vgt, GQA_KV_HEADS)
    gb_ref[...] = _sigmoid(z_gb).astype(BF16)


def _rope_tables(S):
    rows = S // GRID_W
    row = jnp.repeat(jnp.arange(rows, dtype=F32), GRID_W)
    col = jnp.tile(jnp.arange(GRID_W, dtype=F32), rows)

    def cs(rot_dim):
        n_ax = rot_dim // 4
        freqs = 1.0 / (ROPE_THETA ** (jnp.arange(n_ax, dtype=F32) / n_ax))
        ang = jnp.concatenate([row[:, None] * freqs, col[:, None] * freqs], axis=-1)
        return jnp.cos(ang), jnp.sin(ang)

    z = lambda w: jnp.zeros((S, w), F32)
    o = lambda w: jnp.ones((S, w), F32)
    ca, sa = cs(MLA_ROPE)
    pad = LANE - MLA_NOPE - MLA_ROPE
    taba = jnp.stack([
        jnp.concatenate([o(MLA_NOPE), ca, ca, z(pad)], axis=1),
        jnp.concatenate([z(MLA_NOPE), -sa, z(MLA_ROPE // 2), z(pad)], axis=1),
        jnp.concatenate([z(MLA_NOPE), z(MLA_ROPE // 2), sa, z(pad)], axis=1)])
    cb, sb = cs(GQA_HD)
    hz = z(GQA_HD // 2)
    tabb = jnp.stack([
        jnp.concatenate([cb, cb, cb, cb], axis=1),
        jnp.concatenate([-sb, hz, -sb, hz], axis=1),
        jnp.concatenate([hz, sb, hz, sb], axis=1)])
    return taba, tabb


def _block_diag(groups):
    idx = jnp.arange(MXU)
    m = jnp.zeros((MXU, MXU), F32)
    for base in range(0, MXU, LANE):
        for start, size in groups:
            inside = (idx >= base + start) & (idx < base + start + size)
            m = m + jnp.where(inside[:, None] & inside[None, :], 1.0 / size, 0.0)
    return m.astype(BF16)


def _prep_mixer_weights(p):
    d = p["w_in"].shape[0]
    w = p["w_in"]
    o = 0
    cq = w[:, o:o + Q_LORA]; o += Q_LORA
    ckv = w[:, o:o + KV_LORA]; o += KV_LORA
    kr = w[:, o:o + MLA_ROPE]; o += MLA_ROPE
    qb = w[:, o:o + GQA_HEADS * GQA_HD]; o += GQA_HEADS * GQA_HD
    kb = w[:, o:o + GQA_KV_HEADS * GQA_HD]; o += GQA_KV_HEADS * GQA_HD
    vb = w[:, o:o + GQA_KV_HEADS * GQA_HD]; o += GQA_KV_HEADS * GQA_HD
    ga = w[:, o:o + d]; o += d
    gb = w[:, o:o + d]
    zc = lambda n: jnp.zeros((d, n), w.dtype)
    dup = lambda m: jnp.concatenate(
        [m[:, g * GQA_HD:(g + 1) * GQA_HD] for g in range(GQA_KV_HEADS) for _ in range(2)], axis=1)
    win = jnp.concatenate(
        [cq, ckv, zc(MLA_NOPE), kr, zc(LANE - MLA_NOPE - MLA_ROPE), qb, dup(kb), ga, gb], axis=1)

    dq = MLA_NOPE + MLA_ROPE
    wuq = p["mla_w_uq"].reshape(Q_LORA, MLA_HEADS, dq)
    wuq = jnp.pad(wuq, ((0, 0), (0, 0), (0, LANE - dq))).reshape(Q_LORA, _W_SLAB)
    wukv = p["mla_w_ukv"].reshape(KV_LORA, MLA_HEADS, MLA_NOPE + MLA_V)
    wk = jnp.pad(wukv[:, :, :MLA_NOPE], ((0, 0), (0, 0), (0, LANE - MLA_NOPE))).reshape(KV_LORA, _W_SLAB)
    wvt = wukv[:, :, MLA_NOPE:].reshape(KV_LORA, MLA_HEADS * MLA_V).T

    row = lambda v: v.reshape(1, -1).astype(F32)
    zl = lambda n: jnp.zeros((n,), F32)
    gqn, gkn = p["mla_gqn"], p["mla_gkn"]
    pad = LANE - dq
    gqa = jnp.tile(jnp.concatenate([gqn, zl(pad)]), MLA_HEADS)
    gka = jnp.tile(jnp.concatenate([gkn[:MLA_NOPE], zl(LANE - MLA_NOPE)]), MLA_HEADS)
    gkr = jnp.concatenate([zl(MLA_NOPE), gkn[MLA_NOPE:], zl(pad)])
    gqg = jnp.tile(p["gqa_gqn"], GQA_HEADS)
    gkg = jnp.tile(p["gqa_gkn"], 2 * GQA_KV_HEADS)
    return dict(
        gmix=row(p["g_mix"]), win=win.astype(BF16), gq=row(p["mla_gq"]), wuq=wuq.astype(BF16),
        gkv=row(p["mla_gkv"]), wuk=wk.astype(BF16), wuvt=wvt.astype(BF16), wvbt=vb.T.astype(BF16),
        gqa=row(gqa), gka=row(gka), gkr=row(gkr),
        gqg=row(gqg), gkg=row(gkg),
        bda=_block_diag([(0, MLA_NOPE), (MLA_NOPE, MLA_ROPE)]),
        bdb=_block_diag([(0, GQA_HD), (GQA_HD, GQA_HD)]))


def _mixer_in(x2d, w, S, tm):
    n, d = x2d.shape
    tm = _pick(S, tm)
    nt_s = S // tm
    taba, tabb = _rope_tables(S)
    full = lambda a: pl.BlockSpec(a.shape, lambda i: (0,) * a.ndim)
    rows = lambda wd: pl.BlockSpec((tm, wd), lambda i: (i, 0))
    tab = pl.BlockSpec((3, tm, LANE), lambda i: (0, i % nt_s, 0))
    consts = [w[k] for k in ("gmix", "win", "gq", "wuq", "gkv", "wuk", "wuvt", "wvbt", "gqa", "gka", "gkr",
                             "gqg", "gkg", "bda", "bdb")]
    cols = lambda ht: pl.BlockSpec((ht, tm), lambda i: (0, i))
    outs = ((_W_SLAB, False), (_W_SLAB, False), (MLA_HEADS * _VROWS, True), (_W_SLAB, False),
            (GQA_KV_HEADS * LANE, False), (GQA_KV_HEADS * _VROWS, True), (d, False), (d, False))
    return pl.pallas_call(
        functools.partial(_mixer_in_kernel, d_model=d),
        grid=(n // tm,),
        in_specs=[rows(d)] + [full(c) for c in consts] + [tab, tab],
        out_specs=[cols(wd) if t else rows(wd) for wd, t in outs],
        out_shape=[jax.ShapeDtypeStruct((wd, n) if t else (n, wd), BF16) for wd, t in outs],
        compiler_params=pltpu.CompilerParams(dimension_semantics=("parallel",), vmem_limit_bytes=VMEM_LIMIT),
        name="mixer_in",
    )(x2d, *consts, taba, tabb)


def _attn_kernel(q_ref, k_ref, vt_ref, o_ref, m_scr, mp_scr, acc_scr, s_scr, *, k_slab, v_row, tk, hd):
    nheads = len(k_slab)
    nk = k_ref.shape[0] // tk
    acc_scr[...] = jnp.zeros(acc_scr.shape, F32)

    def scores(h, j, slot, m_run):
        q = q_ref[:, h * LANE:(h + 1) * LANE]
        k = k_ref[pl.ds(j * tk, tk), k_slab[h] * LANE:(k_slab[h] + 1) * LANE]
        s = lax.dot_general(k, q, _NT, preferred_element_type=F32)
        s_scr[slot, h] = s
        mx = jnp.max(s, axis=0, keepdims=True)
        return mx if m_run is None else jnp.maximum(m_run, mx)

    def consume(h, j, slot):
        m_new = m_scr[h]
        alpha = jnp.exp2(mp_scr[h] - m_new)
        p = jnp.exp2(s_scr[slot, h] - m_new)
        vt = vt_ref[v_row[h] * _VROWS:(v_row[h] + 1) * _VROWS, pl.ds(j * tk, tk)]
        acc_scr[h] = alpha * acc_scr[h] + jnp.dot(vt, p.astype(BF16), preferred_element_type=F32)

    for h in range(nheads):
        m0 = scores(h, 0, 0, None)
        m_scr[h] = m0
        mp_scr[h] = m0

    for j in range(nk - 1):
        slot = j % 2
        for h in range(nheads):
            m_cur = m_scr[h]
            m_next = scores(h, j + 1, 1 - slot, m_cur)
            consume(h, j, slot)
            mp_scr[h] = m_cur
            m_scr[h] = m_next
    for h in range(nheads):
        consume(h, nk - 1, (nk - 1) % 2)
    out_t = jnp.concatenate([acc_scr[h, 0:hd, :] / acc_scr[h, hd:hd + 1, :] for h in range(nheads)], axis=0)
    o_ref[...] = out_t.T.astype(o_ref.dtype)


def _attention(q, k, vt, B, S, k_slab, v_row, tq, tk):
    n = q.shape[0]
    nheads = len(k_slab)
    hd = LANE // 2
    tq, tk = _pick(S, tq), _pick(S, tk)
    nq = S // tq
    return pl.pallas_call(
        functools.partial(_attn_kernel, k_slab=tuple(k_slab), v_row=tuple(v_row), tk=tk, hd=hd),
        grid=(B, nq),
        in_specs=[pl.BlockSpec((tq, q.shape[1]), lambda b, i: (b * nq + i, 0)),
                  pl.BlockSpec((S, k.shape[1]), lambda b, i: (b, 0)),
                  pl.BlockSpec((vt.shape[0], S), lambda b, i: (0, b))],
        out_specs=pl.BlockSpec((tq, nheads * hd), lambda b, i: (b * nq + i, 0)),
        out_shape=jax.ShapeDtypeStruct((n, nheads * hd), BF16),
        scratch_shapes=[pltpu.VMEM((nheads, 1, tq), F32), pltpu.VMEM((nheads, 1, tq), F32),
                        pltpu.VMEM((nheads, _VROWS, tq), F32), pltpu.VMEM((2, nheads, tk, tq), F32)],
        compiler_params=pltpu.CompilerParams(
            dimension_semantics=("parallel", "arbitrary"), vmem_limit_bytes=VMEM_LIMIT),
        name="attention",
    )(q, k, vt)


def _mem_kv_kernel(mem_ref, gmem_ref, wkv_ref, gkn_ref, kx_ref, vx_ref):
    m = _rms_rows(mem_ref[...], gmem_ref[...]).astype(BF16)
    kv = jnp.dot(m, wkv_ref[...], preferred_element_type=F32)
    w = XA_HEADS * XA_HD
    g = gkn_ref[...]
    parts = [_rms_rows(kv[:, h * XA_HD:(h + 1) * XA_HD], g) for h in range(XA_HEADS)]
    kx_ref[...] = jnp.concatenate(parts, axis=1).astype(BF16)
    vx_ref[...] = kv[:, w:].astype(BF16)


def _mem_kv(mem2d, gmem, wkv, gkn, tm):
    n, d = mem2d.shape
    tm = _pick(n, tm)
    w = XA_HEADS * XA_HD
    full = lambda a: pl.BlockSpec(a.shape, lambda i: (0,) * a.ndim)
    return pl.pallas_call(
        _mem_kv_kernel,
        grid=(n // tm,),
        in_specs=[pl.BlockSpec((tm, d), lambda i: (i, 0)), full(gmem), full(wkv), full(gkn)],
        out_specs=[pl.BlockSpec((tm, w), lambda i: (i, 0))] * 2,
        out_shape=[jax.ShapeDtypeStruct((n, w), BF16)] * 2,
        compiler_params=pltpu.CompilerParams(dimension_semantics=("parallel",), vmem_limit_bytes=VMEM_LIMIT),
        name="mem_kv",
    )(mem2d, gmem, wkv, gkn)


def _post_kernel(x_ref, oa_ref, ob_ref, ga_ref, gb_ref, wbra_ref, wbrb_ref, wmix_ref, gcross_ref, wq_ref,
                 gqn_ref, kx_ref, vx_ref, wo_ref, gmoe_ref, wrhi_ref, wrlo_ref, x2_ref, hm_ref, aff_ref):
    ba = jnp.dot(oa_ref[...], wbra_ref[...], preferred_element_type=F32)
    bb = jnp.dot(ob_ref[...], wbrb_ref[...], preferred_element_type=F32)
    mixed = ga_ref[...].astype(F32) * ba + gb_ref[...].astype(F32) * bb
    x1 = x_ref[...] + jnp.dot(mixed.astype(BF16), wmix_ref[...], preferred_element_type=F32)

    hc = _rms_rows(x1, gcross_ref[...]).astype(BF16)
    q = jnp.dot(hc, wq_ref[...], preferred_element_type=F32)
    gqn = gqn_ref[...]
    sls = [slice(h * XA_HD, (h + 1) * XA_HD) for h in range(XA_HEADS)]
    qhs = [(_rms_rows(q[:, sl], gqn) * (XA_HD ** -0.5)).astype(BF16) for sl in sls]
    ss = [lax.dot_general(qh, kx_ref[:, sl], _NT, preferred_element_type=F32) for qh, sl in zip(qhs, sls)]
    ps = [jnp.exp(s - jnp.max(s, axis=1, keepdims=True)) for s in ss]
    pvs = [jnp.dot(p.astype(BF16), vx_ref[:, sl], preferred_element_type=F32) for p, sl in zip(ps, sls)]
    outs = [pv / jnp.sum(p, axis=1, keepdims=True) for pv, p in zip(pvs, ps)]
    ox = jnp.concatenate(outs, axis=1).astype(BF16)
    x2 = x1 + jnp.dot(ox, wo_ref[...], preferred_element_type=F32)
    x2_ref[...] = x2

    hm = _rms_rows(x2, gmoe_ref[...])
    hi = hm.astype(BF16)
    lo = (hm - hi.astype(F32)).astype(BF16)
    hm_ref[...] = hi
    dn = (((1,), (1,)), ((), ()))
    wrhi = wrhi_ref[...]
    logits = (lax.dot_general(wrhi, hi, dn, preferred_element_type=F32)
              + lax.dot_general(wrhi, lo, dn, preferred_element_type=F32)
              + lax.dot_general(wrlo_ref[...], hi, dn, preferred_element_type=F32))
    e = jnp.exp(logits - jnp.max(logits, axis=0, keepdims=True))
    aff_ref[...] = e / jnp.sum(e, axis=0, keepdims=True)


def _post(x2d, oa, ob, ga, gb, kx, vx, w, S, n_mem, tm):
    n, d = x2d.shape
    tm = _pick(S, tm)
    nt_s = S // tm
    ne = w["wrhi"].shape[0]
    full = lambda a: pl.BlockSpec(a.shape, lambda i: (0,) * a.ndim)
    rows = lambda wd: pl.BlockSpec((tm, wd), lambda i: (i, 0))
    memblk = pl.BlockSpec((n_mem, kx.shape[1]), lambda i: (i // nt_s, 0))
    c = [w[k] for k in ("wbra", "wbrb", "wmix", "gcross", "wq", "gqn")]
    c2 = [w[k] for k in ("wo", "gmoe", "wrhi", "wrlo")]
    return pl.pallas_call(
        _post_kernel,
        grid=(n // tm,),
        in_specs=[rows(d), rows(oa.shape[1]), rows(ob.shape[1]), rows(d), rows(d)] + [full(a) for a in c]
                 + [memblk, memblk] + [full(a) for a in c2],
        out_specs=[rows(d), rows(d), pl.BlockSpec((ne, tm), lambda i: (0, i))],
        out_shape=[jax.ShapeDtypeStruct((n, d), F32), jax.ShapeDtypeStruct((n, d), BF16),
                   jax.ShapeDtypeStruct((ne, n), F32)],
        compiler_params=pltpu.CompilerParams(dimension_semantics=("parallel",), vmem_limit_bytes=VMEM_LIMIT),
        name="post_mix",
    )(x2d, oa, ob, ga, gb, *c, kx, vx, *c2)


_SEL_CHUNK = 2048
_TIE_CHUNK = 512


def _topc_kernel(aff_ref, tri_ref, wsel_ref, pos_ref, *, cap):
    ne, n = aff_ref.shape
    chunk = min(_SEL_CHUNK, n)
    nchunks = n // chunk

    def bits_at(c, width):
        return pltpu.bitcast(aff_ref[:, pl.ds(pl.multiple_of(c * width, width), width)], jnp.int32)

    def count(pred_fn):
        def body(c, acc):
            return acc + pred_fn(bits_at(c, chunk)).astype(jnp.int32)
        acc = lax.fori_loop(0, nchunks, body, jnp.zeros((ne, chunk), jnp.int32))
        return jnp.sum(acc, axis=1, keepdims=True)

    def bit_step(i, t):
        cand = t | jnp.left_shift(jnp.int32(1), 30 - i)
        return jnp.where(count(lambda b: b >= cand) >= cap, cand, t)

    thr = lax.fori_loop(0, 31, bit_step, jnp.zeros((ne, 1), jnp.int32))
    need = (cap - count(lambda b: b > thr)).astype(F32)

    tchunk = min(_TIE_CHUNK, n)
    tri = tri_ref[...]

    def tie_step(c, carry):
        run_eq, run_sel = carry
        sl = pl.ds(pl.multiple_of(c * tchunk, tchunk), tchunk)
        a = aff_ref[:, sl]
        b = pltpu.bitcast(a, jnp.int32)
        eq = b == thr
        eqf = jnp.where(eq, 1.0, 0.0)
        before = run_eq + jnp.dot(eqf.astype(BF16), tri, preferred_element_type=F32)
        sel = (b > thr) | (eq & (before < need))
        self_ = jnp.where(sel, 1.0, 0.0)
        rank = run_sel + jnp.dot(self_.astype(BF16), tri, preferred_element_type=F32)
        wsel_ref[:, sl] = jnp.where(sel, a, 0.0)
        pos_ref[:, sl] = jnp.where(sel, rank, -1.0).astype(jnp.int32)
        return (run_eq + jnp.sum(eqf, axis=1, keepdims=True), run_sel + jnp.sum(self_, axis=1, keepdims=True))

    zero = jnp.zeros((ne, 1), F32)
    lax.fori_loop(0, n // tchunk, tie_step, (zero, zero))


def _topc(aff_t, cap):
    ne, n = aff_t.shape
    tchunk = min(_TIE_CHUNK, n)
    idx = jnp.arange(tchunk)
    tri = (idx[:, None] < idx[None, :]).astype(BF16)
    return pl.pallas_call(
        functools.partial(_topc_kernel, cap=cap),
        out_shape=[jax.ShapeDtypeStruct((ne, n), F32), jax.ShapeDtypeStruct((ne, n), jnp.int32)],
        compiler_params=pltpu.CompilerParams(vmem_limit_bytes=VMEM_LIMIT),
        name="expert_choice_select",
    )(aff_t, tri)


_MOE_SUB = 256
_MOE_TILE = 512
_MOE_GROUP = 4
_GRANULE = 16
_MOE_SHORT_WIN = 64


def _moe_ffn_kernel(off_ref, hm_ref, pos_ref, w_ref, wg_ref, wu_ref, wd_ref, *rest, nsub, tr, nc1, eb):
    o_refs, (stage, vstage) = rest[:eb], rest[eb:]
    grp = pl.program_id(0)
    c = pl.program_id(1)
    sub = hm_ref.shape[0] // nsub
    win = sub + _GRANULE

    @pl.when(c == 0)
    def _():
        stage[...] = jnp.zeros(stage.shape, stage.dtype)
        vstage[...] = jnp.zeros(vstage.shape, vstage.dtype)

    bases = [(grp * eb + j) * nc1 + c * nsub for j in range(eb)]
    flushed = [(off_ref[b] // tr) * tr for b in bases]

    def append(i, k16s, window):
        cols = slice(i * sub, (i + 1) * sub)
        riota = lax.broadcasted_iota(jnp.int32, (window, sub), 0)
        hits = [(pos_ref[j, :, cols] - (flushed[j] + k16s[j])) == riota for j in range(eb)]
        onehot = jnp.concatenate([jnp.where(h, 1.0, 0.0).astype(BF16) for h in hits], axis=0)
        x = jnp.dot(onehot, hm_ref[cols, :], preferred_element_type=F32)
        for j in range(eb):
            rows = pl.ds(k16s[j], window)
            xj = x[j * window:(j + 1) * window, :].astype(BF16)
            stage[j, rows, :] = stage[j, rows, :] + xj
            v = jnp.sum(jnp.where(hits[j], w_ref[j, :, cols], 0.0), axis=1, keepdims=True)
            vstage[j, rows, :] = vstage[j, rows, :] + v

    for i in range(nsub):
        k16s, widest = [], None
        for j in range(eb):
            k = off_ref[bases[j] + i] - flushed[j]
            k16 = pl.multiple_of((k // _GRANULE) * _GRANULE, _GRANULE)
            span = off_ref[bases[j] + i + 1] - flushed[j] - k16
            k16s.append(k16)
            widest = span if widest is None else jnp.maximum(widest, span)
        lax.cond(widest <= _MOE_SHORT_WIN,
                 functools.partial(append, i, k16s, _MOE_SHORT_WIN),
                 functools.partial(append, i, k16s, win))

    for j in range(eb):
        @pl.when(off_ref[bases[j] + nsub] - flushed[j] >= tr)
        def _(j=j):
            xs = stage[j, 0:tr, :]
            g = jnp.dot(xs, wg_ref[j], preferred_element_type=F32)
            u = jnp.dot(xs, wu_ref[j], preferred_element_type=F32)
            hid = (g * _sigmoid(g) * u).astype(BF16)
            out = jnp.dot(hid, wd_ref[j], preferred_element_type=F32)
            o_refs[j][...] = (out * vstage[j, 0:tr, :]).astype(o_refs[j].dtype)
            keep = stage.shape[1] - tr
            stage[j, 0:keep, :] = stage[j, tr:, :]
            stage[j, keep:, :] = jnp.zeros((tr, stage.shape[2]), stage.dtype)
            vstage[j, 0:keep, :] = vstage[j, tr:, :]
            vstage[j, keep:, :] = jnp.zeros((tr, 1), vstage.dtype)


def _moe_ffn(off, hm, pos3, wsel3, wg, wu, wd, cap, tr, nc1, eb):
    n, d = hm.shape
    ne, _, ff = wg.shape
    ts = tr
    nsub = ts // _MOE_SUB
    tiles = cap // tr
    once = pl.Buffered(1)

    def out_spec(j):
        return pl.BlockSpec((tr, d), lambda g, c, off: (
            g * tiles + jnp.minimum(off[(g * eb + j) * nc1 + c * nsub] // tr, tiles - 1), 0))

    grid_spec = pltpu.PrefetchScalarGridSpec(
        num_scalar_prefetch=1,
        grid=(ne // eb, n // ts),
        in_specs=[pl.BlockSpec((ts, d), lambda g, c, off: (c, 0)),
                  pl.BlockSpec((eb, 1, ts), lambda g, c, off: (g, 0, c)),
                  pl.BlockSpec((eb, 1, ts), lambda g, c, off: (g, 0, c)),
                  pl.BlockSpec((eb, d, ff), lambda g, c, off: (g, 0, 0), pipeline_mode=once),
                  pl.BlockSpec((eb, d, ff), lambda g, c, off: (g, 0, 0), pipeline_mode=once),
                  pl.BlockSpec((eb, ff, d), lambda g, c, off: (g, 0, 0), pipeline_mode=once)],
        out_specs=[out_spec(j) for j in range(eb)],
        scratch_shapes=[pltpu.VMEM((eb, tr + ts + _GRANULE, d), BF16),
                        pltpu.VMEM((eb, tr + ts + _GRANULE, 1), F32)])
    return pl.pallas_call(
        functools.partial(_moe_ffn_kernel, nsub=nsub, tr=tr, nc1=nc1, eb=eb),
        grid_spec=grid_spec,
        out_shape=[jax.ShapeDtypeStruct((ne // eb * cap, d), BF16)] * eb,
        compiler_params=pltpu.CompilerParams(
            dimension_semantics=("arbitrary", "arbitrary"), vmem_limit_bytes=VMEM_LIMIT),
        name="moe_ffn",
    )(off, hm, pos3, wsel3, wg, wu, wd)


def _moe_combine_kernel(off_ref, pos_ref, x2_ref, *rest, ne, nc1, cap, eb):
    rows_hbm, (y_ref, gbuf, st_scr, sems) = rest[:eb], rest[eb:]
    c = pl.program_id(0)
    nc = pl.num_programs(0)
    sub = x2_ref.shape[0]
    gran = _GRANULE
    slot = c % 2

    def granule_copy(e, src_row, dst_row, sl):
        return pltpu.make_async_copy(rows_hbm[e % eb].at[pl.ds(src_row, gran), :],
                                     gbuf.at[sl, pl.ds(dst_row, gran), :], sems.at[sl])

    def layout(cc):
        gb = jnp.int32(0)
        shifts, starts, firsts, counts = [], [gb], [], []
        for e in range(ne):
            a = off_ref[e * nc1 + cc]
            b = off_ref[e * nc1 + cc + 1]
            a16 = (a // gran) * gran
            ng = jnp.where(b > a, (b - a16 + gran - 1) // gran, 0)
            shifts.append(gb - a16)
            firsts.append((e // eb) * cap + a16)
            counts.append(ng)
            gb = gb + ng * gran
            starts.append(gb)
        return shifts, starts, firsts, counts

    def fetch(cc, sl):
        _, starts, firsts, counts = layout(cc)
        for e in range(ne):
            def issue(g, carry, e=e):
                granule_copy(e, pl.multiple_of(firsts[e] + g * gran, gran),
                             pl.multiple_of(starts[e] + g * gran, gran), sl).start()
                return carry
            lax.fori_loop(0, counts[e], issue, 0)

    @pl.when(c == 0)
    def _():
        gbuf[...] = jnp.zeros(gbuf.shape, gbuf.dtype)
        fetch(c, slot)

    @pl.when(c + 1 < nc)
    def _():
        fetch(c + 1, 1 - slot)

    shifts, starts, _, _ = layout(c)
    gb = starts[ne]

    def wait_one(g, carry):
        granule_copy(0, 0, 0, slot).wait()
        return carry

    lax.fori_loop(0, gb // gran, wait_one, 0)

    y_ref[...] = x2_ref[...]
    blk = MXU

    def kblock(kb, carry):
        r0 = pl.multiple_of(kb * blk, blk)
        riota = lax.broadcasted_iota(jnp.int32, (blk, sub), 0) + r0
        st_scr[...] = jnp.zeros(st_scr.shape, F32)
        for e in range(ne):
            @pl.when((starts[e] < r0 + blk) & (starts[e + 1] > r0))
            def _(e=e):
                p = pos_ref[e]
                st_scr[...] = jnp.where(((p + shifts[e]) == riota) & (p >= 0), 1.0, st_scr[...])
        y_ref[...] += jnp.dot(st_scr[...].T.astype(BF16), gbuf[slot, pl.ds(r0, blk), :],
                              preferred_element_type=F32)
        return carry

    lax.fori_loop(0, (gb + blk - 1) // blk, kblock, 0)


def _moe_combine(off, pos3, x2, rows, cap, nc1):
    eb = len(rows)
    n, d = x2.shape
    ne = pos3.shape[0]
    sub = _MOE_SUB
    gmax = ne * (sub + 2 * _GRANULE)
    gmax = (gmax + MXU - 1) // MXU * MXU
    grid_spec = pltpu.PrefetchScalarGridSpec(
        num_scalar_prefetch=1,
        grid=(n // sub,),
        in_specs=[pl.BlockSpec((ne, 1, sub), lambda c, off: (0, 0, c)),
                  pl.BlockSpec((sub, d), lambda c, off: (c, 0))]
                 + [pl.BlockSpec(memory_space=pl.ANY)] * eb,
        out_specs=pl.BlockSpec((sub, d), lambda c, off: (c, 0)),
        scratch_shapes=[pltpu.VMEM((2, gmax, d), BF16), pltpu.VMEM((MXU, sub), F32),
                        pltpu.SemaphoreType.DMA((2,))])
    return pl.pallas_call(
        functools.partial(_moe_combine_kernel, ne=ne, nc1=nc1, cap=cap, eb=eb),
        grid_spec=grid_spec,
        out_shape=jax.ShapeDtypeStruct((n, d), F32),
        compiler_params=pltpu.CompilerParams(dimension_semantics=("arbitrary",), vmem_limit_bytes=VMEM_LIMIT),
        name="moe_combine",
    )(off, pos3, x2, *rows)


def _moe(hm, x2, aff_t, wg, wu, wd):
    n, d = hm.shape
    ne = aff_t.shape[0]
    cap = EC_FACTOR * n // ne
    tr = min(_MOE_TILE, cap)
    assert cap % tr == 0 and n % tr == 0 and tr % _MOE_SUB == 0, (n, cap, tr)
    wsel, pos = _topc(aff_t, cap)
    nc = n // _MOE_SUB
    cnt = jnp.sum((pos >= 0).reshape(ne, nc, _MOE_SUB), axis=-1, dtype=jnp.int32)
    off = jnp.concatenate([jnp.zeros((ne, 1), jnp.int32), jnp.cumsum(cnt, axis=1, dtype=jnp.int32)], axis=1)
    off = off.reshape(-1)
    pos3 = pos.reshape(ne, 1, n)
    rows = _moe_ffn(off, hm, pos3, wsel.reshape(ne, 1, n), wg, wu, wd, cap, tr, nc + 1, _MOE_GROUP)
    return _moe_combine(off, pos3, x2, rows, cap, nc + 1)


def _prep_post_weights(p):
    row = lambda v: v.reshape(1, -1).astype(F32)
    wr = p["w_router"].T.astype(F32)
    wrhi = wr.astype(BF16)
    wrlo = (wr - wrhi.astype(F32)).astype(BF16)
    return dict(
        wbra=p["mla_w_br"].astype(BF16), wbrb=p["gqa_w_br"].astype(BF16), wmix=p["w_mix_out"].astype(BF16),
        gcross=row(p["g_cross"]), wq=p["xa_w_q"].astype(BF16), gqn=row(p["xa_gqn"]),
        wo=p["xa_w_o"].astype(BF16), gmoe=row(p["g_moe"]), wrhi=wrhi, wrlo=wrlo,
        gmem=row(p["g_mem"]), wkv=p["xa_w_kv"].astype(BF16), gkn=row(p["xa_gkn"]),
        wg=p["moe_w_gate"].astype(BF16), wu=p["moe_w_up"].astype(BF16), wd=p["moe_w_down"].astype(BF16))


def _encoder_layer(x, mem, wm, wp):
    B, S, D = x.shape
    n = B * S
    n_mem = mem.shape[1]
    x2d = x.reshape(n, D)
    qa, ka, va, qg, kg, vg, ga, gb = _mixer_in(x2d, wm, S, tm=512)
    oa = _attention(qa, ka, va, B, S, k_slab=range(MLA_HEADS), v_row=range(MLA_HEADS), tq=256, tk=256)
    rep = GQA_HEADS // GQA_KV_HEADS
    ob = _attention(qg, kg, vg, B, S, k_slab=[h // rep for h in range(GQA_HEADS)],
                    v_row=[h // rep for h in range(GQA_HEADS)], tq=256, tk=256)
    kx, vx = _mem_kv(mem.reshape(B * n_mem, D), wp["gmem"], wp["wkv"], wp["gkn"], tm=256)
    x2, hm, aff_t = _post(x2d, oa, ob, ga, gb, kx, vx, wp, S, n_mem, tm=512)
    out = _moe(hm, x2, aff_t, wp["wg"], wp["wu"], wp["wd"])
    return out.reshape(B, S, D)


def kernel(x_prompt, x_sample, mem_prompt, mem_sample, g_mix, w_in, mla_gq, mla_w_uq, mla_gkv, mla_w_ukv, mla_gqn, mla_gkn, mla_w_br, gqa_gqn, gqa_gkn, gqa_w_br, w_mix_out, g_cross, g_mem, xa_w_q, xa_w_kv, xa_gqn, xa_gkn, xa_w_o, g_moe, w_router, moe_w_gate, moe_w_up, moe_w_down):
    names = ("g_mix", "w_in", "mla_gq", "mla_w_uq", "mla_gkv", "mla_w_ukv", "mla_gqn", "mla_gkn", "mla_w_br",
             "gqa_gqn", "gqa_gkn", "gqa_w_br", "w_mix_out", "g_cross", "g_mem", "xa_w_q", "xa_w_kv", "xa_gqn",
             "xa_gkn", "xa_w_o", "g_moe", "w_router", "moe_w_gate", "moe_w_up", "moe_w_down")
    vals = (g_mix, w_in, mla_gq, mla_w_uq, mla_gkv, mla_w_ukv, mla_gqn, mla_gkn, mla_w_br,
            gqa_gqn, gqa_gkn, gqa_w_br, w_mix_out, g_cross, g_mem, xa_w_q, xa_w_kv, xa_gqn,
            xa_gkn, xa_w_o, g_moe, w_router, moe_w_gate, moe_w_up, moe_w_down)
    y_prompt, y_sample = x_prompt, x_sample
    depth = w_in.shape[0]
    for l in range(depth):
        p = {k: v[l] for k, v in zip(names, vals)}
        wm = _prep_mixer_weights(p)
        wp = _prep_post_weights(p)
        y_prompt = _encoder_layer(y_prompt, mem_prompt, wm, wp)
        y_sample = _encoder_layer(y_sample, mem_sample, wm, wp)
    return (y_prompt, y_sample)
```

```python
import functools
import math

import jax
import jax.numpy as jnp
from jax import lax
from jax.experimental import pallas as pl
from jax.experimental.pallas import tpu as pltpu

EPS = 1e-6
GRID_W = 64
ROPE_THETA = 10000.0
MLA_HEADS = 8
Q_LORA = 384
KV_LORA = 256
MLA_NOPE = 64
MLA_ROPE = 32
MLA_V = 64
GQA_HEADS = 8
GQA_KV_HEADS = 2
GQA_HD = 64
XA_HEADS = 4
XA_HD = 128
N_EXPERTS = 16
EC_FACTOR = 2

LANE = 128
MXU = 256
VMEM_LIMIT = 56 * 1024 * 1024
LOG2E = math.log2(math.e)

TM_MIXER = 512
TM_POST = 512
TM_MEM = 256
TQ_ATTN = 256
TK_ATTN = 256

F32 = jnp.float32
BF16 = jnp.bfloat16


def _pick(n, pref):
    t = min(n, pref)
    while n % t:
        t -= LANE
    return t


def _rms_rows(x, g):
    ms = jnp.mean(x * x, axis=-1, keepdims=True)
    return x * lax.rsqrt(ms + EPS) * g


def _group_mean_sq(x, bd):
    x2 = x * x
    hi = x2.astype(BF16)
    lo = (x2 - hi.astype(F32)).astype(BF16)
    w = x.shape[1]
    outs = []
    for c in range(0, w, MXU):
        cw = min(MXU, w - c)
        b = bd[:cw, :cw]
        outs.append(jnp.dot(hi[:, c:c + cw], b, preferred_element_type=F32)
                    + jnp.dot(lo[:, c:c + cw], b, preferred_element_type=F32))
    return outs[0] if len(outs) == 1 else jnp.concatenate(outs, axis=1)


def _group_rms(x, g, bd):
    return x * lax.rsqrt(_group_mean_sq(x, bd) + EPS) * g


def _rope_slabs(x, tab, shift):
    c, s1, s2 = tab[0], tab[1], tab[2]
    outs = []
    for a in range(0, x.shape[1], LANE):
        xs = x[:, a:a + LANE]
        outs.append(xs * c + pltpu.roll(xs, LANE - shift, 1) * s1 + pltpu.roll(xs, shift, 1) * s2)
    return outs[0] if len(outs) == 1 else jnp.concatenate(outs, axis=1)


def _sigmoid(x):
    return 1.0 / (1.0 + jnp.exp(-x))


_SEG_CQ = (0, Q_LORA)
_SEG_CKV = (_SEG_CQ[1], _SEG_CQ[1] + KV_LORA)
_SEG_KR = (_SEG_CKV[1], _SEG_CKV[1] + LANE)
_SEG_QB = (_SEG_KR[1], _SEG_KR[1] + GQA_HEADS * GQA_HD)
_SEG_KB = (_SEG_QB[1], _SEG_QB[1] + GQA_KV_HEADS * LANE)
_W_SLAB = MLA_HEADS * LANE

_NT = (((1,), (1,)), ((), ()))

_VROWS = MLA_V + 16


def _with_count_rows(vt, heads):
    tm = vt.shape[1]
    ones = jnp.where(lax.broadcasted_iota(jnp.int32, (16, tm), 0) == 0, 1.0, 0.0)
    parts = []
    for h in range(heads):
        parts += [vt[h * MLA_V:(h + 1) * MLA_V, :], ones]
    return jnp.concatenate(parts, axis=0).astype(BF16)


def _mixer_in_kernel(x_ref, gmix_ref, win_ref, gq_ref, wuq_ref, gkv_ref, wuk_ref, wuvt_ref, wvbt_ref,
                     gqa_ref, gka_ref, gkr_ref, gqg_ref, gkg_ref, bda_ref, bdb_ref, taba_ref, tabb_ref,
                     qa_ref, ka_ref, vat_ref, qg_ref, kg_ref, vgt_ref, ga_ref, gb_ref, *, d_model):
    h = _rms_rows(x_ref[...], gmix_ref[...]).astype(BF16)

    def proj(seg):
        return jnp.dot(h, win_ref[:, seg[0]:seg[1]], preferred_element_type=F32)

    bda = bda_ref[...]
    bdb = bdb_ref[...]
    taba = taba_ref[...]
    tabb = tabb_ref[...]

    g0 = _SEG_KB[1]
    z_cq = proj(_SEG_CQ)
    z_ckv = proj(_SEG_CKV)
    cq = _rms_rows(z_cq, gq_ref[...]).astype(BF16)
    z_kr = proj(_SEG_KR)
    qa = jnp.dot(cq, wuq_ref[...], preferred_element_type=F32)
    ckv = _rms_rows(z_ckv, gkv_ref[...]).astype(BF16)
    z_qb = proj(_SEG_QB)
    kn_raw = jnp.dot(ckv, wuk_ref[...], preferred_element_type=F32)

    qa = _rope_slabs(_group_rms(qa, gqa_ref[...], bda), taba, MLA_ROPE // 2)
    qa_ref[...] = (qa * ((MLA_NOPE + MLA_ROPE) ** -0.5 * LOG2E)).astype(BF16)
    z_kb = proj(_SEG_KB)
    vat = lax.dot_general(wuvt_ref[...], ckv, _NT, preferred_element_type=F32)

    kn = _group_rms(kn_raw, gka_ref[...], bdb)
    kr = _rope_slabs(_group_rms(z_kr, gkr_ref[...], bda), taba, MLA_ROPE // 2)
    ka_ref[...] = (kn + jnp.concatenate([kr] * MLA_HEADS, axis=1)).astype(BF16)
    z_ga = proj((g0, g0 + d_model))
    vat_ref[...] = _with_count_rows(vat, MLA_HEADS)

    qg = _rope_slabs(_group_rms(z_qb, gqg_ref[...], bdb), tabb, GQA_HD // 2) * (GQA_HD ** -0.5 * LOG2E)
    lane = lax.broadcasted_iota(jnp.int32, (1, LANE), 1)
    low = lane < GQA_HD
    parts = []
    for p in range(GQA_HEADS // 2):
        s = qg[:, p * LANE:(p + 1) * LANE]
        parts.append(jnp.where(low, s, 0.0))
        parts.append(jnp.where(low, 0.0, s))
    qg_ref[...] = jnp.concatenate(parts, axis=1).astype(BF16)
    z_gb = proj((g0 + d_model, g0 + 2 * d_model))
    kg = _rope_slabs(_group_rms(z_kb, gkg_ref[...], bdb), tabb, GQA_HD // 2)
    kg_ref[...] = kg.astype(BF16)
    vgt = lax.dot_general(wvbt_ref[...], h, _NT, preferred_element_type=F32)

    ga_ref[...] = _sigmoid(z_ga).astype(BF16)
    vgt_ref[...] = _with_count_rows(vgt, GQA_KV_HEADS)
    gb_ref[...] = _sigmoid(z_gb).astype(BF16)


def _rope_tables(S):
    rows = S // GRID_W
    row = jnp.repeat(jnp.arange(rows, dtype=F32), GRID_W)
    col = jnp.tile(jnp.arange(GRID_W, dtype=F32), rows)

    def cs(rot_dim):
        n_ax = rot_dim // 4
        freqs = 1.0 / (ROPE_THETA ** (jnp.arange(n_ax, dtype=F32) / n_ax))
        ang = jnp.concatenate([row[:, None] * freqs, col[:, None] * freqs], axis=-1)
        return jnp.cos(ang), jnp.sin(ang)

    z = lambda w: jnp.zeros((S, w), F32)
    o = lambda w: jnp.ones((S, w), F32)
    ca, sa = cs(MLA_ROPE)
    pad = LANE - MLA_NOPE - MLA_ROPE
    taba = jnp.stack([
        jnp.concatenate([o(MLA_NOPE), ca, ca, z(pad)], axis=1),
        jnp.concatenate([z(MLA_NOPE), -sa, z(MLA_ROPE // 2), z(pad)], axis=1),
        jnp.concatenate([z(MLA_NOPE), z(MLA_ROPE // 2), sa, z(pad)], axis=1)])
    cb, sb = cs(GQA_HD)
    hz = z(GQA_HD // 2)
    tabb = jnp.stack([
        jnp.concatenate([cb, cb, cb, cb], axis=1),
        jnp.concatenate([-sb, hz, -sb, hz], axis=1),
        jnp.concatenate([hz, sb, hz, sb], axis=1)])
    return taba, tabb


def _block_diag(groups):
    idx = jnp.arange(MXU)
    m = jnp.zeros((MXU, MXU), F32)
    for base in range(0, MXU, LANE):
        for start, size in groups:
            inside = (idx >= base + start) & (idx < base + start + size)
            m = m + jnp.where(inside[:, None] & inside[None, :], 1.0 / size, 0.0)
    return m.astype(BF16)


def _prep_mixer_weights(p):
    d = p["w_in"].shape[0]
    w = p["w_in"]
    o = 0
    cq = w[:, o:o + Q_LORA]; o += Q_LORA
    ckv = w[:, o:o + KV_LORA]; o += KV_LORA
    kr = w[:, o:o + MLA_ROPE]; o += MLA_ROPE
    qb = w[:, o:o + GQA_HEADS * GQA_HD]; o += GQA_HEADS * GQA_HD
    kb = w[:, o:o + GQA_KV_HEADS * GQA_HD]; o += GQA_KV_HEADS * GQA_HD
    vb = w[:, o:o + GQA_KV_HEADS * GQA_HD]; o += GQA_KV_HEADS * GQA_HD
    ga = w[:, o:o + d]; o += d
    gb = w[:, o:o + d]
    zc = lambda n: jnp.zeros((d, n), w.dtype)
    dup = lambda m: jnp.concatenate(
        [m[:, g * GQA_HD:(g + 1) * GQA_HD] for g in range(GQA_KV_HEADS) for _ in range(2)], axis=1)
    win = jnp.concatenate(
        [cq, ckv, zc(MLA_NOPE), kr, zc(LANE - MLA_NOPE - MLA_ROPE), qb, dup(kb), ga, gb], axis=1)

    dq = MLA_NOPE + MLA_ROPE
    wuq = p["mla_w_uq"].reshape(Q_LORA, MLA_HEADS, dq)
    wuq = jnp.pad(wuq, ((0, 0), (0, 0), (0, LANE - dq))).reshape(Q_LORA, _W_SLAB)
    wukv = p["mla_w_ukv"].reshape(KV_LORA, MLA_HEADS, MLA_NOPE + MLA_V)
    wk = jnp.pad(wukv[:, :, :MLA_NOPE], ((0, 0), (0, 0), (0, LANE - MLA_NOPE))).reshape(KV_LORA, _W_SLAB)
    wvt = wukv[:, :, MLA_NOPE:].reshape(KV_LORA, MLA_HEADS * MLA_V).T

    row = lambda v: v.reshape(1, -1).astype(F32)
    zl = lambda n: jnp.zeros((n,), F32)
    gqn, gkn = p["mla_gqn"], p["mla_gkn"]
    pad = LANE - dq
    gqa = jnp.tile(jnp.concatenate([gqn, zl(pad)]), MLA_HEADS)
    gka = jnp.tile(jnp.concatenate([gkn[:MLA_NOPE], zl(LANE - MLA_NOPE)]), MLA_HEADS)
    gkr = jnp.concatenate([zl(MLA_NOPE), gkn[MLA_NOPE:], zl(pad)])
    gqg = jnp.tile(p["gqa_gqn"], GQA_HEADS)
    gkg = jnp.tile(p["gqa_gkn"], 2 * GQA_KV_HEADS)
    return dict(
        gmix=row(p["g_mix"]), win=win.astype(BF16), gq=row(p["mla_gq"]), wuq=wuq.astype(BF16),
        gkv=row(p["mla_gkv"]), wuk=wk.astype(BF16), wuvt=wvt.astype(BF16), wvbt=vb.T.astype(BF16),
        gqa=row(gqa), gka=row(gka), gkr=row(gkr),
        gqg=row(gqg), gkg=row(gkg),
        bda=_block_diag([(0, MLA_NOPE), (MLA_NOPE, MLA_ROPE)]),
        bdb=_block_diag([(0, GQA_HD), (GQA_HD, GQA_HD)]))


def _mixer_in(x2d, w, S, tm):
    n, d = x2d.shape
    tm = _pick(S, tm)
    nt_s = S // tm
    taba, tabb = _rope_tables(S)
    full = lambda a: pl.BlockSpec(a.shape, lambda i: (0,) * a.ndim)
    rows = lambda wd: pl.BlockSpec((tm, wd), lambda i: (i, 0))
    tab = pl.BlockSpec((3, tm, LANE), lambda i: (0, i % nt_s, 0))
    consts = [w[k] for k in ("gmix", "win", "gq", "wuq", "gkv", "wuk", "wuvt", "wvbt", "gqa", "gka", "gkr",
                             "gqg", "gkg", "bda", "bdb")]
    cols = lambda ht: pl.BlockSpec((ht, tm), lambda i: (0, i))
    outs = ((_W_SLAB, False), (_W_SLAB, False), (MLA_HEADS * _VROWS, True), (_W_SLAB, False),
            (GQA_KV_HEADS * LANE, False), (GQA_KV_HEADS * _VROWS, True), (d, False), (d, False))
    return pl.pallas_call(
        functools.partial(_mixer_in_kernel, d_model=d),
        grid=(n // tm,),
        in_specs=[rows(d)] + [full(c) for c in consts] + [tab, tab],
        out_specs=[cols(wd) if t else rows(wd) for wd, t in outs],
        out_shape=[jax.ShapeDtypeStruct((wd, n) if t else (n, wd), BF16) for wd, t in outs],
        compiler_params=pltpu.CompilerParams(dimension_semantics=("parallel",), vmem_limit_bytes=VMEM_LIMIT),
        name="mixer_in",
    )(x2d, *consts, taba, tabb)


def _attn_kernel(q_ref, k_ref, vt_ref, o_ref, m_scr, mp_scr, acc_scr, s_scr, *, k_slab, v_row, tk, hd):
    nheads = len(k_slab)
    nk = k_ref.shape[0] // tk
    acc_scr[...] = jnp.zeros(acc_scr.shape, F32)

    def scores(h, j, slot, m_run):
        q = q_ref[:, h * LANE:(h + 1) * LANE]
        k = k_ref[pl.ds(j * tk, tk), k_slab[h] * LANE:(k_slab[h] + 1) * LANE]
        s = lax.dot_general(k, q, _NT, preferred_element_type=F32)
        s_scr[slot, h] = s
        mx = jnp.max(s, axis=0, keepdims=True)
        return mx if m_run is None else jnp.maximum(m_run, mx)

    def consume(h, j, slot):
        m_new = m_scr[h]
        alpha = jnp.exp2(mp_scr[h] - m_new)
        p = jnp.exp2(s_scr[slot, h] - m_new)
        vt = vt_ref[v_row[h] * _VROWS:(v_row[h] + 1) * _VROWS, pl.ds(j * tk, tk)]
        acc_scr[h] = alpha * acc_scr[h] + jnp.dot(vt, p.astype(BF16), preferred_element_type=F32)

    for h in range(nheads):
        m0 = scores(h, 0, 0, None)
        m_scr[h] = m0
        mp_scr[h] = m0

    for j in range(nk - 1):
        slot = j % 2
        for h in range(nheads):
            m_cur = m_scr[h]
            m_next = scores(h, j + 1, 1 - slot, m_cur)
            consume(h, j, slot)
            mp_scr[h] = m_cur
            m_scr[h] = m_next
    for h in range(nheads):
        consume(h, nk - 1, (nk - 1) % 2)
    out_t = jnp.concatenate([acc_scr[h, 0:hd, :] / acc_scr[h, hd:hd + 1, :] for h in range(nheads)], axis=0)
    o_ref[...] = out_t.T.astype(o_ref.dtype)


def _attention(q, k, vt, B, S, k_slab, v_row, tq, tk):
    n = q.shape[0]
    nheads = len(k_slab)
    hd = LANE // 2
    tq, tk = _pick(S, tq), _pick(S, tk)
    nq = S // tq
    return pl.pallas_call(
        functools.partial(_attn_kernel, k_slab=tuple(k_slab), v_row=tuple(v_row), tk=tk, hd=hd),
        grid=(B, nq),
        in_specs=[pl.BlockSpec((tq, q.shape[1]), lambda b, i: (b * nq + i, 0)),
                  pl.BlockSpec((S, k.shape[1]), lambda b, i: (b, 0)),
                  pl.BlockSpec((vt.shape[0], S), lambda b, i: (0, b))],
        out_specs=pl.BlockSpec((tq, nheads * hd), lambda b, i: (b * nq + i, 0)),
        out_shape=jax.ShapeDtypeStruct((n, nheads * hd), BF16),
        scratch_shapes=[pltpu.VMEM((nheads, 1, tq), F32), pltpu.VMEM((nheads, 1, tq), F32),
                        pltpu.VMEM((nheads, _VROWS, tq), F32), pltpu.VMEM((2, nheads, tk, tq), F32)],
        compiler_params=pltpu.CompilerParams(
            dimension_semantics=("parallel", "arbitrary"), vmem_limit_bytes=VMEM_LIMIT),
        name="attention",
    )(q, k, vt)


def _mem_kv_kernel(mem_ref, gmem_ref, wkv_ref, gkn_ref, kx_ref, vx_ref):
    m = _rms_rows(mem_ref[...], gmem_ref[...]).astype(BF16)
    kv = jnp.dot(m, wkv_ref[...], preferred_element_type=F32)
    w = XA_HEADS * XA_HD
    g = gkn_ref[...]
    parts = [_rms_rows(kv[:, h * XA_HD:(h + 1) * XA_HD], g) for h in range(XA_HEADS)]
    kx_ref[...] = jnp.concatenate(parts, axis=1).astype(BF16)
    vx_ref[...] = kv[:, w:].astype(BF16)


def _mem_kv(mem2d, gmem, wkv, gkn, tm):
    n, d = mem2d.shape
    tm = _pick(n, tm)
    w = XA_HEADS * XA_HD
    full = lambda a: pl.BlockSpec(a.shape, lambda i: (0,) * a.ndim)
    return pl.pallas_call(
        _mem_kv_kernel,
        grid=(n // tm,),
        in_specs=[pl.BlockSpec((tm, d), lambda i: (i, 0)), full(gmem), full(wkv), full(gkn)],
        out_specs=[pl.BlockSpec((tm, w), lambda i: (i, 0))] * 2,
        out_shape=[jax.ShapeDtypeStruct((n, w), BF16)] * 2,
        compiler_params=pltpu.CompilerParams(dimension_semantics=("parallel",), vmem_limit_bytes=VMEM_LIMIT),
        name="mem_kv",
    )(mem2d, gmem, wkv, gkn)


def _post_kernel(x_ref, oa_ref, ob_ref, ga_ref, gb_ref, wbra_ref, wbrb_ref, wmix_ref, gcross_ref, wq_ref,
                 gqn_ref, kx_ref, vx_ref, wo_ref, gmoe_ref, wrhi_ref, wrlo_ref, x2_ref, hm_ref, aff_ref):
    ba = jnp.dot(oa_ref[...], wbra_ref[...], preferred_element_type=F32)
    bb = jnp.dot(ob_ref[...], wbrb_ref[...], preferred_element_type=F32)
    mixed = ga_ref[...].astype(F32) * ba + gb_ref[...].astype(F32) * bb
    x1 = x_ref[...] + jnp.dot(mixed.astype(BF16), wmix_ref[...], preferred_element_type=F32)

    hc = _rms_rows(x1, gcross_ref[...]).astype(BF16)
    q = jnp.dot(hc, wq_ref[...], preferred_element_type=F32)
    gqn = gqn_ref[...]
    sls = [slice(h * XA_HD, (h + 1) * XA_HD) for h in range(XA_HEADS)]
    qhs = [(_rms_rows(q[:, sl], gqn) * (XA_HD ** -0.5)).astype(BF16) for sl in sls]
    ss = [lax.dot_general(qh, kx_ref[:, sl], _NT, preferred_element_type=F32) for qh, sl in zip(qhs, sls)]
    ps = [jnp.exp(s - jnp.max(s, axis=1, keepdims=True)) for s in ss]
    pvs = [jnp.dot(p.astype(BF16), vx_ref[:, sl], preferred_element_type=F32) for p, sl in zip(ps, sls)]
    outs = [pv / jnp.sum(p, axis=1, keepdims=True) for pv, p in zip(pvs, ps)]
    ox = jnp.concatenate(outs, axis=1).astype(BF16)
    x2 = x1 + jnp.dot(ox, wo_ref[...], preferred_element_type=F32)
    x2_ref[...] = x2

    hm = _rms_rows(x2, gmoe_ref[...])
    hi = hm.astype(BF16)
    lo = (hm - hi.astype(F32)).astype(BF16)
    hm_ref[...] = hi
    dn = (((1,), (1,)), ((), ()))
    wrhi = wrhi_ref[...]
    logits = (lax.dot_general(wrhi, hi, dn, preferred_element_type=F32)
              + lax.dot_general(wrhi, lo, dn, preferred_element_type=F32)
              + lax.dot_general(wrlo_ref[...], hi, dn, preferred_element_type=F32))
    e = jnp.exp(logits - jnp.max(logits, axis=0, keepdims=True))
    aff_ref[...] = e / jnp.sum(e, axis=0, keepdims=True)


def _post(x2d, oa, ob, ga, gb, kx, vx, w, S, n_mem, tm):
    n, d = x2d.shape
    tm = _pick(S, tm)
    nt_s = S // tm
    ne = w["wrhi"].shape[0]
    full = lambda a: pl.BlockSpec(a.shape, lambda i: (0,) * a.ndim)
    rows = lambda wd: pl.BlockSpec((tm, wd), lambda i: (i, 0))
    memblk = pl.BlockSpec((n_mem, kx.shape[1]), lambda i: (i // nt_s, 0))
    c = [w[k] for k in ("wbra", "wbrb", "wmix", "gcross", "wq", "gqn")]
    c2 = [w[k] for k in ("wo", "gmoe", "wrhi", "wrlo")]
    return pl.pallas_call(
        _post_kernel,
        grid=(n // tm,),
        in_specs=[rows(d), rows(oa.shape[1]), rows(ob.shape[1]), rows(d), rows(d)] + [full(a) for a in c]
                 + [memblk, memblk] + [full(a) for a in c2],
        out_specs=[rows(d), rows(d), pl.BlockSpec((ne, tm), lambda i: (0, i))],
        out_shape=[jax.ShapeDtypeStruct((n, d), F32), jax.ShapeDtypeStruct((n, d), BF16),
                   jax.ShapeDtypeStruct((ne, n), F32)],
        compiler_params=pltpu.CompilerParams(dimension_semantics=("parallel",), vmem_limit_bytes=VMEM_LIMIT),
        name="post_mix",
    )(x2d, oa, ob, ga, gb, *c, kx, vx, *c2)


_SEL_CHUNK = 2048
_TIE_CHUNK = 512


def _topc_kernel(aff_ref, tri_ref, wsel_ref, pos_ref, *, cap):
    ne, n = aff_ref.shape
    chunk = min(_SEL_CHUNK, n)
    nchunks = n // chunk

    def bits_at(c, width):
        return pltpu.bitcast(aff_ref[:, pl.ds(pl.multiple_of(c * width, width), width)], jnp.int32)

    def count(pred_fn):
        def body(c, acc):
            return acc + pred_fn(bits_at(c, chunk)).astype(jnp.int32)
        acc = lax.fori_loop(0, nchunks, body, jnp.zeros((ne, chunk), jnp.int32))
        return jnp.sum(acc, axis=1, keepdims=True)

    def bit_step(i, t):
        cand = t | jnp.left_shift(jnp.int32(1), 30 - i)
        return jnp.where(count(lambda b: b >= cand) >= cap, cand, t)

    thr = lax.fori_loop(0, 31, bit_step, jnp.zeros((ne, 1), jnp.int32))
    need = (cap - count(lambda b: b > thr)).astype(F32)

    tchunk = min(_TIE_CHUNK, n)
    tri = tri_ref[...]

    def tie_step(c, carry):
        run_eq, run_sel = carry
        sl = pl.ds(pl.multiple_of(c * tchunk, tchunk), tchunk)
        a = aff_ref[:, sl]
        b = pltpu.bitcast(a, jnp.int32)
        eq = b == thr
        eqf = jnp.where(eq, 1.0, 0.0)
        before = run_eq + jnp.dot(eqf.astype(BF16), tri, preferred_element_type=F32)
        sel = (b > thr) | (eq & (before < need))
        self_ = jnp.where(sel, 1.0, 0.0)
        rank = run_sel + jnp.dot(self_.astype(BF16), tri, preferred_element_type=F32)
        wsel_ref[:, sl] = jnp.where(sel, a, 0.0)
        pos_ref[:, sl] = jnp.where(sel, rank, -1.0).astype(jnp.int32)
        return (run_eq + jnp.sum(eqf, axis=1, keepdims=True), run_sel + jnp.sum(self_, axis=1, keepdims=True))

    zero = jnp.zeros((ne, 1), F32)
    lax.fori_loop(0, n // tchunk, tie_step, (zero, zero))


def _topc(aff_t, cap):
    ne, n = aff_t.shape
    tchunk = min(_TIE_CHUNK, n)
    idx = jnp.arange(tchunk)
    tri = (idx[:, None] < idx[None, :]).astype(BF16)
    return pl.pallas_call(
        functools.partial(_topc_kernel, cap=cap),
        out_shape=[jax.ShapeDtypeStruct((ne, n), F32), jax.ShapeDtypeStruct((ne, n), jnp.int32)],
        compiler_params=pltpu.CompilerParams(vmem_limit_bytes=VMEM_LIMIT),
        name="expert_choice_select",
    )(aff_t, tri)


_MOE_SUB = 256
_MOE_TILE = 512
_MOE_GROUP = 4
_GRANULE = 16
_MOE_SHORT_WIN = 64


def _moe_ffn_kernel(off_ref, hm_ref, pos_ref, w_ref, wg_ref, wu_ref, wd_ref, *rest, nsub, tr, nc1, eb):
    o_refs, (stage, vstage) = rest[:eb], rest[eb:]
    grp = pl.program_id(0)
    c = pl.program_id(1)
    sub = hm_ref.shape[0] // nsub
    win = sub + _GRANULE

    @pl.when(c == 0)
    def _():
        stage[...] = jnp.zeros(stage.shape, stage.dtype)
        vstage[...] = jnp.zeros(vstage.shape, vstage.dtype)

    bases = [(grp * eb + j) * nc1 + c * nsub for j in range(eb)]
    flushed = [(off_ref[b] // tr) * tr for b in bases]

    def append(i, k16s, window):
        cols = slice(i * sub, (i + 1) * sub)
        riota = lax.broadcasted_iota(jnp.int32, (window, sub), 0)
        hits = [(pos_ref[j, :, cols] - (flushed[j] + k16s[j])) == riota for j in range(eb)]
        onehot = jnp.concatenate([jnp.where(h, 1.0, 0.0).astype(BF16) for h in hits], axis=0)
        x = jnp.dot(onehot, hm_ref[cols, :], preferred_element_type=F32)
        for j in range(eb):
            rows = pl.ds(k16s[j], window)
            xj = x[j * window:(j + 1) * window, :].astype(BF16)
            stage[j, rows, :] = stage[j, rows, :] + xj
            v = jnp.sum(jnp.where(hits[j], w_ref[j, :, cols], 0.0), axis=1, keepdims=True)
            vstage[j, rows, :] = vstage[j, rows, :] + v

    for i in range(nsub):
        k16s, widest = [], None
        for j in range(eb):
            k = off_ref[bases[j] + i] - flushed[j]
            k16 = pl.multiple_of((k // _GRANULE) * _GRANULE, _GRANULE)
            span = off_ref[bases[j] + i + 1] - flushed[j] - k16
            k16s.append(k16)
            widest = span if widest is None else jnp.maximum(widest, span)
        lax.cond(widest <= _MOE_SHORT_WIN,
                 functools.partial(append, i, k16s, _MOE_SHORT_WIN),
                 functools.partial(append, i, k16s, win))

    for j in range(eb):
        @pl.when(off_ref[bases[j] + nsub] - flushed[j] >= tr)
        def _(j=j):
            xs = stage[j, 0:tr, :]
            g = jnp.dot(xs, wg_ref[j], preferred_element_type=F32)
            u = jnp.dot(xs, wu_ref[j], preferred_element_type=F32)
            hid = (g * _sigmoid(g) * u).astype(BF16)
            out = jnp.dot(hid, wd_ref[j], preferred_element_type=F32)
            o_refs[j][...] = (out * vstage[j, 0:tr, :]).astype(o_refs[j].dtype)
            keep = stage.shape[1] - tr
            stage[j, 0:keep, :] = stage[j, tr:, :]
            stage[j, keep:, :] = jnp.zeros((tr, stage.shape[2]), stage.dtype)
            vstage[j, 0:keep, :] = vstage[j, tr:, :]
            vstage[j, keep:, :] = jnp.zeros((tr, 1), vstage.dtype)


def _moe_ffn(off, hm, pos3, wsel3, wg, wu, wd, cap, tr, nc1, eb):
    n, d = hm.shape
    ne, _, ff = wg.shape
    ts = tr
    nsub = ts // _MOE_SUB
    tiles = cap // tr
    once = pl.Buffered(1)

    def out_spec(j):
        return pl.BlockSpec((tr, d), lambda g, c, off: (
            g * tiles + jnp.minimum(off[(g * eb + j) * nc1 + c * nsub] // tr, tiles - 1), 0))

    grid_spec = pltpu.PrefetchScalarGridSpec(
        num_scalar_prefetch=1,
        grid=(ne // eb, n // ts),
        in_specs=[pl.BlockSpec((ts, d), lambda g, c, off: (c, 0)),
                  pl.BlockSpec((eb, 1, ts), lambda g, c, off: (g, 0, c)),
                  pl.BlockSpec((eb, 1, ts), lambda g, c, off: (g, 0, c)),
                  pl.BlockSpec((eb, d, ff), lambda g, c, off: (g, 0, 0), pipeline_mode=once),
                  pl.BlockSpec((eb, d, ff), lambda g, c, off: (g, 0, 0), pipeline_mode=once),
                  pl.BlockSpec((eb, ff, d), lambda g, c, off: (g, 0, 0), pipeline_mode=once)],
        out_specs=[out_spec(j) for j in range(eb)],
        scratch_shapes=[pltpu.VMEM((eb, tr + ts + _GRANULE, d), BF16),
                        pltpu.VMEM((eb, tr + ts + _GRANULE, 1), F32)])
    return pl.pallas_call(
        functools.partial(_moe_ffn_kernel, nsub=nsub, tr=tr, nc1=nc1, eb=eb),
        grid_spec=grid_spec,
        out_shape=[jax.ShapeDtypeStruct((ne // eb * cap, d), BF16)] * eb,
        compiler_params=pltpu.CompilerParams(
            dimension_semantics=("arbitrary", "arbitrary"), vmem_limit_bytes=VMEM_LIMIT),
        name="moe_ffn",
    )(off, hm, pos3, wsel3, wg, wu, wd)


def _moe_combine_kernel(off_ref, pos_ref, x2_ref, *rest, ne, nc1, cap, eb):
    rows_hbm, (y_ref, gbuf, st_scr, sems) = rest[:eb], rest[eb:]
    c = pl.program_id(0)
    nc = pl.num_programs(0)
    sub = x2_ref.shape[0]
    gran = _GRANULE
    slot = c % 2

    def granule_copy(e, src_row, dst_row, sl):
        return pltpu.make_async_copy(rows_hbm[e % eb].at[pl.ds(src_row, gran), :],
                                     gbuf.at[sl, pl.ds(dst_row, gran), :], sems.at[sl])

    def layout(cc):
        gb = jnp.int32(0)
        shifts, starts, firsts, counts = [], [gb], [], []
        for e in range(ne):
            a = off_ref[e * nc1 + cc]
            b = off_ref[e * nc1 + cc + 1]
            a16 = (a // gran) * gran
            ng = jnp.where(b > a, (b - a16 + gran - 1) // gran, 0)
            shifts.append(gb - a16)
            firsts.append((e // eb) * cap + a16)
            counts.append(ng)
            gb = gb + ng * gran
            starts.append(gb)
        return shifts, starts, firsts, counts

    def fetch(cc, sl):
        _, starts, firsts, counts = layout(cc)
        for e in range(ne):
            def issue(g, carry, e=e):
                granule_copy(e, pl.multiple_of(firsts[e] + g * gran, gran),
                             pl.multiple_of(starts[e] + g * gran, gran), sl).start()
                return carry
            lax.fori_loop(0, counts[e], issue, 0)

    @pl.when(c == 0)
    def _():
        gbuf[...] = jnp.zeros(gbuf.shape, gbuf.dtype)
        fetch(c, slot)

    @pl.when(c + 1 < nc)
    def _():
        fetch(c + 1, 1 - slot)

    shifts, starts, _, _ = layout(c)
    gb = starts[ne]

    def wait_one(g, carry):
        granule_copy(0, 0, 0, slot).wait()
        return carry

    lax.fori_loop(0, gb // gran, wait_one, 0)

    y_ref[...] = x2_ref[...]
    blk = MXU

    def kblock(kb, carry):
        r0 = pl.multiple_of(kb * blk, blk)
        riota = lax.broadcasted_iota(jnp.int32, (blk, sub), 0) + r0
        st_scr[...] = jnp.zeros(st_scr.shape, F32)
        for e in range(ne):
            @pl.when((starts[e] < r0 + blk) & (starts[e + 1] > r0))
            def _(e=e):
                p = pos_ref[e]
                st_scr[...] = jnp.where(((p + shifts[e]) == riota) & (p >= 0), 1.0, st_scr[...])
        y_ref[...] += jnp.dot(st_scr[...].T.astype(BF16), gbuf[slot, pl.ds(r0, blk), :],
                              preferred_element_type=F32)
        return carry

    lax.fori_loop(0, (gb + blk - 1) // blk, kblock, 0)


def _moe_combine(off, pos3, x2, rows, cap, nc1):
    eb = len(rows)
    n, d = x2.shape
    ne = pos3.shape[0]
    sub = _MOE_SUB
    gmax = ne * (sub + 2 * _GRANULE)
    gmax = (gmax + MXU - 1) // MXU * MXU
    grid_spec = pltpu.PrefetchScalarGridSpec(
        num_scalar_prefetch=1,
        grid=(n // sub,),
        in_specs=[pl.BlockSpec((ne, 1, sub), lambda c, off: (0, 0, c)),
                  pl.BlockSpec((sub, d), lambda c, off: (c, 0))]
                 + [pl.BlockSpec(memory_space=pl.ANY)] * eb,
        out_specs=pl.BlockSpec((sub, d), lambda c, off: (c, 0)),
        scratch_shapes=[pltpu.VMEM((2, gmax, d), BF16), pltpu.VMEM((MXU, sub), F32),
                        pltpu.SemaphoreType.DMA((2,))])
    return pl.pallas_call(
        functools.partial(_moe_combine_kernel, ne=ne, nc1=nc1, cap=cap, eb=eb),
        grid_spec=grid_spec,
        out_shape=jax.ShapeDtypeStruct((n, d), F32),
        compiler_params=pltpu.CompilerParams(dimension_semantics=("arbitrary",), vmem_limit_bytes=VMEM_LIMIT),
        name="moe_combine",
    )(off, pos3, x2, *rows)


def _moe(hm, x2, aff_t, wg, wu, wd):
    n, d = hm.shape
    ne = aff_t.shape[0]
    cap = EC_FACTOR * n // ne
    tr = min(_MOE_TILE, cap)
    assert cap % tr == 0 and n % tr == 0 and tr % _MOE_SUB == 0, (n, cap, tr)
    wsel, pos = _topc(aff_t, cap)
    nc = n // _MOE_SUB
    cnt = jnp.sum((pos >= 0).reshape(ne, nc, _MOE_SUB), axis=-1, dtype=jnp.int32)
    off = jnp.concatenate([jnp.zeros((ne, 1), jnp.int32), jnp.cumsum(cnt, axis=1, dtype=jnp.int32)], axis=1)
    off = off.reshape(-1)
    pos3 = pos.reshape(ne, 1, n)
    rows = _moe_ffn(off, hm, pos3, wsel.reshape(ne, 1, n), wg, wu, wd, cap, tr, nc + 1, _MOE_GROUP)
    return _moe_combine(off, pos3, x2, rows, cap, nc + 1)


def _prep_post_weights(p):
    row = lambda v: v.reshape(1, -1).astype(F32)
    wr = p["w_router"].T.astype(F32)
    wrhi = wr.astype(BF16)
    wrlo = (wr - wrhi.astype(F32)).astype(BF16)
    return dict(
        wbra=p["mla_w_br"].astype(BF16), wbrb=p["gqa_w_br"].astype(BF16), wmix=p["w_mix_out"].astype(BF16),
        gcross=row(p["g_cross"]), wq=p["xa_w_q"].astype(BF16), gqn=row(p["xa_gqn"]),
        wo=p["xa_w_o"].astype(BF16), gmoe=row(p["g_moe"]), wrhi=wrhi, wrlo=wrlo,
        gmem=row(p["g_mem"]), wkv=p["xa_w_kv"].astype(BF16), gkn=row(p["xa_gkn"]),
        wg=p["moe_w_gate"].astype(BF16), wu=p["moe_w_up"].astype(BF16), wd=p["moe_w_down"].astype(BF16))


def _encoder_layer(x, mem, wm, wp):
    B, S, D = x.shape
    n = B * S
    n_mem = mem.shape[1]
    x2d = x.reshape(n, D)
    qa, ka, va, qg, kg, vg, ga, gb = _mixer_in(x2d, wm, S, tm=TM_MIXER)
    oa = _attention(qa, ka, va, B, S, k_slab=range(MLA_HEADS), v_row=range(MLA_HEADS), tq=TQ_ATTN, tk=TK_ATTN)
    rep = GQA_HEADS // GQA_KV_HEADS
    ob = _attention(qg, kg, vg, B, S, k_slab=[h // rep for h in range(GQA_HEADS)],
                    v_row=[h // rep for h in range(GQA_HEADS)], tq=TQ_ATTN, tk=TK_ATTN)
    kx, vx = _mem_kv(mem.reshape(B * n_mem, D), wp["gmem"], wp["wkv"], wp["gkn"], tm=TM_MEM)
    x2, hm, aff_t = _post(x2d, oa, ob, ga, gb, kx, vx, wp, S, n_mem, tm=TM_POST)
    out = _moe(hm, x2, aff_t, wp["wg"], wp["wu"], wp["wd"])
    return out.reshape(B, S, D)


def kernel(x_prompt, x_sample, mem_prompt, mem_sample, g_mix, w_in, mla_gq, mla_w_uq, mla_gkv, mla_w_ukv, mla_gqn, mla_gkn, mla_w_br, gqa_gqn, gqa_gkn, gqa_w_br, w_mix_out, g_cross, g_mem, xa_w_q, xa_w_kv, xa_gqn, xa_gkn, xa_w_o, g_moe, w_router, moe_w_gate, moe_w_up, moe_w_down):
    names = ("g_mix", "w_in", "mla_gq", "mla_w_uq", "mla_gkv", "mla_w_ukv", "mla_gqn", "mla_gkn", "mla_w_br",
             "gqa_gqn", "gqa_gkn", "gqa_w_br", "w_mix_out", "g_cross", "g_mem", "xa_w_q", "xa_w_kv", "xa_gqn",
             "xa_gkn", "xa_w_o", "g_moe", "w_router", "moe_w_gate", "moe_w_up", "moe_w_down")
    vals = (g_mix, w_in, mla_gq, mla_w_uq, mla_gkv, mla_w_ukv, mla_gqn, mla_gkn, mla_w_br,
            gqa_gqn, gqa_gkn, gqa_w_br, w_mix_out, g_cross, g_mem, xa_w_q, xa_w_kv, xa_gqn,
            xa_gkn, xa_w_o, g_moe, w_router, moe_w_gate, moe_w_up, moe_w_down)
    y_prompt, y_sample = x_prompt, x_sample
    depth = w_in.shape[0]
    for l in range(depth):
        p = {k: v[l] for k, v in zip(names, vals)}
        wm = _prep_mixer_weights(p)
        wp = _prep_post_weights(p)
        y_prompt = _encoder_layer(y_prompt, mem_prompt, wm, wp)
        y_sample = _encoder_layer(y_sample, mem_sample, wm, wp)
    return (y_prompt, y_sample)
```

```python
import functools
import math

import jax
import jax.numpy as jnp
from jax import lax
from jax.experimental import pallas as pl
from jax.experimental.pallas import tpu as pltpu

EPS = 1e-6
GRID_W = 64
ROPE_THETA = 10000.0
MLA_HEADS = 8
Q_LORA = 384
KV_LORA = 256
MLA_NOPE = 64
MLA_ROPE = 32
MLA_V = 64
GQA_HEADS = 8
GQA_KV_HEADS = 2
GQA_HD = 64
XA_HEADS = 4
XA_HD = 128
N_EXPERTS = 16
EC_FACTOR = 2

LANE = 128
MXU = 256
VMEM_LIMIT = 56 * 1024 * 1024
LOG2E = math.log2(math.e)

TM_MIXER = 512
TM_POST = 512
TM_MEM = 256
TQ_ATTN = 256
TK_ATTN = 256

F32 = jnp.float32
BF16 = jnp.bfloat16


def _pick(n, pref):
    t = min(n, pref)
    while n % t:
        t -= LANE
    return t


def _rms_rows(x, g):
    ms = jnp.mean(x * x, axis=-1, keepdims=True)
    return x * lax.rsqrt(ms + EPS) * g


def _group_mean_sq(x, bd):
    x2 = x * x
    hi = x2.astype(BF16)
    lo = (x2 - hi.astype(F32)).astype(BF16)
    w = x.shape[1]
    outs = []
    for c in range(0, w, MXU):
        cw = min(MXU, w - c)
        b = bd[:cw, :cw]
        outs.append(jnp.dot(hi[:, c:c + cw], b, preferred_element_type=F32)
                    + jnp.dot(lo[:, c:c + cw], b, preferred_element_type=F32))
    return outs[0] if len(outs) == 1 else jnp.concatenate(outs, axis=1)


def _group_rms(x, g, bd):
    return x * lax.rsqrt(_group_mean_sq(x, bd) + EPS) * g


def _rope_slabs(x, tab, shift):
    c, s1, s2 = tab[0], tab[1], tab[2]
    outs = []
    for a in range(0, x.shape[1], LANE):
        xs = x[:, a:a + LANE]
        outs.append(xs * c + pltpu.roll(xs, LANE - shift, 1) * s1 + pltpu.roll(xs, shift, 1) * s2)
    return outs[0] if len(outs) == 1 else jnp.concatenate(outs, axis=1)


def _sigmoid(x):
    return 1.0 / (1.0 + jnp.exp(-x))


_SEG_CQ = (0, Q_LORA)
_SEG_CKV = (_SEG_CQ[1], _SEG_CQ[1] + KV_LORA)
_SEG_KR = (_SEG_CKV[1], _SEG_CKV[1] + LANE)
_SEG_QB = (_SEG_KR[1], _SEG_KR[1] + GQA_HEADS * GQA_HD)
_SEG_KB = (_SEG_QB[1], _SEG_QB[1] + GQA_KV_HEADS * LANE)
_W_SLAB = MLA_HEADS * LANE

_NT = (((1,), (1,)), ((), ()))

_VROWS = MLA_V + 16


def _with_count_rows(vt, heads):
    tm = vt.shape[1]
    ones = jnp.where(lax.broadcasted_iota(jnp.int32, (16, tm), 0) == 0, 1.0, 0.0)
    parts = []
    for h in range(heads):
        parts += [vt[h * MLA_V:(h + 1) * MLA_V, :], ones]
    return jnp.concatenate(parts, axis=0).astype(BF16)


def _mixer_in_kernel(x_ref, gmix_ref, win_ref, gq_ref, wuq_ref, gkv_ref, wuk_ref, wuvt_ref, wvbt_ref,
                     gqa_ref, gka_ref, gkr_ref, gqg_ref, gkg_ref, bda_ref, bdb_ref, taba_ref, tabb_ref,
                     qa_ref, ka_ref, vat_ref, qg_ref, kg_ref, vgt_ref, ga_ref, gb_ref, *, d_model):
    h = _rms_rows(x_ref[...], gmix_ref[...]).astype(BF16)

    def proj(seg):
        return jnp.dot(h, win_ref[:, seg[0]:seg[1]], preferred_element_type=F32)

    bda = bda_ref[...]
    bdb = bdb_ref[...]
    taba = taba_ref[...]
    tabb = tabb_ref[...]

    g0 = _SEG_KB[1]
    z_cq = proj(_SEG_CQ)
    z_ckv = proj(_SEG_CKV)
    cq = _rms_rows(z_cq, gq_ref[...]).astype(BF16)
    z_kr = proj(_SEG_KR)
    qa = jnp.dot(cq, wuq_ref[...], preferred_element_type=F32)
    ckv = _rms_rows(z_ckv, gkv_ref[...]).astype(BF16)
    z_qb = proj(_SEG_QB)
    kn_raw = jnp.dot(ckv, wuk_ref[...], preferred_element_type=F32)

    qa = _rope_slabs(_group_rms(qa, gqa_ref[...], bda), taba, MLA_ROPE // 2)
    qa_ref[...] = (qa * ((MLA_NOPE + MLA_ROPE) ** -0.5 * LOG2E)).astype(BF16)
    z_kb = proj(_SEG_KB)
    vat = lax.dot_general(wuvt_ref[...], ckv, _NT, preferred_element_type=F32)

    kn = _group_rms(kn_raw, gka_ref[...], bdb)
    kr = _rope_slabs(_group_rms(z_kr, gkr_ref[...], bda), taba, MLA_ROPE // 2)
    ka_ref[...] = (kn + jnp.concatenate([kr] * MLA_HEADS, axis=1)).astype(BF16)
    z_ga = proj((g0, g0 + d_model))
    vat_ref[...] = _with_count_rows(vat, MLA_HEADS)

    qg = _rope_slabs(_group_rms(z_qb, gqg_ref[...], bdb), tabb, GQA_HD // 2) * (GQA_HD ** -0.5 * LOG2E)
    lane = lax.broadcasted_iota(jnp.int32, (1, LANE), 1)
    low = lane < GQA_HD
    parts = []
    for p in range(GQA_HEADS // 2):
        s = qg[:, p * LANE:(p + 1) * LANE]
        parts.append(jnp.where(low, s, 0.0))
        parts.append(jnp.where(low, 0.0, s))
    qg_ref[...] = jnp.concatenate(parts, axis=1).astype(BF16)
    z_gb = proj((g0 + d_model, g0 + 2 * d_model))
    kg = _rope_slabs(_group_rms(z_kb, gkg_ref[...], bdb), tabb, GQA_HD // 2)
    kg_ref[...] = kg.astype(BF16)
    vgt = lax.dot_general(wvbt_ref[...], h, _NT, preferred_element_type=F32)

    ga_ref[...] = _sigmoid(z_ga).astype(BF16)
    vgt_ref[...] = _with_count_rows(vgt, GQA_KV_HEADS)
    gb_ref[...] = _sigmoid(z_gb).astype(BF16)


def _rope_tables(S):
    rows = S // GRID_W
    row = jnp.repeat(jnp.arange(rows, dtype=F32), GRID_W)
    col = jnp.tile(jnp.arange(GRID_W, dtype=F32), rows)

    def cs(rot_dim):
        n_ax = rot_dim // 4
        freqs = 1.0 / (ROPE_THETA ** (jnp.arange(n_ax, dtype=F32) / n_ax))
        ang = jnp.concatenate([row[:, None] * freqs, col[:, None] * freqs], axis=-1)
        return jnp.cos(ang), jnp.sin(ang)

    z = lambda w: jnp.zeros((S, w), F32)
    o = lambda w: jnp.ones((S, w), F32)
    ca, sa = cs(MLA_ROPE)
    pad = LANE - MLA_NOPE - MLA_ROPE
    taba = jnp.stack([
        jnp.concatenate([o(MLA_NOPE), ca, ca, z(pad)], axis=1),
        jnp.concatenate([z(MLA_NOPE), -sa, z(MLA_ROPE // 2), z(pad)], axis=1),
        jnp.concatenate([z(MLA_NOPE), z(MLA_ROPE // 2), sa, z(pad)], axis=1)])
    cb, sb = cs(GQA_HD)
    hz = z(GQA_HD // 2)
    tabb = jnp.stack([
        jnp.concatenate([cb, cb, cb, cb], axis=1),
        jnp.concatenate([-sb, hz, -sb, hz], axis=1),
        jnp.concatenate([hz, sb, hz, sb], axis=1)])
    return taba, tabb


def _block_diag(groups):
    idx = jnp.arange(MXU)
    m = jnp.zeros((MXU, MXU), F32)
    for base in range(0, MXU, LANE):
        for start, size in groups:
            inside = (idx >= base + start) & (idx < base + start + size)
            m = m + jnp.where(inside[:, None] & inside[None, :], 1.0 / size, 0.0)
    return m.astype(BF16)


def _prep_mixer_weights(p):
    d = p["w_in"].shape[0]
    w = p["w_in"]
    o = 0
    cq = w[:, o:o + Q_LORA]; o += Q_LORA
    ckv = w[:, o:o + KV_LORA]; o += KV_LORA
    kr = w[:, o:o + MLA_ROPE]; o += MLA_ROPE
    qb = w[:, o:o + GQA_HEADS * GQA_HD]; o += GQA_HEADS * GQA_HD
    kb = w[:, o:o + GQA_KV_HEADS * GQA_HD]; o += GQA_KV_HEADS * GQA_HD
    vb = w[:, o:o + GQA_KV_HEADS * GQA_HD]; o += GQA_KV_HEADS * GQA_HD
    ga = w[:, o:o + d]; o += d
    gb = w[:, o:o + d]
    zc = lambda n: jnp.zeros((d, n), w.dtype)
    dup = lambda m: jnp.concatenate(
        [m[:, g * GQA_HD:(g + 1) * GQA_HD] for g in range(GQA_KV_HEADS) for _ in range(2)], axis=1)
    win = jnp.concatenate(
        [cq, ckv, zc(MLA_NOPE), kr, zc(LANE - MLA_NOPE - MLA_ROPE), qb, dup(kb), ga, gb], axis=1)

    dq = MLA_NOPE + MLA_ROPE
    wuq = p["mla_w_uq"].reshape(Q_LORA, MLA_HEADS, dq)
    wuq = jnp.pad(wuq, ((0, 0), (0, 0), (0, LANE - dq))).reshape(Q_LORA, _W_SLAB)
    wukv = p["mla_w_ukv"].reshape(KV_LORA, MLA_HEADS, MLA_NOPE + MLA_V)
    wk = jnp.pad(wukv[:, :, :MLA_NOPE], ((0, 0), (0, 0), (0, LANE - MLA_NOPE))).reshape(KV_LORA, _W_SLAB)
    wvt = wukv[:, :, MLA_NOPE:].reshape(KV_LORA, MLA_HEADS * MLA_V).T

    row = lambda v: v.reshape(1, -1).astype(F32)
    zl = lambda n: jnp.zeros((n,), F32)
    gqn, gkn = p["mla_gqn"], p["mla_gkn"]
    pad = LANE - dq
    gqa = jnp.tile(jnp.concatenate([gqn, zl(pad)]), MLA_HEADS)
    gka = jnp.tile(jnp.concatenate([gkn[:MLA_NOPE], zl(LANE - MLA_NOPE)]), MLA_HEADS)
    gkr = jnp.concatenate([zl(MLA_NOPE), gkn[MLA_NOPE:], zl(pad)])
    gqg = jnp.tile(p["gqa_gqn"], GQA_HEADS)
    gkg = jnp.tile(p["gqa_gkn"], 2 * GQA_KV_HEADS)
    return dict(
        gmix=row(p["g_mix"]), win=win.astype(BF16), gq=row(p["mla_gq"]), wuq=wuq.astype(BF16),
        gkv=row(p["mla_gkv"]), wuk=wk.astype(BF16), wuvt=wvt.astype(BF16), wvbt=vb.T.astype(BF16),
        gqa=row(gqa), gka=row(gka), gkr=row(gkr),
        gqg=row(gqg), gkg=row(gkg),
        bda=_block_diag([(0, MLA_NOPE), (MLA_NOPE, MLA_ROPE)]),
        bdb=_block_diag([(0, GQA_HD), (GQA_HD, GQA_HD)]))


def _mixer_in(x2d, w, S, tm):
    n, d = x2d.shape
    tm = _pick(S, tm)
    nt_s = S // tm
    taba, tabb = _rope_tables(S)
    full = lambda a: pl.BlockSpec(a.shape, lambda i: (0,) * a.ndim)
    rows = lambda wd: pl.BlockSpec((tm, wd), lambda i: (i, 0))
    tab = pl.BlockSpec((3, tm, LANE), lambda i: (0, i % nt_s, 0))
    consts = [w[k] for k in ("gmix", "win", "gq", "wuq", "gkv", "wuk", "wuvt", "wvbt", "gqa", "gka", "gkr",
                             "gqg", "gkg", "bda", "bdb")]
    cols = lambda ht: pl.BlockSpec((ht, tm), lambda i: (0, i))
    outs = ((_W_SLAB, False), (_W_SLAB, False), (MLA_HEADS * _VROWS, True), (_W_SLAB, False),
            (GQA_KV_HEADS * LANE, False), (GQA_KV_HEADS * _VROWS, True), (d, False), (d, False))
    return pl.pallas_call(
        functools.partial(_mixer_in_kernel, d_model=d),
        grid=(n // tm,),
        in_specs=[rows(d)] + [full(c) for c in consts] + [tab, tab],
        out_specs=[cols(wd) if t else rows(wd) for wd, t in outs],
        out_shape=[jax.ShapeDtypeStruct((wd, n) if t else (n, wd), BF16) for wd, t in outs],
        compiler_params=pltpu.CompilerParams(dimension_semantics=("parallel",), vmem_limit_bytes=VMEM_LIMIT),
        name="mixer_in",
    )(x2d, *consts, taba, tabb)


def _attn_kernel(q_ref, k_ref, vt_ref, o_ref, m_scr, mp_scr, acc_scr, s_scr, *, k_slab, v_row, tk, hd):
    nheads = len(k_slab)
    nk = k_ref.shape[0] // tk
    acc_scr[...] = jnp.zeros(acc_scr.shape, F32)

    def scores(h, j, slot, m_run):
        q = q_ref[:, h * LANE:(h + 1) * LANE]
        k = k_ref[pl.ds(j * tk, tk), k_slab[h] * LANE:(k_slab[h] + 1) * LANE]
        s = lax.dot_general(k, q, _NT, preferred_element_type=F32)
        s_scr[slot, h] = s
        mx = jnp.max(s, axis=0, keepdims=True)
        return mx if m_run is None else jnp.maximum(m_run, mx)

    def consume(h, j, slot):
        m_new = m_scr[h]
        alpha = jnp.exp2(mp_scr[h] - m_new)
        p = jnp.exp2(s_scr[slot, h] - m_new)
        vt = vt_ref[v_row[h] * _VROWS:(v_row[h] + 1) * _VROWS, pl.ds(j * tk, tk)]
        acc_scr[h] = alpha * acc_scr[h] + jnp.dot(vt, p.astype(BF16), preferred_element_type=F32)

    for h in range(nheads):
        m0 = scores(h, 0, 0, None)
        m_scr[h] = m0
        mp_scr[h] = m0

    for j in range(nk - 1):
        slot = j % 2
        for h in range(nheads):
            m_cur = m_scr[h]
            m_next = scores(h, j + 1, 1 - slot, m_cur)
            consume(h, j, slot)
            mp_scr[h] = m_cur
            m_scr[h] = m_next
    for h in range(nheads):
        consume(h, nk - 1, (nk - 1) % 2)
    out_t = jnp.concatenate([acc_scr[h, 0:hd, :] / acc_scr[h, hd:hd + 1, :] for h in range(nheads)], axis=0)
    o_ref[...] = out_t.T.astype(o_ref.dtype)


def _attention(q, k, vt, B, S, k_slab, v_row, tq, tk):
    n = q.shape[0]
    nheads = len(k_slab)
    hd = LANE // 2
    tq, tk = _pick(S, tq), _pick(S, tk)
    nq = S // tq
    return pl.pallas_call(
        functools.partial(_attn_kernel, k_slab=tuple(k_slab), v_row=tuple(v_row), tk=tk, hd=hd),
        grid=(B, nq),
        in_specs=[pl.BlockSpec((tq, q.shape[1]), lambda b, i: (b * nq + i, 0)),
                  pl.BlockSpec((S, k.shape[1]), lambda b, i: (b, 0)),
                  pl.BlockSpec((vt.shape[0], S), lambda b, i: (0, b))],
        out_specs=pl.BlockSpec((tq, nheads * hd), lambda b, i: (b * nq + i, 0)),
        out_shape=jax.ShapeDtypeStruct((n, nheads * hd), BF16),
        scratch_shapes=[pltpu.VMEM((nheads, 1, tq), F32), pltpu.VMEM((nheads, 1, tq), F32),
                        pltpu.VMEM((nheads, _VROWS, tq), F32), pltpu.VMEM((2, nheads, tk, tq), F32)],
        compiler_params=pltpu.CompilerParams(
            dimension_semantics=("parallel", "arbitrary"), vmem_limit_bytes=VMEM_LIMIT),
        name="attention",
    )(q, k, vt)


def _mem_kv_kernel(mem_ref, gmem_ref, wkv_ref, gkn_ref, kx_ref, vx_ref):
    m = _rms_rows(mem_ref[...], gmem_ref[...]).astype(BF16)
    kv = jnp.dot(m, wkv_ref[...], preferred_element_type=F32)
    w = XA_HEADS * XA_HD
    g = gkn_ref[...]
    parts = [_rms_rows(kv[:, h * XA_HD:(h + 1) * XA_HD], g) for h in range(XA_HEADS)]
    kx_ref[...] = jnp.concatenate(parts, axis=1).astype(BF16)
    vx_ref[...] = kv[:, w:].astype(BF16)


def _mem_kv(mem2d, gmem, wkv, gkn, tm):
    n, d = mem2d.shape
    tm = _pick(n, tm)
    w = XA_HEADS * XA_HD
    full = lambda a: pl.BlockSpec(a.shape, lambda i: (0,) * a.ndim)
    return pl.pallas_call(
        _mem_kv_kernel,
        grid=(n // tm,),
        in_specs=[pl.BlockSpec((tm, d), lambda i: (i, 0)), full(gmem), full(wkv), full(gkn)],
        out_specs=[pl.BlockSpec((tm, w), lambda i: (i, 0))] * 2,
        out_shape=[jax.ShapeDtypeStruct((n, w), BF16)] * 2,
        compiler_params=pltpu.CompilerParams(dimension_semantics=("parallel",), vmem_limit_bytes=VMEM_LIMIT),
        name="mem_kv",
    )(mem2d, gmem, wkv, gkn)


def _post_kernel(x_ref, oa_ref, ob_ref, ga_ref, gb_ref, wbra_ref, wbrb_ref, wmix_ref, gcross_ref, wq_ref,
                 gqn_ref, kx_ref, vx_ref, wo_ref, gmoe_ref, wrhi_ref, wrlo_ref, x2_ref, hm_ref, aff_ref):
    ba = jnp.dot(oa_ref[...], wbra_ref[...], preferred_element_type=F32)
    bb = jnp.dot(ob_ref[...], wbrb_ref[...], preferred_element_type=F32)
    mixed = ga_ref[...].astype(F32) * ba + gb_ref[...].astype(F32) * bb
    x1 = x_ref[...] + jnp.dot(mixed.astype(BF16), wmix_ref[...], preferred_element_type=F32)

    hc = _rms_rows(x1, gcross_ref[...]).astype(BF16)
    q = jnp.dot(hc, wq_ref[...], preferred_element_type=F32)
    gqn = gqn_ref[...]
    sls = [slice(h * XA_HD, (h + 1) * XA_HD) for h in range(XA_HEADS)]
    qhs = [(_rms_rows(q[:, sl], gqn) * (XA_HD ** -0.5)).astype(BF16) for sl in sls]
    ss = [lax.dot_general(qh, kx_ref[:, sl], _NT, preferred_element_type=F32) for qh, sl in zip(qhs, sls)]
    ps = [jnp.exp(s - jnp.max(s, axis=1, keepdims=True)) for s in ss]
    pvs = [jnp.dot(p.astype(BF16), vx_ref[:, sl], preferred_element_type=F32) for p, sl in zip(ps, sls)]
    outs = [pv / jnp.sum(p, axis=1, keepdims=True) for pv, p in zip(pvs, ps)]
    ox = jnp.concatenate(outs, axis=1).astype(BF16)
    x2 = x1 + jnp.dot(ox, wo_ref[...], preferred_element_type=F32)
    x2_ref[...] = x2

    hm = _rms_rows(x2, gmoe_ref[...])
    hi = hm.astype(BF16)
    lo = (hm - hi.astype(F32)).astype(BF16)
    hm_ref[...] = hi
    dn = (((1,), (1,)), ((), ()))
    wrhi = wrhi_ref[...]
    logits = (lax.dot_general(wrhi, hi, dn, preferred_element_type=F32)
              + lax.dot_general(wrhi, lo, dn, preferred_element_type=F32)
              + lax.dot_general(wrlo_ref[...], hi, dn, preferred_element_type=F32))
    e = jnp.exp(logits - jnp.max(logits, axis=0, keepdims=True))
    aff_ref[...] = e / jnp.sum(e, axis=0, keepdims=True)


def _post(x2d, oa, ob, ga, gb, kx, vx, w, S, n_mem, tm):
    n, d = x2d.shape
    tm = _pick(S, tm)
    nt_s = S // tm
    ne = w["wrhi"].shape[0]
    full = lambda a: pl.BlockSpec(a.shape, lambda i: (0,) * a.ndim)
    rows = lambda wd: pl.BlockSpec((tm, wd), lambda i: (i, 0))
    memblk = pl.BlockSpec((n_mem, kx.shape[1]), lambda i: (i // nt_s, 0))
    c = [w[k] for k in ("wbra", "wbrb", "wmix", "gcross", "wq", "gqn")]
    c2 = [w[k] for k in ("wo", "gmoe", "wrhi", "wrlo")]
    return pl.pallas_call(
        _post_kernel,
        grid=(n // tm,),
        in_specs=[rows(d), rows(oa.shape[1]), rows(ob.shape[1]), rows(d), rows(d)] + [full(a) for a in c]
                 + [memblk, memblk] + [full(a) for a in c2],
        out_specs=[rows(d), rows(d), pl.BlockSpec((ne, tm), lambda i: (0, i))],
        out_shape=[jax.ShapeDtypeStruct((n, d), F32), jax.ShapeDtypeStruct((n, d), BF16),
                   jax.ShapeDtypeStruct((ne, n), F32)],
        compiler_params=pltpu.CompilerParams(dimension_semantics=("parallel",), vmem_limit_bytes=VMEM_LIMIT),
        name="post_mix",
    )(x2d, oa, ob, ga, gb, *c, kx, vx, *c2)


_SEL_CHUNK = 2048
_TIE_CHUNK = 512


def _topc_kernel(aff_ref, tri_ref, wsel_ref, pos_ref, *, cap):
    ne, n = aff_ref.shape
    chunk = min(_SEL_CHUNK, n)
    nchunks = n // chunk

    def bits_at(c, width):
        return pltpu.bitcast(aff_ref[:, pl.ds(pl.multiple_of(c * width, width), width)], jnp.int32)

    def count(pred_fn):
        def body(c, acc):
            return acc + pred_fn(bits_at(c, chunk)).astype(jnp.int32)
        acc = lax.fori_loop(0, nchunks, body, jnp.zeros((ne, chunk), jnp.int32))
        return jnp.sum(acc, axis=1, keepdims=True)

    def bit_step(i, t):
        cand = t | jnp.left_shift(jnp.int32(1), 30 - i)
        return jnp.where(count(lambda b: b >= cand) >= cap, cand, t)

    thr = lax.fori_loop(0, 31, bit_step, jnp.zeros((ne, 1), jnp.int32))
    need = (cap - count(lambda b: b > thr)).astype(F32)

    tchunk = min(_TIE_CHUNK, n)
    tri = tri_ref[...]

    def tie_step(c, carry):
        run_eq, run_sel = carry
        sl = pl.ds(pl.multiple_of(c * tchunk, tchunk), tchunk)
        a = aff_ref[:, sl]
        b = pltpu.bitcast(a, jnp.int32)
        eq = b == thr
        eqf = jnp.where(eq, 1.0, 0.0)
        before = run_eq + jnp.dot(eqf.astype(BF16), tri, preferred_element_type=F32)
        sel = (b > thr) | (eq & (before < need))
        self_ = jnp.where(sel, 1.0, 0.0)
        rank = run_sel + jnp.dot(self_.astype(BF16), tri, preferred_element_type=F32)
        wsel_ref[:, sl] = jnp.where(sel, a, 0.0)
        pos_ref[:, sl] = jnp.where(sel, rank, -1.0).astype(jnp.int32)
        return (run_eq + jnp.sum(eqf, axis=1, keepdims=True), run_sel + jnp.sum(self_, axis=1, keepdims=True))

    zero = jnp.zeros((ne, 1), F32)
    lax.fori_loop(0, n // tchunk, tie_step, (zero, zero))


def _topc(aff_t, cap):
    ne, n = aff_t.shape
    tchunk = min(_TIE_CHUNK, n)
    idx = jnp.arange(tchunk)
    tri = (idx[:, None] < idx[None, :]).astype(BF16)
    return pl.pallas_call(
        functools.partial(_topc_kernel, cap=cap),
        out_shape=[jax.ShapeDtypeStruct((ne, n), F32), jax.ShapeDtypeStruct((ne, n), jnp.int32)],
        compiler_params=pltpu.CompilerParams(vmem_limit_bytes=VMEM_LIMIT),
        name="expert_choice_select",
    )(aff_t, tri)


_MOE_SUB = 256
_MOE_TILE = 512
_MOE_GROUP = 4
_MOE_STEP = 1024
_GRANULE = 16
_MOE_SHORT_WIN = 64


def _moe_ffn_kernel(off_ref, hm_ref, pos_ref, w_ref, wg_ref, wu_ref, wd_ref, *rest, nsub, tr, nc1, eb, tiles):
    o_hbm, (stage, vstage, obuf, sems) = rest[:eb], rest[eb:]
    grp = pl.program_id(0)
    c = pl.program_id(1)
    sub = _MOE_SUB
    win = sub + _GRANULE
    per_flush = tr // sub

    @pl.when(c == 0)
    def _():
        stage[...] = jnp.zeros(stage.shape, stage.dtype)
        vstage[...] = jnp.zeros(vstage.shape, vstage.dtype)

    bases = [(grp * eb + j) * nc1 + c * nsub for j in range(eb)]

    def tile_copy(j, tile, slot):
        dst = o_hbm[j].at[pl.ds(pl.multiple_of((grp * tiles + tile) * tr, tr), tr), :]
        return pltpu.make_async_copy(obuf.at[j, slot], dst, sems.at[j, slot])

    def append(i, flushed, k16s, window):
        cols = slice(i * sub, (i + 1) * sub)
        riota = lax.broadcasted_iota(jnp.int32, (window, sub), 0)
        hits = [(pos_ref[j, :, cols] - (flushed[j] + k16s[j])) == riota for j in range(eb)]
        onehot = jnp.concatenate([jnp.where(h, 1.0, 0.0).astype(BF16) for h in hits], axis=0)
        x = jnp.dot(onehot, hm_ref[cols, :], preferred_element_type=F32)
        for j in range(eb):
            rows = pl.ds(k16s[j], window)
            xj = x[j * window:(j + 1) * window, :].astype(BF16)
            stage[j, rows, :] = stage[j, rows, :] + xj
            v = jnp.sum(jnp.where(hits[j], w_ref[j, :, cols], 0.0), axis=1, keepdims=True)
            vstage[j, rows, :] = vstage[j, rows, :] + v

    for part in range(nsub // per_flush):
        first = part * per_flush
        flushed = [(off_ref[b + first] // tr) * tr for b in bases]
        for i in range(first, first + per_flush):
            k16s, widest = [], None
            for j in range(eb):
                k = off_ref[bases[j] + i] - flushed[j]
                k16 = pl.multiple_of((k // _GRANULE) * _GRANULE, _GRANULE)
                span = off_ref[bases[j] + i + 1] - flushed[j] - k16
                k16s.append(k16)
                widest = span if widest is None else jnp.maximum(widest, span)
            lax.cond(widest <= _MOE_SHORT_WIN,
                     functools.partial(append, i, flushed, k16s, _MOE_SHORT_WIN),
                     functools.partial(append, i, flushed, k16s, win))

        for j in range(eb):
            @pl.when(off_ref[bases[j] + first + per_flush] - flushed[j] >= tr)
            def _(j=j):
                tile = flushed[j] // tr
                slot = tile % 2

                @pl.when(tile >= 2)
                def _():
                    tile_copy(j, tile - 2, slot).wait()

                xs = stage[j, 0:tr, :]
                g = jnp.dot(xs, wg_ref[j], preferred_element_type=F32)
                u = jnp.dot(xs, wu_ref[j], preferred_element_type=F32)
                hid = (g * _sigmoid(g) * u).astype(BF16)
                out = jnp.dot(hid, wd_ref[j], preferred_element_type=F32)
                obuf[j, slot] = (out * vstage[j, 0:tr, :]).astype(obuf.dtype)
                tile_copy(j, tile, slot).start()
                keep = stage.shape[1] - tr
                stage[j, 0:keep, :] = stage[j, tr:, :]
                stage[j, keep:, :] = jnp.zeros((tr, stage.shape[2]), stage.dtype)
                vstage[j, 0:keep, :] = vstage[j, tr:, :]
                vstage[j, keep:, :] = jnp.zeros((tr, 1), vstage.dtype)

    @pl.when(c == pl.num_programs(1) - 1)
    def _():
        for j in range(eb):
            for tile in range(max(tiles - 2, 0), tiles):
                tile_copy(j, tile, tile % 2).wait()


def _moe_ffn(off, hm, pos3, wsel3, wg, wu, wd, cap, tr, nc1, eb):
    n, d = hm.shape
    ne, _, ff = wg.shape
    ts = min(_MOE_STEP, n)
    assert ts % tr == 0 and n % ts == 0, (n, ts, tr)
    nsub = ts // _MOE_SUB
    tiles = cap // tr
    once = pl.Buffered(1)
    grid_spec = pltpu.PrefetchScalarGridSpec(
        num_scalar_prefetch=1,
        grid=(ne // eb, n // ts),
        in_specs=[pl.BlockSpec((ts, d), lambda g, c, off: (c, 0)),
                  pl.BlockSpec((eb, 1, ts), lambda g, c, off: (g, 0, c)),
                  pl.BlockSpec((eb, 1, ts), lambda g, c, off: (g, 0, c)),
                  pl.BlockSpec((eb, d, ff), lambda g, c, off: (g, 0, 0), pipeline_mode=once),
                  pl.BlockSpec((eb, d, ff), lambda g, c, off: (g, 0, 0), pipeline_mode=once),
                  pl.BlockSpec((eb, ff, d), lambda g, c, off: (g, 0, 0), pipeline_mode=once)],
        out_specs=[pl.BlockSpec(memory_space=pl.ANY)] * eb,
        scratch_shapes=[pltpu.VMEM((eb, 2 * tr + _GRANULE, d), BF16),
                        pltpu.VMEM((eb, 2 * tr + _GRANULE, 1), F32),
                        pltpu.VMEM((eb, 2, tr, d), BF16),
                        pltpu.SemaphoreType.DMA((eb, 2))])
    return pl.pallas_call(
        functools.partial(_moe_ffn_kernel, nsub=nsub, tr=tr, nc1=nc1, eb=eb, tiles=tiles),
        grid_spec=grid_spec,
        out_shape=[jax.ShapeDtypeStruct((ne // eb * cap, d), BF16)] * eb,
        compiler_params=pltpu.CompilerParams(
            dimension_semantics=("arbitrary", "arbitrary"), vmem_limit_bytes=VMEM_LIMIT),
        name="moe_ffn",
    )(off, hm, pos3, wsel3, wg, wu, wd)


def _moe_combine_kernel(off_ref, pos_ref, x2_ref, *rest, ne, nc1, cap, eb):
    rows_hbm, (y_ref, gbuf, st_scr, sems) = rest[:eb], rest[eb:]
    c = pl.program_id(0)
    nc = pl.num_programs(0)
    sub = x2_ref.shape[0]
    gran = _GRANULE
    slot = c % 2

    def granule_copy(e, src_row, dst_row, sl):
        return pltpu.make_async_copy(rows_hbm[e % eb].at[pl.ds(src_row, gran), :],
                                     gbuf.at[sl, pl.ds(dst_row, gran), :], sems.at[sl])

    def layout(cc):
        gb = jnp.int32(0)
        shifts, starts, firsts, counts = [], [gb], [], []
        for e in range(ne):
            a = off_ref[e * nc1 + cc]
            b = off_ref[e * nc1 + cc + 1]
            a16 = (a // gran) * gran
            ng = jnp.where(b > a, (b - a16 + gran - 1) // gran, 0)
            shifts.append(gb - a16)
            firsts.append((e // eb) * cap + a16)
            counts.append(ng)
            gb = gb + ng * gran
            starts.append(gb)
        return shifts, starts, firsts, counts

    def fetch(cc, sl):
        _, starts, firsts, counts = layout(cc)
        for e in range(ne):
            def issue(g, carry, e=e):
                granule_copy(e, pl.multiple_of(firsts[e] + g * gran, gran),
                             pl.multiple_of(starts[e] + g * gran, gran), sl).start()
                return carry
            lax.fori_loop(0, counts[e], issue, 0)

    @pl.when(c == 0)
    def _():
        gbuf[...] = jnp.zeros(gbuf.shape, gbuf.dtype)
        fetch(c, slot)

    @pl.when(c + 1 < nc)
    def _():
        fetch(c + 1, 1 - slot)

    shifts, starts, _, _ = layout(c)
    gb = starts[ne]

    def wait_one(g, carry):
        granule_copy(0, 0, 0, slot).wait()
        return carry

    lax.fori_loop(0, gb // gran, wait_one, 0)

    y_ref[...] = x2_ref[...]
    blk = MXU

    def kblock(kb, carry):
        r0 = pl.multiple_of(kb * blk, blk)
        riota = lax.broadcasted_iota(jnp.int32, (blk, sub), 0) + r0
        st_scr[...] = jnp.zeros(st_scr.shape, F32)
        for e in range(ne):
            @pl.when((starts[e] < r0 + blk) & (starts[e + 1] > r0))
            def _(e=e):
                p = pos_ref[e]
                st_scr[...] = jnp.where(((p + shifts[e]) == riota) & (p >= 0), 1.0, st_scr[...])
        y_ref[...] += jnp.dot(st_scr[...].T.astype(BF16), gbuf[slot, pl.ds(r0, blk), :],
                              preferred_element_type=F32)
        return carry

    lax.fori_loop(0, (gb + blk - 1) // blk, kblock, 0)


def _moe_combine(off, pos3, x2, rows, cap, nc1):
    eb = len(rows)
    n, d = x2.shape
    ne = pos3.shape[0]
    sub = _MOE_SUB
    gmax = ne * (sub + 2 * _GRANULE)
    gmax = (gmax + MXU - 1) // MXU * MXU
    grid_spec = pltpu.PrefetchScalarGridSpec(
        num_scalar_prefetch=1,
        grid=(n // sub,),
        in_specs=[pl.BlockSpec((ne, 1, sub), lambda c, off: (0, 0, c)),
                  pl.BlockSpec((sub, d), lambda c, off: (c, 0))]
                 + [pl.BlockSpec(memory_space=pl.ANY)] * eb,
        out_specs=pl.BlockSpec((sub, d), lambda c, off: (c, 0)),
        scratch_shapes=[pltpu.VMEM((2, gmax, d), BF16), pltpu.VMEM((MXU, sub), F32),
                        pltpu.SemaphoreType.DMA((2,))])
    return pl.pallas_call(
        functools.partial(_moe_combine_kernel, ne=ne, nc1=nc1, cap=cap, eb=eb),
        grid_spec=grid_spec,
        out_shape=jax.ShapeDtypeStruct((n, d), F32),
        compiler_params=pltpu.CompilerParams(dimension_semantics=("arbitrary",), vmem_limit_bytes=VMEM_LIMIT),
        name="moe_combine",
    )(off, pos3, x2, *rows)


def _moe(hm, x2, aff_t, wg, wu, wd):
    n, d = hm.shape
    ne = aff_t.shape[0]
    cap = EC_FACTOR * n // ne
    tr = min(_MOE_TILE, cap)
    assert cap % tr == 0 and n % tr == 0 and tr % _MOE_SUB == 0, (n, cap, tr)
    wsel, pos = _topc(aff_t, cap)
    nc = n // _MOE_SUB
    cnt = jnp.sum((pos >= 0).reshape(ne, nc, _MOE_SUB), axis=-1, dtype=jnp.int32)
    off = jnp.concatenate([jnp.zeros((ne, 1), jnp.int32), jnp.cumsum(cnt, axis=1, dtype=jnp.int32)], axis=1)
    off = off.reshape(-1)
    pos3 = pos.reshape(ne, 1, n)
    rows = _moe_ffn(off, hm, pos3, wsel.reshape(ne, 1, n), wg, wu, wd, cap, tr, nc + 1, _MOE_GROUP)
    return _moe_combine(off, pos3, x2, rows, cap, nc + 1)


def _prep_post_weights(p):
    row = lambda v: v.reshape(1, -1).astype(F32)
    wr = p["w_router"].T.astype(F32)
    wrhi = wr.astype(BF16)
    wrlo = (wr - wrhi.astype(F32)).astype(BF16)
    return dict(
        wbra=p["mla_w_br"].astype(BF16), wbrb=p["gqa_w_br"].astype(BF16), wmix=p["w_mix_out"].astype(BF16),
        gcross=row(p["g_cross"]), wq=p["xa_w_q"].astype(BF16), gqn=row(p["xa_gqn"]),
        wo=p["xa_w_o"].astype(BF16), gmoe=row(p["g_moe"]), wrhi=wrhi, wrlo=wrlo,
        gmem=row(p["g_mem"]), wkv=p["xa_w_kv"].astype(BF16), gkn=row(p["xa_gkn"]),
        wg=p["moe_w_gate"].astype(BF16), wu=p["moe_w_up"].astype(BF16), wd=p["moe_w_down"].astype(BF16))


def _encoder_layer(x, mem, wm, wp):
    B, S, D = x.shape
    n = B * S
    n_mem = mem.shape[1]
    x2d = x.reshape(n, D)
    qa, ka, va, qg, kg, vg, ga, gb = _mixer_in(x2d, wm, S, tm=TM_MIXER)
    oa = _attention(qa, ka, va, B, S, k_slab=range(MLA_HEADS), v_row=range(MLA_HEADS), tq=TQ_ATTN, tk=TK_ATTN)
    rep = GQA_HEADS // GQA_KV_HEADS
    ob = _attention(qg, kg, vg, B, S, k_slab=[h // rep for h in range(GQA_HEADS)],
                    v_row=[h // rep for h in range(GQA_HEADS)], tq=TQ_ATTN, tk=TK_ATTN)
    kx, vx = _mem_kv(mem.reshape(B * n_mem, D), wp["gmem"], wp["wkv"], wp["gkn"], tm=TM_MEM)
    x2, hm, aff_t = _post(x2d, oa, ob, ga, gb, kx, vx, wp, S, n_mem, tm=TM_POST)
    out = _moe(hm, x2, aff_t, wp["wg"], wp["wu"], wp["wd"])
    return out.reshape(B, S, D)


def kernel(x_prompt, x_sample, mem_prompt, mem_sample, g_mix, w_in, mla_gq, mla_w_uq, mla_gkv, mla_w_ukv, mla_gqn, mla_gkn, mla_w_br, gqa_gqn, gqa_gkn, gqa_w_br, w_mix_out, g_cross, g_mem, xa_w_q, xa_w_kv, xa_gqn, xa_gkn, xa_w_o, g_moe, w_router, moe_w_gate, moe_w_up, moe_w_down):
    names = ("g_mix", "w_in", "mla_gq", "mla_w_uq", "mla_gkv", "mla_w_ukv", "mla_gqn", "mla_gkn", "mla_w_br",
             "gqa_gqn", "gqa_gkn", "gqa_w_br", "w_mix_out", "g_cross", "g_mem", "xa_w_q", "xa_w_kv", "xa_gqn",
             "xa_gkn", "xa_w_o", "g_moe", "w_router", "moe_w_gate", "moe_w_up", "moe_w_down")
    vals = (g_mix, w_in, mla_gq, mla_w_uq, mla_gkv, mla_w_ukv, mla_gqn, mla_gkn, mla_w_br,
            gqa_gqn, gqa_gkn, gqa_w_br, w_mix_out, g_cross, g_mem, xa_w_q, xa_w_kv, xa_gqn,
            xa_gkn, xa_w_o, g_moe, w_router, moe_w_gate, moe_w_up, moe_w_down)
    y_prompt, y_sample = x_prompt, x_sample
    depth = w_in.shape[0]
    for l in range(depth):
        p = {k: v[l] for k, v in zip(names, vals)}
        wm = _prep_mixer_weights(p)
        wp = _prep_post_weights(p)
        y_prompt = _encoder_layer(y_prompt, mem_prompt, wm, wp)
        y_sample = _encoder_layer(y_sample, mem_sample, wm, wp)
    return (y_prompt, y_sample)
```

```python
import functools
import math

import jax
import jax.numpy as jnp
from jax import lax
from jax.experimental import pallas as pl
from jax.experimental.pallas import tpu as pltpu

EPS = 1e-6
GRID_W = 64
ROPE_THETA = 10000.0
MLA_HEADS = 8
Q_LORA = 384
KV_LORA = 256
MLA_NOPE = 64
MLA_ROPE = 32
MLA_V = 64
GQA_HEADS = 8
GQA_KV_HEADS = 2
GQA_HD = 64
XA_HEADS = 4
XA_HD = 128
N_EXPERTS = 16
EC_FACTOR = 2

LANE = 128
MXU = 256
VMEM_LIMIT = 56 * 1024 * 1024
LOG2E = math.log2(math.e)

TM_MIXER = 512
TM_POST = 512
TM_MEM = 256
TQ_ATTN = 256
TK_ATTN = 256

F32 = jnp.float32
BF16 = jnp.bfloat16


def _pick(n, pref):
    t = min(n, pref)
    while n % t:
        t -= LANE
    return t


def _rms_rows(x, g):
    ms = jnp.mean(x * x, axis=-1, keepdims=True)
    return x * lax.rsqrt(ms + EPS) * g


def _group_mean_sq(x, bd):
    x2 = x * x
    hi = x2.astype(BF16)
    lo = (x2 - hi.astype(F32)).astype(BF16)
    w = x.shape[1]
    outs = []
    for c in range(0, w, MXU):
        cw = min(MXU, w - c)
        b = bd[:cw, :cw]
        outs.append(jnp.dot(hi[:, c:c + cw], b, preferred_element_type=F32)
                    + jnp.dot(lo[:, c:c + cw], b, preferred_element_type=F32))
    return outs[0] if len(outs) == 1 else jnp.concatenate(outs, axis=1)


def _group_rms(x, g, bd):
    return x * lax.rsqrt(_group_mean_sq(x, bd) + EPS) * g


def _rope_slabs(x, tab, shift):
    c, s1, s2 = tab[0], tab[1], tab[2]
    outs = []
    for a in range(0, x.shape[1], LANE):
        xs = x[:, a:a + LANE]
        outs.append(xs * c + pltpu.roll(xs, LANE - shift, 1) * s1 + pltpu.roll(xs, shift, 1) * s2)
    return outs[0] if len(outs) == 1 else jnp.concatenate(outs, axis=1)


def _sigmoid(x):
    return 1.0 / (1.0 + jnp.exp(-x))


_SEG_CQ = (0, Q_LORA)
_SEG_CKV = (_SEG_CQ[1], _SEG_CQ[1] + KV_LORA)
_SEG_KR = (_SEG_CKV[1], _SEG_CKV[1] + LANE)
_SEG_QB = (_SEG_KR[1], _SEG_KR[1] + GQA_HEADS * GQA_HD)
_SEG_KB = (_SEG_QB[1], _SEG_QB[1] + GQA_KV_HEADS * LANE)
_W_SLAB = MLA_HEADS * LANE

_NT = (((1,), (1,)), ((), ()))

_VROWS = MLA_V + 16


def _with_count_rows(vt, heads):
    tm = vt.shape[1]
    ones = jnp.where(lax.broadcasted_iota(jnp.int32, (16, tm), 0) == 0, 1.0, 0.0)
    parts = []
    for h in range(heads):
        parts += [vt[h * MLA_V:(h + 1) * MLA_V, :], ones]
    return jnp.concatenate(parts, axis=0).astype(BF16)


def _mixer_in_kernel(x_ref, gmix_ref, win_ref, gq_ref, wuq_ref, gkv_ref, wuk_ref, wuvt_ref, wvbt_ref,
                     gqa_ref, gka_ref, gkr_ref, gqg_ref, gkg_ref, bda_ref, bdb_ref, taba_ref, tabb_ref,
                     qa_ref, ka_ref, vat_ref, qg_ref, kg_ref, vgt_ref, ga_ref, gb_ref, *, d_model):
    h = _rms_rows(x_ref[...], gmix_ref[...]).astype(BF16)

    def proj(seg):
        return jnp.dot(h, win_ref[:, seg[0]:seg[1]], preferred_element_type=F32)

    bda = bda_ref[...]
    bdb = bdb_ref[...]
    taba = taba_ref[...]
    tabb = tabb_ref[...]

    g0 = _SEG_KB[1]
    z_cq = proj(_SEG_CQ)
    z_ckv = proj(_SEG_CKV)
    cq = _rms_rows(z_cq, gq_ref[...]).astype(BF16)
    z_kr = proj(_SEG_KR)
    qa = jnp.dot(cq, wuq_ref[...], preferred_element_type=F32)
    ckv = _rms_rows(z_ckv, gkv_ref[...]).astype(BF16)
    z_qb = proj(_SEG_QB)
    kn_raw = jnp.dot(ckv, wuk_ref[...], preferred_element_type=F32)

    qa = _rope_slabs(_group_rms(qa, gqa_ref[...], bda), taba, MLA_ROPE // 2)
    qa_ref[...] = (qa * ((MLA_NOPE + MLA_ROPE) ** -0.5 * LOG2E)).astype(BF16)
    z_kb = proj(_SEG_KB)
    vat = lax.dot_general(wuvt_ref[...], ckv, _NT, preferred_element_type=F32)

    kn = _group_rms(kn_raw, gka_ref[...], bdb)
    kr = _rope_slabs(_group_rms(z_kr, gkr_ref[...], bda), taba, MLA_ROPE // 2)
    ka_ref[...] = (kn + jnp.concatenate([kr] * MLA_HEADS, axis=1)).astype(BF16)
    z_ga = proj((g0, g0 + d_model))
    vat_ref[...] = _with_count_rows(vat, MLA_HEADS)

    qg = _rope_slabs(_group_rms(z_qb, gqg_ref[...], bdb), tabb, GQA_HD // 2) * (GQA_HD ** -0.5 * LOG2E)
    lane = lax.broadcasted_iota(jnp.int32, (1, LANE), 1)
    low = lane < GQA_HD
    parts = []
    for p in range(GQA_HEADS // 2):
        s = qg[:, p * LANE:(p + 1) * LANE]
        parts.append(jnp.where(low, s, 0.0))
        parts.append(jnp.where(low, 0.0, s))
    qg_ref[...] = jnp.concatenate(parts, axis=1).astype(BF16)
    z_gb = proj((g0 + d_model, g0 + 2 * d_model))
    kg = _rope_slabs(_group_rms(z_kb, gkg_ref[...], bdb), tabb, GQA_HD // 2)
    kg_ref[...] = kg.astype(BF16)
    vgt = lax.dot_general(wvbt_ref[...], h, _NT, preferred_element_type=F32)

    ga_ref[...] = _sigmoid(z_ga).astype(BF16)
    vgt_ref[...] = _with_count_rows(vgt, GQA_KV_HEADS)
    gb_ref[...] = _sigmoid(z_gb).astype(BF16)


def _rope_tables(S):
    rows = S // GRID_W
    row = jnp.repeat(jnp.arange(rows, dtype=F32), GRID_W)
    col = jnp.tile(jnp.arange(GRID_W, dtype=F32), rows)

    def cs(rot_dim):
        n_ax = rot_dim // 4
        freqs = 1.0 / (ROPE_THETA ** (jnp.arange(n_ax, dtype=F32) / n_ax))
        ang = jnp.concatenate([row[:, None] * freqs, col[:, None] * freqs], axis=-1)
        return jnp.cos(ang), jnp.sin(ang)

    z = lambda w: jnp.zeros((S, w), F32)
    o = lambda w: jnp.ones((S, w), F32)
    ca, sa = cs(MLA_ROPE)
    pad = LANE - MLA_NOPE - MLA_ROPE
    taba = jnp.stack([
        jnp.concatenate([o(MLA_NOPE), ca, ca, z(pad)], axis=1),
        jnp.concatenate([z(MLA_NOPE), -sa, z(MLA_ROPE // 2), z(pad)], axis=1),
        jnp.concatenate([z(MLA_NOPE), z(MLA_ROPE // 2), sa, z(pad)], axis=1)])
    cb, sb = cs(GQA_HD)
    hz = z(GQA_HD // 2)
    tabb = jnp.stack([
        jnp.concatenate([cb, cb, cb, cb], axis=1),
        jnp.concatenate([-sb, hz, -sb, hz], axis=1),
        jnp.concatenate([hz, sb, hz, sb], axis=1)])
    return taba, tabb


def _block_diag(groups):
    idx = jnp.arange(MXU)
    m = jnp.zeros((MXU, MXU), F32)
    for base in range(0, MXU, LANE):
        for start, size in groups:
            inside = (idx >= base + start) & (idx < base + start + size)
            m = m + jnp.where(inside[:, None] & inside[None, :], 1.0 / size, 0.0)
    return m.astype(BF16)


def _prep_mixer_weights(p):
    d = p["w_in"].shape[0]
    w = p["w_in"]
    o = 0
    cq = w[:, o:o + Q_LORA]; o += Q_LORA
    ckv = w[:, o:o + KV_LORA]; o += KV_LORA
    kr = w[:, o:o + MLA_ROPE]; o += MLA_ROPE
    qb = w[:, o:o + GQA_HEADS * GQA_HD]; o += GQA_HEADS * GQA_HD
    kb = w[:, o:o + GQA_KV_HEADS * GQA_HD]; o += GQA_KV_HEADS * GQA_HD
    vb = w[:, o:o + GQA_KV_HEADS * GQA_HD]; o += GQA_KV_HEADS * GQA_HD
    ga = w[:, o:o + d]; o += d
    gb = w[:, o:o + d]
    zc = lambda n: jnp.zeros((d, n), w.dtype)
    dup = lambda m: jnp.concatenate(
        [m[:, g * GQA_HD:(g + 1) * GQA_HD] for g in range(GQA_KV_HEADS) for _ in range(2)], axis=1)
    win = jnp.concatenate(
        [cq, ckv, zc(MLA_NOPE), kr, zc(LANE - MLA_NOPE - MLA_ROPE), qb, dup(kb), ga, gb], axis=1)

    dq = MLA_NOPE + MLA_ROPE
    wuq = p["mla_w_uq"].reshape(Q_LORA, MLA_HEADS, dq)
    wuq = jnp.pad(wuq, ((0, 0), (0, 0), (0, LANE - dq))).reshape(Q_LORA, _W_SLAB)
    wukv = p["mla_w_ukv"].reshape(KV_LORA, MLA_HEADS, MLA_NOPE + MLA_V)
    wk = jnp.pad(wukv[:, :, :MLA_NOPE], ((0, 0), (0, 0), (0, LANE - MLA_NOPE))).reshape(KV_LORA, _W_SLAB)
    wvt = wukv[:, :, MLA_NOPE:].reshape(KV_LORA, MLA_HEADS * MLA_V).T

    row = lambda v: v.reshape(1, -1).astype(F32)
    zl = lambda n: jnp.zeros((n,), F32)
    gqn, gkn = p["mla_gqn"], p["mla_gkn"]
    pad = LANE - dq
    gqa = jnp.tile(jnp.concatenate([gqn, zl(pad)]), MLA_HEADS)
    gka = jnp.tile(jnp.concatenate([gkn[:MLA_NOPE], zl(LANE - MLA_NOPE)]), MLA_HEADS)
    gkr = jnp.concatenate([zl(MLA_NOPE), gkn[MLA_NOPE:], zl(pad)])
    gqg = jnp.tile(p["gqa_gqn"], GQA_HEADS)
    gkg = jnp.tile(p["gqa_gkn"], 2 * GQA_KV_HEADS)
    return dict(
        gmix=row(p["g_mix"]), win=win.astype(BF16), gq=row(p["mla_gq"]), wuq=wuq.astype(BF16),
        gkv=row(p["mla_gkv"]), wuk=wk.astype(BF16), wuvt=wvt.astype(BF16), wvbt=vb.T.astype(BF16),
        gqa=row(gqa), gka=row(gka), gkr=row(gkr),
        gqg=row(gqg), gkg=row(gkg),
        bda=_block_diag([(0, MLA_NOPE), (MLA_NOPE, MLA_ROPE)]),
        bdb=_block_diag([(0, GQA_HD), (GQA_HD, GQA_HD)]))


def _mixer_in(x2d, w, S, tm):
    n, d = x2d.shape
    tm = _pick(S, tm)
    nt_s = S // tm
    taba, tabb = _rope_tables(S)
    full = lambda a: pl.BlockSpec(a.shape, lambda i: (0,) * a.ndim)
    rows = lambda wd: pl.BlockSpec((tm, wd), lambda i: (i, 0))
    tab = pl.BlockSpec((3, tm, LANE), lambda i: (0, i % nt_s, 0))
    consts = [w[k] for k in ("gmix", "win", "gq", "wuq", "gkv", "wuk", "wuvt", "wvbt", "gqa", "gka", "gkr",
                             "gqg", "gkg", "bda", "bdb")]
    cols = lambda ht: pl.BlockSpec((ht, tm), lambda i: (0, i))
    outs = ((_W_SLAB, False), (_W_SLAB, False), (MLA_HEADS * _VROWS, True), (_W_SLAB, False),
            (GQA_KV_HEADS * LANE, False), (GQA_KV_HEADS * _VROWS, True), (d, False), (d, False))
    return pl.pallas_call(
        functools.partial(_mixer_in_kernel, d_model=d),
        grid=(n // tm,),
        in_specs=[rows(d)] + [full(c) for c in consts] + [tab, tab],
        out_specs=[cols(wd) if t else rows(wd) for wd, t in outs],
        out_shape=[jax.ShapeDtypeStruct((wd, n) if t else (n, wd), BF16) for wd, t in outs],
        compiler_params=pltpu.CompilerParams(dimension_semantics=("parallel",), vmem_limit_bytes=VMEM_LIMIT),
        name="mixer_in",
    )(x2d, *consts, taba, tabb)


def _attn_kernel(q_ref, k_ref, vt_ref, o_ref, m_scr, mp_scr, acc_scr, s_scr, *, k_slab, v_row, tk, hd):
    nheads = len(k_slab)
    nk = k_ref.shape[0] // tk
    acc_scr[...] = jnp.zeros(acc_scr.shape, F32)

    def scores(h, j, slot, m_run):
        q = q_ref[:, h * LANE:(h + 1) * LANE]
        k = k_ref[pl.ds(j * tk, tk), k_slab[h] * LANE:(k_slab[h] + 1) * LANE]
        s = lax.dot_general(k, q, _NT, preferred_element_type=F32)
        s_scr[slot, h] = s
        mx = jnp.max(s, axis=0, keepdims=True)
        return mx if m_run is None else jnp.maximum(m_run, mx)

    def consume(h, j, slot):
        m_new = m_scr[h]
        alpha = jnp.exp2(mp_scr[h] - m_new)
        p = jnp.exp2(s_scr[slot, h] - m_new)
        vt = vt_ref[v_row[h] * _VROWS:(v_row[h] + 1) * _VROWS, pl.ds(j * tk, tk)]
        acc_scr[h] = alpha * acc_scr[h] + jnp.dot(vt, p.astype(BF16), preferred_element_type=F32)

    for h in range(nheads):
        m0 = scores(h, 0, 0, None)
        m_scr[h] = m0
        mp_scr[h] = m0

    for j in range(nk - 1):
        slot = j % 2
        for h in range(nheads):
            m_cur = m_scr[h]
            m_next = scores(h, j + 1, 1 - slot, m_cur)
            consume(h, j, slot)
            mp_scr[h] = m_cur
            m_scr[h] = m_next
    for h in range(nheads):
        consume(h, nk - 1, (nk - 1) % 2)
    out_t = jnp.concatenate([acc_scr[h, 0:hd, :] / acc_scr[h, hd:hd + 1, :] for h in range(nheads)], axis=0)
    o_ref[...] = out_t.T.astype(o_ref.dtype)


def _attention(q, k, vt, B, S, k_slab, v_row, tq, tk):
    n = q.shape[0]
    nheads = len(k_slab)
    hd = LANE // 2
    tq, tk = _pick(S, tq), _pick(S, tk)
    nq = S // tq
    return pl.pallas_call(
        functools.partial(_attn_kernel, k_slab=tuple(k_slab), v_row=tuple(v_row), tk=tk, hd=hd),
        grid=(B, nq),
        in_specs=[pl.BlockSpec((tq, q.shape[1]), lambda b, i: (b * nq + i, 0)),
                  pl.BlockSpec((S, k.shape[1]), lambda b, i: (b, 0)),
                  pl.BlockSpec((vt.shape[0], S), lambda b, i: (0, b))],
        out_specs=pl.BlockSpec((tq, nheads * hd), lambda b, i: (b * nq + i, 0)),
        out_shape=jax.ShapeDtypeStruct((n, nheads * hd), BF16),
        scratch_shapes=[pltpu.VMEM((nheads, 1, tq), F32), pltpu.VMEM((nheads, 1, tq), F32),
                        pltpu.VMEM((nheads, _VROWS, tq), F32), pltpu.VMEM((2, nheads, tk, tq), F32)],
        compiler_params=pltpu.CompilerParams(
            dimension_semantics=("parallel", "arbitrary"), vmem_limit_bytes=VMEM_LIMIT),
        name="attention",
    )(q, k, vt)


def _mem_kv_kernel(mem_ref, gmem_ref, wkv_ref, gkn_ref, kx_ref, vx_ref):
    m = _rms_rows(mem_ref[...], gmem_ref[...]).astype(BF16)
    kv = jnp.dot(m, wkv_ref[...], preferred_element_type=F32)
    w = XA_HEADS * XA_HD
    g = gkn_ref[...]
    parts = [_rms_rows(kv[:, h * XA_HD:(h + 1) * XA_HD], g) for h in range(XA_HEADS)]
    kx_ref[...] = jnp.concatenate(parts, axis=1).astype(BF16)
    vx_ref[...] = kv[:, w:].astype(BF16)


def _mem_kv(mem2d, gmem, wkv, gkn, tm):
    n, d = mem2d.shape
    tm = _pick(n, tm)
    w = XA_HEADS * XA_HD
    full = lambda a: pl.BlockSpec(a.shape, lambda i: (0,) * a.ndim)
    return pl.pallas_call(
        _mem_kv_kernel,
        grid=(n // tm,),
        in_specs=[pl.BlockSpec((tm, d), lambda i: (i, 0)), full(gmem), full(wkv), full(gkn)],
        out_specs=[pl.BlockSpec((tm, w), lambda i: (i, 0))] * 2,
        out_shape=[jax.ShapeDtypeStruct((n, w), BF16)] * 2,
        compiler_params=pltpu.CompilerParams(dimension_semantics=("parallel",), vmem_limit_bytes=VMEM_LIMIT),
        name="mem_kv",
    )(mem2d, gmem, wkv, gkn)


def _post_kernel(x_ref, oa_ref, ob_ref, ga_ref, gb_ref, wbra_ref, wbrb_ref, wmix_ref, gcross_ref, wq_ref,
                 gqn_ref, kx_ref, vx_ref, wo_ref, gmoe_ref, wrhi_ref, wrlo_ref, x2_ref, hm_ref, aff_ref):
    ba = jnp.dot(oa_ref[...], wbra_ref[...], preferred_element_type=F32)
    bb = jnp.dot(ob_ref[...], wbrb_ref[...], preferred_element_type=F32)
    mixed = ga_ref[...].astype(F32) * ba + gb_ref[...].astype(F32) * bb
    x1 = x_ref[...] + jnp.dot(mixed.astype(BF16), wmix_ref[...], preferred_element_type=F32)

    hc = _rms_rows(x1, gcross_ref[...]).astype(BF16)
    q = jnp.dot(hc, wq_ref[...], preferred_element_type=F32)
    gqn = gqn_ref[...]
    sls = [slice(h * XA_HD, (h + 1) * XA_HD) for h in range(XA_HEADS)]
    qhs = [(_rms_rows(q[:, sl], gqn) * (XA_HD ** -0.5)).astype(BF16) for sl in sls]
    ss = [lax.dot_general(qh, kx_ref[:, sl], _NT, preferred_element_type=F32) for qh, sl in zip(qhs, sls)]
    ps = [jnp.exp(s - jnp.max(s, axis=1, keepdims=True)) for s in ss]
    pvs = [jnp.dot(p.astype(BF16), vx_ref[:, sl], preferred_element_type=F32) for p, sl in zip(ps, sls)]
    outs = [pv / jnp.sum(p, axis=1, keepdims=True) for pv, p in zip(pvs, ps)]
    ox = jnp.concatenate(outs, axis=1).astype(BF16)
    x2 = x1 + jnp.dot(ox, wo_ref[...], preferred_element_type=F32)
    x2_ref[...] = x2

    hm = _rms_rows(x2, gmoe_ref[...])
    hi = hm.astype(BF16)
    lo = (hm - hi.astype(F32)).astype(BF16)
    hm_ref[...] = hi
    dn = (((1,), (1,)), ((), ()))
    wrhi = wrhi_ref[...]
    logits = (lax.dot_general(wrhi, hi, dn, preferred_element_type=F32)
              + lax.dot_general(wrhi, lo, dn, preferred_element_type=F32)
              + lax.dot_general(wrlo_ref[...], hi, dn, preferred_element_type=F32))
    e = jnp.exp(logits - jnp.max(logits, axis=0, keepdims=True))
    aff_ref[...] = e / jnp.sum(e, axis=0, keepdims=True)


def _post(x2d, oa, ob, ga, gb, kx, vx, w, S, n_mem, tm):
    n, d = x2d.shape
    tm = _pick(S, tm)
    nt_s = S // tm
    ne = w["wrhi"].shape[0]
    full = lambda a: pl.BlockSpec(a.shape, lambda i: (0,) * a.ndim)
    rows = lambda wd: pl.BlockSpec((tm, wd), lambda i: (i, 0))
    memblk = pl.BlockSpec((n_mem, kx.shape[1]), lambda i: (i // nt_s, 0))
    c = [w[k] for k in ("wbra", "wbrb", "wmix", "gcross", "wq", "gqn")]
    c2 = [w[k] for k in ("wo", "gmoe", "wrhi", "wrlo")]
    return pl.pallas_call(
        _post_kernel,
        grid=(n // tm,),
        in_specs=[rows(d), rows(oa.shape[1]), rows(ob.shape[1]), rows(d), rows(d)] + [full(a) for a in c]
                 + [memblk, memblk] + [full(a) for a in c2],
        out_specs=[rows(d), rows(d), pl.BlockSpec((ne, tm), lambda i: (0, i))],
        out_shape=[jax.ShapeDtypeStruct((n, d), F32), jax.ShapeDtypeStruct((n, d), BF16),
                   jax.ShapeDtypeStruct((ne, n), F32)],
        compiler_params=pltpu.CompilerParams(dimension_semantics=("parallel",), vmem_limit_bytes=VMEM_LIMIT),
        name="post_mix",
    )(x2d, oa, ob, ga, gb, *c, kx, vx, *c2)


_SEL_CHUNK = 2048
_TIE_CHUNK = 512


def _topc_kernel(aff_ref, tri_ref, wsel_ref, pos_ref, *, cap):
    ne, n = aff_ref.shape
    chunk = min(_SEL_CHUNK, n)
    nchunks = n // chunk

    def bits_at(c, width):
        return pltpu.bitcast(aff_ref[:, pl.ds(pl.multiple_of(c * width, width), width)], jnp.int32)

    def count(pred_fn):
        def body(c, acc):
            return acc + pred_fn(bits_at(c, chunk)).astype(jnp.int32)
        acc = lax.fori_loop(0, nchunks, body, jnp.zeros((ne, chunk), jnp.int32))
        return jnp.sum(acc, axis=1, keepdims=True)

    def bit_step(i, t):
        cand = t | jnp.left_shift(jnp.int32(1), 30 - i)
        return jnp.where(count(lambda b: b >= cand) >= cap, cand, t)

    thr = lax.fori_loop(0, 31, bit_step, jnp.zeros((ne, 1), jnp.int32))
    need = (cap - count(lambda b: b > thr)).astype(F32)

    tchunk = min(_TIE_CHUNK, n)
    tri = tri_ref[...]

    def tie_step(c, carry):
        run_eq, run_sel = carry
        sl = pl.ds(pl.multiple_of(c * tchunk, tchunk), tchunk)
        a = aff_ref[:, sl]
        b = pltpu.bitcast(a, jnp.int32)
        eq = b == thr
        eqf = jnp.where(eq, 1.0, 0.0)
        before = run_eq + jnp.dot(eqf.astype(BF16), tri, preferred_element_type=F32)
        sel = (b > thr) | (eq & (before < need))
        self_ = jnp.where(sel, 1.0, 0.0)
        rank = run_sel + jnp.dot(self_.astype(BF16), tri, preferred_element_type=F32)
        wsel_ref[:, sl] = jnp.where(sel, a, 0.0)
        pos_ref[:, sl] = jnp.where(sel, rank, -1.0).astype(jnp.int32)
        return (run_eq + jnp.sum(eqf, axis=1, keepdims=True), run_sel + jnp.sum(self_, axis=1, keepdims=True))

    zero = jnp.zeros((ne, 1), F32)
    lax.fori_loop(0, n // tchunk, tie_step, (zero, zero))


def _topc(aff_t, cap):
    ne, n = aff_t.shape
    tchunk = min(_TIE_CHUNK, n)
    idx = jnp.arange(tchunk)
    tri = (idx[:, None] < idx[None, :]).astype(BF16)
    return pl.pallas_call(
        functools.partial(_topc_kernel, cap=cap),
        out_shape=[jax.ShapeDtypeStruct((ne, n), F32), jax.ShapeDtypeStruct((ne, n), jnp.int32)],
        compiler_params=pltpu.CompilerParams(vmem_limit_bytes=VMEM_LIMIT),
        name="expert_choice_select",
    )(aff_t, tri)


_MOE_SUB = 256
_MOE_TILE = 512
_MOE_GROUP = 4
_MOE_STEP = 2048
_GRANULE = 16
_MOE_SHORT_WIN = 64


def _moe_ffn_kernel(off_ref, hm_ref, pos_ref, w_ref, wg_ref, wu_ref, wd_ref, *rest, nsub, tr, nc1, eb, tiles):
    o_hbm, (stage, vstage, obuf, sems) = rest[:eb], rest[eb:]
    grp = pl.program_id(0)
    c = pl.program_id(1)
    sub = _MOE_SUB
    win = sub + _GRANULE
    per_flush = tr // sub

    @pl.when(c == 0)
    def _():
        stage[...] = jnp.zeros(stage.shape, stage.dtype)
        vstage[...] = jnp.zeros(vstage.shape, vstage.dtype)

    bases = [(grp * eb + j) * nc1 + c * nsub for j in range(eb)]

    def tile_copy(j, tile, slot):
        dst = o_hbm[j].at[pl.ds(pl.multiple_of((grp * tiles + tile) * tr, tr), tr), :]
        return pltpu.make_async_copy(obuf.at[j, slot], dst, sems.at[j, slot])

    def append(i, flushed, k16s, window):
        cols = slice(i * sub, (i + 1) * sub)
        riota = lax.broadcasted_iota(jnp.int32, (window, sub), 0)
        hits = [(pos_ref[j, :, cols] - (flushed[j] + k16s[j])) == riota for j in range(eb)]
        onehot = jnp.concatenate([jnp.where(h, 1.0, 0.0).astype(BF16) for h in hits], axis=0)
        x = jnp.dot(onehot, hm_ref[cols, :], preferred_element_type=F32)
        for j in range(eb):
            rows = pl.ds(k16s[j], window)
            xj = x[j * window:(j + 1) * window, :].astype(BF16)
            stage[j, rows, :] = stage[j, rows, :] + xj
            v = jnp.sum(jnp.where(hits[j], w_ref[j, :, cols], 0.0), axis=1, keepdims=True)
            vstage[j, rows, :] = vstage[j, rows, :] + v

    for part in range(nsub // per_flush):
        first = part * per_flush
        flushed = [(off_ref[b + first] // tr) * tr for b in bases]
        for i in range(first, first + per_flush):
            k16s, widest = [], None
            for j in range(eb):
                k = off_ref[bases[j] + i] - flushed[j]
                k16 = pl.multiple_of((k // _GRANULE) * _GRANULE, _GRANULE)
                span = off_ref[bases[j] + i + 1] - flushed[j] - k16
                k16s.append(k16)
                widest = span if widest is None else jnp.maximum(widest, span)
            lax.cond(widest <= _MOE_SHORT_WIN,
                     functools.partial(append, i, flushed, k16s, _MOE_SHORT_WIN),
                     functools.partial(append, i, flushed, k16s, win))

        for j in range(eb):
            @pl.when(off_ref[bases[j] + first + per_flush] - flushed[j] >= tr)
            def _(j=j):
                tile = flushed[j] // tr
                slot = tile % 2

                @pl.when(tile >= 2)
                def _():
                    tile_copy(j, tile - 2, slot).wait()

                xs = stage[j, 0:tr, :]
                g = jnp.dot(xs, wg_ref[j], preferred_element_type=F32)
                u = jnp.dot(xs, wu_ref[j], preferred_element_type=F32)
                hid = (g * _sigmoid(g) * u).astype(BF16)
                out = jnp.dot(hid, wd_ref[j], preferred_element_type=F32)
                obuf[j, slot] = (out * vstage[j, 0:tr, :]).astype(obuf.dtype)
                tile_copy(j, tile, slot).start()
                keep = stage.shape[1] - tr
                stage[j, 0:keep, :] = stage[j, tr:, :]
                stage[j, keep:, :] = jnp.zeros((tr, stage.shape[2]), stage.dtype)
                vstage[j, 0:keep, :] = vstage[j, tr:, :]
                vstage[j, keep:, :] = jnp.zeros((tr, 1), vstage.dtype)

    @pl.when(c == pl.num_programs(1) - 1)
    def _():
        for j in range(eb):
            for tile in range(max(tiles - 2, 0), tiles):
                tile_copy(j, tile, tile % 2).wait()


def _moe_ffn(off, hm, pos3, wsel3, wg, wu, wd, cap, tr, nc1, eb):
    n, d = hm.shape
    ne, _, ff = wg.shape
    ts = min(_MOE_STEP, n)
    assert ts % tr == 0 and n % ts == 0, (n, ts, tr)
    nsub = ts // _MOE_SUB
    tiles = cap // tr
    once = pl.Buffered(1)
    grid_spec = pltpu.PrefetchScalarGridSpec(
        num_scalar_prefetch=1,
        grid=(ne // eb, n // ts),
        in_specs=[pl.BlockSpec((ts, d), lambda g, c, off: (c, 0)),
                  pl.BlockSpec((eb, 1, ts), lambda g, c, off: (g, 0, c)),
                  pl.BlockSpec((eb, 1, ts), lambda g, c, off: (g, 0, c)),
                  pl.BlockSpec((eb, d, ff), lambda g, c, off: (g, 0, 0), pipeline_mode=once),
                  pl.BlockSpec((eb, d, ff), lambda g, c, off: (g, 0, 0), pipeline_mode=once),
                  pl.BlockSpec((eb, ff, d), lambda g, c, off: (g, 0, 0), pipeline_mode=once)],
        out_specs=[pl.BlockSpec(memory_space=pl.ANY)] * eb,
        scratch_shapes=[pltpu.VMEM((eb, 2 * tr + _GRANULE, d), BF16),
                        pltpu.VMEM((eb, 2 * tr + _GRANULE, 1), F32),
                        pltpu.VMEM((eb, 2, tr, d), BF16),
                        pltpu.SemaphoreType.DMA((eb, 2))])
    return pl.pallas_call(
        functools.partial(_moe_ffn_kernel, nsub=nsub, tr=tr, nc1=nc1, eb=eb, tiles=tiles),
        grid_spec=grid_spec,
        out_shape=[jax.ShapeDtypeStruct((ne // eb * cap, d), BF16)] * eb,
        compiler_params=pltpu.CompilerParams(
            dimension_semantics=("arbitrary", "arbitrary"), vmem_limit_bytes=VMEM_LIMIT),
        name="moe_ffn",
    )(off, hm, pos3, wsel3, wg, wu, wd)


def _moe_combine_kernel(off_ref, pos_ref, x2_ref, *rest, ne, nc1, cap, eb):
    rows_hbm, (y_ref, gbuf, st_scr, sems) = rest[:eb], rest[eb:]
    c = pl.program_id(0)
    nc = pl.num_programs(0)
    sub = x2_ref.shape[0]
    gran = _GRANULE
    slot = c % 2

    def granule_copy(e, src_row, dst_row, sl):
        return pltpu.make_async_copy(rows_hbm[e % eb].at[pl.ds(src_row, gran), :],
                                     gbuf.at[sl, pl.ds(dst_row, gran), :], sems.at[sl])

    def layout(cc):
        gb = jnp.int32(0)
        shifts, starts, firsts, counts = [], [gb], [], []
        for e in range(ne):
            a = off_ref[e * nc1 + cc]
            b = off_ref[e * nc1 + cc + 1]
            a16 = (a // gran) * gran
            ng = jnp.where(b > a, (b - a16 + gran - 1) // gran, 0)
            shifts.append(gb - a16)
            firsts.append((e // eb) * cap + a16)
            counts.append(ng)
            gb = gb + ng * gran
            starts.append(gb)
        return shifts, starts, firsts, counts

    def fetch(cc, sl):
        _, starts, firsts, counts = layout(cc)
        for e in range(ne):
            def issue(g, carry, e=e):
                granule_copy(e, pl.multiple_of(firsts[e] + g * gran, gran),
                             pl.multiple_of(starts[e] + g * gran, gran), sl).start()
                return carry
            lax.fori_loop(0, counts[e], issue, 0)

    @pl.when(c == 0)
    def _():
        gbuf[...] = jnp.zeros(gbuf.shape, gbuf.dtype)
        fetch(c, slot)

    @pl.when(c + 1 < nc)
    def _():
        fetch(c + 1, 1 - slot)

    shifts, starts, _, _ = layout(c)
    gb = starts[ne]

    def wait_one(g, carry):
        granule_copy(0, 0, 0, slot).wait()
        return carry

    lax.fori_loop(0, gb // gran, wait_one, 0)

    y_ref[...] = x2_ref[...]
    blk = MXU

    def kblock(kb, carry):
        r0 = pl.multiple_of(kb * blk, blk)
        riota = lax.broadcasted_iota(jnp.int32, (blk, sub), 0) + r0
        st_scr[...] = jnp.zeros(st_scr.shape, F32)
        for e in range(ne):
            @pl.when((starts[e] < r0 + blk) & (starts[e + 1] > r0))
            def _(e=e):
                p = pos_ref[e]
                st_scr[...] = jnp.where(((p + shifts[e]) == riota) & (p >= 0), 1.0, st_scr[...])
        y_ref[...] += jnp.dot(st_scr[...].T.astype(BF16), gbuf[slot, pl.ds(r0, blk), :],
                              preferred_element_type=F32)
        return carry

    lax.fori_loop(0, (gb + blk - 1) // blk, kblock, 0)


def _moe_combine(off, pos3, x2, rows, cap, nc1):
    eb = len(rows)
    n, d = x2.shape
    ne = pos3.shape[0]
    sub = _MOE_SUB
    gmax = ne * (sub + 2 * _GRANULE)
    gmax = (gmax + MXU - 1) // MXU * MXU
    grid_spec = pltpu.PrefetchScalarGridSpec(
        num_scalar_prefetch=1,
        grid=(n // sub,),
        in_specs=[pl.BlockSpec((ne, 1, sub), lambda c, off: (0, 0, c)),
                  pl.BlockSpec((sub, d), lambda c, off: (c, 0))]
                 + [pl.BlockSpec(memory_space=pl.ANY)] * eb,
        out_specs=pl.BlockSpec((sub, d), lambda c, off: (c, 0)),
        scratch_shapes=[pltpu.VMEM((2, gmax, d), BF16), pltpu.VMEM((MXU, sub), F32),
                        pltpu.SemaphoreType.DMA((2,))])
    return pl.pallas_call(
        functools.partial(_moe_combine_kernel, ne=ne, nc1=nc1, cap=cap, eb=eb),
        grid_spec=grid_spec,
        out_shape=jax.ShapeDtypeStruct((n, d), F32),
        compiler_params=pltpu.CompilerParams(dimension_semantics=("arbitrary",), vmem_limit_bytes=VMEM_LIMIT),
        name="moe_combine",
    )(off, pos3, x2, *rows)


def _moe(hm, x2, aff_t, wg, wu, wd):
    n, d = hm.shape
    ne = aff_t.shape[0]
    cap = EC_FACTOR * n // ne
    tr = min(_MOE_TILE, cap)
    assert cap % tr == 0 and n % tr == 0 and tr % _MOE_SUB == 0, (n, cap, tr)
    wsel, pos = _topc(aff_t, cap)
    nc = n // _MOE_SUB
    cnt = jnp.sum((pos >= 0).reshape(ne, nc, _MOE_SUB), axis=-1, dtype=jnp.int32)
    off = jnp.concatenate([jnp.zeros((ne, 1), jnp.int32), jnp.cumsum(cnt, axis=1, dtype=jnp.int32)], axis=1)
    off = off.reshape(-1)
    pos3 = pos.reshape(ne, 1, n)
    rows = _moe_ffn(off, hm, pos3, wsel.reshape(ne, 1, n), wg, wu, wd, cap, tr, nc + 1, _MOE_GROUP)
    return _moe_combine(off, pos3, x2, rows, cap, nc + 1)


def _prep_post_weights(p):
    row = lambda v: v.reshape(1, -1).astype(F32)
    wr = p["w_router"].T.astype(F32)
    wrhi = wr.astype(BF16)
    wrlo = (wr - wrhi.astype(F32)).astype(BF16)
    return dict(
        wbra=p["mla_w_br"].astype(BF16), wbrb=p["gqa_w_br"].astype(BF16), wmix=p["w_mix_out"].astype(BF16),
        gcross=row(p["g_cross"]), wq=p["xa_w_q"].astype(BF16), gqn=row(p["xa_gqn"]),
        wo=p["xa_w_o"].astype(BF16), gmoe=row(p["g_moe"]), wrhi=wrhi, wrlo=wrlo,
        gmem=row(p["g_mem"]), wkv=p["xa_w_kv"].astype(BF16), gkn=row(p["xa_gkn"]),
        wg=p["moe_w_gate"].astype(BF16), wu=p["moe_w_up"].astype(BF16), wd=p["moe_w_down"].astype(BF16))


def _encoder_layer(x, mem, wm, wp):
    B, S, D = x.shape
    n = B * S
    n_mem = mem.shape[1]
    x2d = x.reshape(n, D)
    qa, ka, va, qg, kg, vg, ga, gb = _mixer_in(x2d, wm, S, tm=TM_MIXER)
    oa = _attention(qa, ka, va, B, S, k_slab=range(MLA_HEADS), v_row=range(MLA_HEADS), tq=TQ_ATTN, tk=TK_ATTN)
    rep = GQA_HEADS // GQA_KV_HEADS
    ob = _attention(qg, kg, vg, B, S, k_slab=[h // rep for h in range(GQA_HEADS)],
                    v_row=[h // rep for h in range(GQA_HEADS)], tq=TQ_ATTN, tk=TK_ATTN)
    kx, vx = _mem_kv(mem.reshape(B * n_mem, D), wp["gmem"], wp["wkv"], wp["gkn"], tm=TM_MEM)
    x2, hm, aff_t = _post(x2d, oa, ob, ga, gb, kx, vx, wp, S, n_mem, tm=TM_POST)
    out = _moe(hm, x2, aff_t, wp["wg"], wp["wu"], wp["wd"])
    return out.reshape(B, S, D)


def kernel(x_prompt, x_sample, mem_prompt, mem_sample, g_mix, w_in, mla_gq, mla_w_uq, mla_gkv, mla_w_ukv, mla_gqn, mla_gkn, mla_w_br, gqa_gqn, gqa_gkn, gqa_w_br, w_mix_out, g_cross, g_mem, xa_w_q, xa_w_kv, xa_gqn, xa_gkn, xa_w_o, g_moe, w_router, moe_w_gate, moe_w_up, moe_w_down):
    names = ("g_mix", "w_in", "mla_gq", "mla_w_uq", "mla_gkv", "mla_w_ukv", "mla_gqn", "mla_gkn", "mla_w_br",
             "gqa_gqn", "gqa_gkn", "gqa_w_br", "w_mix_out", "g_cross", "g_mem", "xa_w_q", "xa_w_kv", "xa_gqn",
             "xa_gkn", "xa_w_o", "g_moe", "w_router", "moe_w_gate", "moe_w_up", "moe_w_down")
    vals = (g_mix, w_in, mla_gq, mla_w_uq, mla_gkv, mla_w_ukv, mla_gqn, mla_gkn, mla_w_br,
            gqa_gqn, gqa_gkn, gqa_w_br, w_mix_out, g_cross, g_mem, xa_w_q, xa_w_kv, xa_gqn,
            xa_gkn, xa_w_o, g_moe, w_router, moe_w_gate, moe_w_up, moe_w_down)
    y_prompt, y_sample = x_prompt, x_sample
    depth = w_in.shape[0]
    for l in range(depth):
        p = {k: v[l] for k, v in zip(names, vals)}
        wm = _prep_mixer_weights(p)
        wp = _prep_post_weights(p)
        y_prompt = _encoder_layer(y_prompt, mem_prompt, wm, wp)
        y_sample = _encoder_layer(y_sample, mem_sample, wm, wp)
    return (y_prompt, y_sample)
```
